```python
import jax, jax.numpy as jnp
from jax import lax
import numpy as np

D_MODEL = 2048
BATCH = 8
SEQ = 2048
DEPTH = 4

BRANCH_W = 1024
N_BRANCH = 3
RET_HEADS = 8
RET_DK = 128
RET_DV = BRANCH_W // RET_HEADS
RET_CHUNK = 128
SB_HEADS = 8
SB_DH = BRANCH_W // SB_HEADS
SB_BLOCK = 128
MLA_HEADS = 8
MLA_Q_LORA = 512
MLA_KV_LORA = 512
MLA_NOPE = 128
MLA_ROPE = 64
MLA_V = BRANCH_W // MLA_HEADS
MLA_BLOCK = 128
ROPE_BASE = 10000.0
N_GROUPS = 4
EXPERTS_PER_GROUP = 8
N_EXPERTS = N_GROUPS * EXPERTS_PER_GROUP
TOP_K_INNER = 2
D_EXPERT = 512
MOE_BLOCK = 128
NORM_EPS = 1e-5
DEEPNORM_ALPHA = (2 * DEPTH) ** 0.25
DEEPNORM_BETA = (8 * DEPTH) ** -0.25
IN_WIDTHS = (
    RET_HEADS * RET_DK, RET_HEADS * RET_DK, BRANCH_W, BRANCH_W,
    BRANCH_W, BRANCH_W, BRANCH_W,
    MLA_Q_LORA, MLA_KV_LORA, MLA_ROPE,
    N_BRANCH * D_MODEL,
)
IN_COLS = int(sum(IN_WIDTHS))
IN_SPLIT_IDX = tuple(int(i) for i in np.cumsum(IN_WIDTHS)[:-1])

kernel_name = 'hybrid_ret_sb_mla_hmoe_deepnorm'


def layer_norm(x, g, b):
    xf = x.astype(jnp.float32)
    mu = jnp.mean(xf, -1, keepdims=True)
    var = jnp.mean(jnp.square(xf - mu), -1, keepdims=True)
    return ((xf - mu) * lax.rsqrt(var + NORM_EPS) * g + b).astype(x.dtype)


def rms_norm(x, g):
    xf = x.astype(jnp.float32)
    y = xf * lax.rsqrt(jnp.mean(xf * xf, -1, keepdims=True) + NORM_EPS)
    return (y * g).astype(x.dtype)


def rope_angles(positions, dim):
    inv = 1.0 / (ROPE_BASE ** (jnp.arange(0, dim, 2, dtype=jnp.float32) / dim))
    ang = positions.astype(jnp.float32)[..., None] * inv
    return jnp.cos(ang), jnp.sin(ang)


def apply_rope(x, cos, sin):
    x1, x2 = jnp.split(x.astype(jnp.float32), 2, axis=-1)
    c = cos[:, :, None, :]
    s = sin[:, :, None, :]
    return jnp.concatenate([x1 * c - x2 * s, x1 * s + x2 * c], -1).astype(x.dtype)


def retention(q, k, v):
    B, S, H, DK = q.shape
    DV = v.shape[-1]
    C = RET_CHUNK
    NC = S // C
    f32 = jnp.float32
    log_gamma = jnp.log1p(-jnp.exp2(-5.0 - jnp.arange(H, dtype=f32)))

    def chunks(t):
        return t.astype(f32).reshape(B, NC, C, H, -1).transpose(0, 3, 1, 2, 4)

    qc = chunks(q)
    kc = chunks(k) * (DK ** -0.5)
    vc = chunks(v)
    pos = jnp.arange(C, dtype=f32)
    rel = pos[:, None] - pos[None, :]
    intra = jnp.where(rel >= 0, jnp.exp(log_gamma[:, None, None] * jnp.maximum(rel, 0.0)), 0.0)
    scores = jnp.einsum('bhnid,bhnjd->bhnij', qc, kc) * intra[None, :, None]
    o_intra = jnp.einsum('bhnij,bhnje->bhnie', scores, vc)
    k_dec = kc * jnp.exp(log_gamma[:, None] * (C - 1 - pos))[None, :, None, :, None]
    kv_inc = jnp.einsum('bhnjd,bhnje->nbhde', k_dec, vc)
    chunk_decay = jnp.exp(log_gamma * C)[None, :, None, None]

    def step(state, inc):
        return state * chunk_decay + inc, state

    _, prev = lax.scan(step, jnp.zeros((B, H, DK, DV), f32), kv_inc)
    q_dec = qc * jnp.exp(log_gamma[:, None] * (pos + 1.0))[None, :, None, :, None]
    o_cross = jnp.einsum('bhnid,nbhde->bhnie', q_dec, prev)
    o = (o_intra + o_cross).transpose(0, 2, 3, 1, 4).reshape(B, S, H, DV)
    return o * lax.rsqrt(jnp.mean(o * o, -1, keepdims=True) + NORM_EPS)


def stick_breaking(q, k, v):
    B, S, H, D = q.shape
    scale = D ** -0.5
    qh = q.transpose(0, 2, 1, 3)
    kh = k.transpose(0, 2, 1, 3)
    vh = v.transpose(0, 2, 1, 3)
    outs = []
    for blk in range(S // SB_BLOCK):
        t0 = blk * SB_BLOCK
        t1 = t0 + SB_BLOCK
        z = jnp.einsum('bhtd,bhsd->bhts', qh[:, :, t0:t1], kh[:, :, :t1]).astype(jnp.float32) * scale
        t_idx = t0 + jnp.arange(SB_BLOCK)[:, None]
        s_idx = jnp.arange(t1)[None, :]
        strict = s_idx < t_idx
        log_keep = jnp.where(strict, jax.nn.log_sigmoid(-z), 0.0)
        after = lax.cumsum(log_keep, axis=3, reverse=True) - log_keep
        w = jnp.where(strict, jnp.exp(jax.nn.log_sigmoid(z) + after), 0.0)
        outs.append(jnp.einsum('bhts,bhsd->bhtd', w.astype(v.dtype), vh[:, :, :t1]))
    o = jnp.concatenate(outs, axis=2)
    return o.transpose(0, 2, 1, 3).reshape(B, S, H * D)


def mla(q_lat, kv_lat, k_rope, q_norm, w_q_b, kv_norm, w_kv_b, cos, sin):
    B, S, _ = q_lat.shape
    q = (rms_norm(q_lat, q_norm) @ w_q_b).reshape(B, S, MLA_HEADS, MLA_NOPE + MLA_ROPE)
    q_nope, q_pe = q[..., :MLA_NOPE], apply_rope(q[..., MLA_NOPE:], cos, sin)
    kv = (rms_norm(kv_lat, kv_norm) @ w_kv_b).reshape(B, S, MLA_HEADS, MLA_NOPE + MLA_V)
    k_nope, v = kv[..., :MLA_NOPE], kv[..., MLA_NOPE:]
    k_pe = apply_rope(k_rope[:, :, None, :], cos, sin)[:, :, 0, :]
    scale = (MLA_NOPE + MLA_ROPE) ** -0.5
    neg = jnp.finfo(jnp.float32).min
    outs = []
    for blk in range(S // MLA_BLOCK):
        t0 = blk * MLA_BLOCK
        t1 = t0 + MLA_BLOCK
        s = (jnp.einsum('bthd,bshd->bhts', q_nope[:, t0:t1], k_nope[:, :t1])
             + jnp.einsum('bthr,bsr->bhts', q_pe[:, t0:t1], k_pe[:, :t1])).astype(jnp.float32) * scale
        causal = jnp.arange(t1)[None, :] <= (t0 + jnp.arange(MLA_BLOCK))[:, None]
        p = jax.nn.softmax(jnp.where(causal, s, neg), axis=-1)
        outs.append(jnp.einsum('bhts,bshd->bthd', p.astype(v.dtype), v[:, :t1]))
    return jnp.concatenate(outs, axis=1).reshape(B, S, MLA_HEADS * MLA_V)


def mixer(x, cos_r, sin_r, cos_m, sin_m, w_in, q_norm, w_q_b, kv_norm, w_kv_b, w_branch, w_out):
    B, S, D = x.shape
    proj = x @ w_in
    rq, rk, rv, rg, sq, sk, sv, mq, mkv, mkr, gates = jnp.split(proj, IN_SPLIT_IDX, axis=-1)

    def heads(t, h):
        return t.reshape(B, S, h, -1)

    ret = retention(apply_rope(heads(rq, RET_HEADS), cos_r, sin_r),
                    apply_rope(heads(rk, RET_HEADS), cos_r, sin_r), heads(rv, RET_HEADS))
    ret = (ret * jax.nn.silu(heads(rg, RET_HEADS).astype(jnp.float32))).reshape(B, S, BRANCH_W).astype(x.dtype)
    sb = stick_breaking(heads(sq, SB_HEADS), heads(sk, SB_HEADS), heads(sv, SB_HEADS))
    ml = mla(mq, mkv, mkr, q_norm, w_q_b, kv_norm, w_kv_b, cos_m, sin_m)
    branches = jnp.einsum('bsnc,nce->bsne', jnp.stack([ret, sb, ml], axis=2), w_branch)
    g = jax.nn.sigmoid(gates.astype(jnp.float32)).reshape(B, S, N_BRANCH, D).astype(x.dtype)
    merged = jnp.sum(g * branches, axis=2)
    return merged @ w_out


def routed_expert_ffn(xt, eid, gate, w1, w3, w2):
    N, D = xt.shape
    A = eid.shape[0]
    NE = w1.shape[0]
    bs = MOE_BLOCK
    P = A + NE * bs
    tok = jnp.arange(A, dtype=jnp.int32) // TOP_K_INNER
    order = jnp.argsort(eid)
    e_sorted = eid[order]
    counts = jnp.bincount(eid, length=NE)
    padded = (counts + bs - 1) // bs * bs
    start = jnp.cumsum(counts) - counts
    pend = jnp.cumsum(padded)
    pstart = pend - padded
    dest = pstart[e_sorted] + jnp.arange(A, dtype=jnp.int32) - start[e_sorted]
    row_tok = jnp.full((P,), N, jnp.int32).at[dest].set(tok[order])
    row_gate = jnp.zeros((P,), xt.dtype).at[dest].set(gate[order].astype(xt.dtype))
    nb = P // bs
    block_expert = jnp.minimum(jnp.searchsorted(pend, jnp.arange(nb) * bs, side='right'), NE - 1)
    x_pad = jnp.concatenate([xt, jnp.zeros((1, D), xt.dtype)], axis=0)
    xin = x_pad[row_tok].reshape(nb, bs, D)

    def expert_block(args):
        xb, e = args
        h = jax.nn.silu(xb @ w1[e]) * (xb @ w3[e])
        return h @ w2[e]

    yb = lax.map(expert_block, (xin, block_expert)).reshape(P, D)
    return jax.ops.segment_sum(yb * row_gate[:, None], row_tok, num_segments=N + 1)[:N]


def hier_moe(x, w_rg, b_rg, w_re, b_re, w1, w3, w2):
    B, S, D = x.shape
    N = B * S
    xt = x.reshape(N, D)
    g_logits = (xt @ w_rg + b_rg).astype(jnp.float32)
    g_prob = jax.nn.softmax(g_logits, axis=-1)
    _, g_sel = lax.top_k(g_logits, 1)
    g_w = jnp.take_along_axis(g_prob, g_sel, axis=-1)
    e_logits = (xt @ w_re + b_re).astype(jnp.float32).reshape(N, N_GROUPS, EXPERTS_PER_GROUP)
    e_logits = jnp.take_along_axis(e_logits, g_sel[:, :, None], axis=1)[:, 0]
    top_l, top_i = lax.top_k(e_logits, TOP_K_INNER)
    top_w = jax.nn.softmax(top_l, axis=-1) * g_w
    expert_id = (g_sel * EXPERTS_PER_GROUP + top_i).astype(jnp.int32)
    y = routed_expert_ffn(xt, expert_id.reshape(-1), top_w.reshape(-1), w1, w3, w2)
    return y.reshape(B, S, D).astype(x.dtype)


def setup_inputs(seed: int = 0) -> dict:
    key = jax.random.key(seed)
    ks = jax.random.split(key, 24)
    f32 = jnp.float32

    def nrm(k, shape, scale):
        return jax.random.normal(k, shape, f32) * scale

    def gain(k, shape):
        return 1.0 + 0.02 * jax.random.normal(k, shape, f32)

    L, D = DEPTH, D_MODEL
    offs = jax.random.randint(ks[1], (BATCH, 1), 0, 4096, dtype=jnp.int32)
    return {
        'x': jax.random.normal(ks[0], (BATCH, SEQ, D), f32),
        'positions': offs + jnp.arange(SEQ, dtype=jnp.int32)[None, :],
        'w_in': nrm(ks[2], (L, D, IN_COLS), D ** -0.5),
        'mla_q_norm': gain(ks[3], (L, MLA_Q_LORA)),
        'mla_w_q_b': nrm(ks[4], (L, MLA_Q_LORA, MLA_HEADS * (MLA_NOPE + MLA_ROPE)), MLA_Q_LORA ** -0.5),
        'mla_kv_norm': gain(ks[5], (L, MLA_KV_LORA)),
        'mla_w_kv_b': nrm(ks[6], (L, MLA_KV_LORA, MLA_HEADS * (MLA_NOPE + MLA_V)), MLA_KV_LORA ** -0.5),
        'w_branch': nrm(ks[7], (L, N_BRANCH, BRANCH_W, D), BRANCH_W ** -0.5 * DEEPNORM_BETA),
        'w_out': nrm(ks[8], (L, D, D), D ** -0.5 * DEEPNORM_BETA),
        'ln1_g': gain(ks[9], (L, D)),
        'ln1_b': nrm(ks[10], (L, D), 0.02),
        'router_group_w': nrm(ks[11], (L, D, N_GROUPS), D ** -0.5),
        'router_group_b': nrm(ks[12], (L, N_GROUPS), 0.01),
        'router_expert_w': nrm(ks[13], (L, D, N_EXPERTS), D ** -0.5),
        'router_expert_b': nrm(ks[14], (L, N_EXPERTS), 0.01),
        'expert_w1': nrm(ks[15], (L, N_EXPERTS, D, D_EXPERT), D ** -0.5),
        'expert_w3': nrm(ks[16], (L, N_EXPERTS, D, D_EXPERT), D ** -0.5),
        'expert_w2': nrm(ks[17], (L, N_EXPERTS, D_EXPERT, D), D_EXPERT ** -0.5 * DEEPNORM_BETA),
        'ln2_g': gain(ks[18], (L, D)),
        'ln2_b': nrm(ks[19], (L, D), 0.02),
    }


def reference(x, positions, w_in, mla_q_norm, mla_w_q_b, mla_kv_norm, mla_w_kv_b, w_branch, w_out,
              ln1_g, ln1_b, router_group_w, router_group_b, router_expert_w, router_expert_b,
              expert_w1, expert_w3, expert_w2, ln2_g, ln2_b):
    cos_r, sin_r = rope_angles(positions, RET_DK)
    cos_m, sin_m = rope_angles(positions, MLA_ROPE)
    for l in range(DEPTH):
        mix = mixer(x, cos_r, sin_r, cos_m, sin_m, w_in[l], mla_q_norm[l], mla_w_q_b[l],
                    mla_kv_norm[l], mla_w_kv_b[l], w_branch[l], w_out[l])
        x = layer_norm(DEEPNORM_ALPHA * x + mix, ln1_g[l], ln1_b[l])
        ffn = hier_moe(x, router_group_w[l], router_group_b[l], router_expert_w[l], router_expert_b[l],
                       expert_w1[l], expert_w3[l], expert_w2[l])
        x = layer_norm(DEEPNORM_ALPHA * x + ffn, ln2_g[l], ln2_b[l])
    return x
```

```python
import functools

import numpy as np
import jax
import jax.numpy as jnp
from jax import lax
from jax.experimental import pallas as pl
from jax.experimental.pallas import tpu as pltpu

F32 = jnp.float32
BF16 = jnp.bfloat16

LANE = 128
SUBLANE = 8
N_HEADS = 8
HEAD_W = 128
BRANCH_W = N_HEADS * HEAD_W
RET_CHUNK = 128
MLA_LORA = 512
MLA_NOPE = 128
MLA_ROPE = 64
ROPE_BASE = 10000.0
N_GROUPS = 4
EXPERTS_PER_GROUP = 8
N_EXPERTS = N_GROUPS * EXPERTS_PER_GROUP
NORM_EPS = 1e-5
MOE_ROWS = 256
VMEM_LIMIT = 56 * 1024 * 1024

CB_RQ, CB_RK, CB_RV, CB_RG = 0, 8, 16, 24
CB_SQ, CB_SK, CB_SV = 32, 40, 48
CB_MQ, CB_MKV = 56, 60
CB_GATE = 64
N_CB = 112


def _cparams(*sem):
    return pltpu.CompilerParams(dimension_semantics=sem, vmem_limit_bytes=VMEM_LIMIT)


def _dot(a, b):
    return jnp.dot(a, b, preferred_element_type=F32)


def _dot_nt(a, b):
    return lax.dot_general(a, b, (((1,), (1,)), ((), ())), preferred_element_type=F32)


def _layer_norm(y, g, b):
    mu = jnp.mean(y, axis=-1, keepdims=True)
    d = y - mu
    var = jnp.mean(d * d, axis=-1, keepdims=True)
    return d * lax.rsqrt(var + NORM_EPS) * g + b


def _proj_kernel(x_ref, w_ref, o_ref):
    acc = _dot(x_ref[...], w_ref[...])
    for h in range(o_ref.shape[1]):
        o_ref[0, h] = acc[:, h * LANE:(h + 1) * LANE].astype(o_ref.dtype)


def _proj(x_bf, w, batch, seq, tm, tn):
    n, k = x_bf.shape
    c = w.shape[1]
    tm = min(tm, seq)
    spb = seq // tm
    return pl.pallas_call(
        _proj_kernel,
        grid=(n // tm, c // tn),
        in_specs=[pl.BlockSpec((tm, k), lambda i, j: (i, 0)),
                  pl.BlockSpec((k, tn), lambda i, j: (0, j))],
        out_specs=pl.BlockSpec((1, tn // LANE, tm, LANE), lambda i, j: (i // spb, j, i % spb, 0)),
        out_shape=jax.ShapeDtypeStruct((batch, c // LANE, seq, LANE), BF16),
        compiler_params=_cparams("parallel", "parallel"),
        name="in_proj",
    )(x_bf, w)


def _retention_tables():
    h = np.arange(N_HEADS, dtype=np.float64)
    log_gamma = np.log1p(-np.exp2(-5.0 - h))
    pos = np.arange(RET_CHUNK, dtype=np.float64)
    rel = pos[:, None] - pos[None, :]
    intra = np.where(rel >= 0, np.exp(log_gamma[:, None, None] * np.maximum(rel, 0.0)), 0.0)
    ones = np.ones((1, 1, RET_CHUNK))
    qdec = np.exp(log_gamma[:, None] * (pos + 1.0))[:, :, None] * ones
    kdec = np.exp(log_gamma[:, None] * (RET_CHUNK - 1 - pos))[:, :, None] * ones
    cdec = [float(np.float32(np.exp(lg * RET_CHUNK))) for lg in log_gamma]
    return (jnp.asarray(intra, F32), jnp.asarray(qdec, F32), jnp.asarray(kdec, F32), cdec)


def _retention_kernel(q_ref, k_ref, v_ref, g_ref, cos_ref, sin_ref, intra_ref, qdec_ref, kdec_ref,
                      o_ref, state_ref, *, n_chunk, cdec):
    @pl.when(pl.program_id(1) == 0)
    def _():
        state_ref[...] = jnp.zeros_like(state_ref)

    k_scale = HEAD_W ** -0.5
    for c in range(n_chunk):
        rows = slice(c * RET_CHUNK, (c + 1) * RET_CHUNK)
        cos = cos_ref[0, rows, :]
        sin = sin_ref[0, rows, :]
        for h in range(N_HEADS):
            q = q_ref[0, h, rows, :].astype(F32)
            k = k_ref[0, h, rows, :].astype(F32)
            q = q * cos + pltpu.roll(q, HEAD_W // 2, 1) * sin
            k = (k * cos + pltpu.roll(k, HEAD_W // 2, 1) * sin) * k_scale
            v = v_ref[0, h, rows, :]
            st = state_ref[h]
            scores = _dot_nt(q.astype(BF16), k.astype(BF16)) * intra_ref[h]
            o = _dot(scores.astype(BF16), v) + _dot((q * qdec_ref[h]).astype(BF16), st.astype(BF16))
            k_dec_t = jnp.transpose(k * kdec_ref[h]).astype(BF16)
            state_ref[h] = st * cdec[h] + _dot(k_dec_t, v)
            o = o * lax.rsqrt(jnp.mean(o * o, axis=-1, keepdims=True) + NORM_EPS)
            g = g_ref[0, h, rows, :].astype(F32)
            o_ref[0, h, rows, :] = (o * (g * jax.nn.sigmoid(g))).astype(o_ref.dtype)


def _retention(p, cos_r, sin_r, rows_per_step):
    batch, _, seq, _ = p.shape
    r = min(rows_per_step, seq)
    intra, qdec, kdec, cdec = _retention_tables()

    def pspec(cb):
        return pl.BlockSpec((1, N_HEADS, r, LANE), lambda b, i: (b, cb // N_HEADS, i, 0))

    tspec = pl.BlockSpec((1, r, LANE), lambda b, i: (b, i, 0))
    cspec = pl.BlockSpec((N_HEADS, RET_CHUNK, RET_CHUNK), lambda b, i: (0, 0, 0))
    return pl.pallas_call(
        functools.partial(_retention_kernel, n_chunk=r // RET_CHUNK, cdec=cdec),
        grid=(batch, seq // r),
        in_specs=[pspec(CB_RQ), pspec(CB_RK), pspec(CB_RV), pspec(CB_RG), tspec, tspec, cspec, cspec, cspec],
        out_specs=pl.BlockSpec((1, N_HEADS, r, LANE), lambda b, i: (b, 0, i, 0)),
        out_shape=jax.ShapeDtypeStruct((batch, N_HEADS, seq, LANE), BF16),
        scratch_shapes=[pltpu.VMEM((N_HEADS, HEAD_W, HEAD_W), F32)],
        compiler_params=_cparams("parallel", "arbitrary"),
        name="retention",
    )(p, p, p, p, cos_r, sin_r, intra, qdec, kdec)


def _sb_kernel(q_ref, k_ref, v_ref, u_ref, o_ref, *, tq):
    i = pl.program_id(2)
    q = q_ref[0, 0]
    u = u_ref[...]
    scale = HEAD_W ** -0.5
    row = lax.broadcasted_iota(jnp.int32, (tq, tq), 0)
    col = lax.broadcasted_iota(jnp.int32, (tq, tq), 1)
    strict = col < row

    def tile(j, c, acc, diag):
        off = pl.multiple_of(j * tq, tq)
        kb = k_ref[0, 0, pl.ds(off, tq), :]
        vb = v_ref[0, 0, pl.ds(off, tq), :]
        z = _dot_nt(q, kb) * scale
        log_beta = jnp.minimum(z, 0.0) - jnp.log1p(jnp.exp(-jnp.abs(z)))
        log_keep = log_beta - z
        if diag:
            log_keep = jnp.where(strict, log_keep, 0.0)
        hi = log_keep.astype(BF16)
        lo = (log_keep - hi.astype(F32)).astype(BF16)
        after = _dot(hi, u) + _dot(lo, u)
        w = jnp.exp(log_beta + after + c)
        if diag:
            w = jnp.where(strict, w, 0.0)
        acc = acc + _dot(w.astype(BF16), vb)
        c = c + jnp.sum(log_keep, axis=-1, keepdims=True)
        return c, acc

    c, acc = tile(i, jnp.zeros((tq, 1), F32), jnp.zeros((tq, HEAD_W), F32), True)
    c, acc = lax.fori_loop(0, i, lambda jj, carry: tile(i - 1 - jj, carry[0], carry[1], False), (c, acc))
    o_ref[0, 0] = acc.astype(o_ref.dtype)


def _stick_breaking(p, tq):
    batch, _, seq, _ = p.shape
    tq = min(tq, seq)
    idx = np.arange(tq)
    u = jnp.asarray(idx[:, None] > idx[None, :], BF16)
    return pl.pallas_call(
        functools.partial(_sb_kernel, tq=tq),
        grid=(batch, N_HEADS, seq // tq),
        in_specs=[pl.BlockSpec((1, 1, tq, LANE), lambda b, h, i: (b, CB_SQ + h, i, 0)),
                  pl.BlockSpec((1, 1, seq, LANE), lambda b, h, i: (b, CB_SK + h, 0, 0)),
                  pl.BlockSpec((1, 1, seq, LANE), lambda b, h, i: (b, CB_SV + h, 0, 0)),
                  pl.BlockSpec((tq, tq), lambda b, h, i: (0, 0))],
        out_specs=pl.BlockSpec((1, 1, tq, LANE), lambda b, h, i: (b, h, i, 0)),
        out_shape=jax.ShapeDtypeStruct((batch, N_HEADS, seq, LANE), BF16),
        compiler_params=_cparams("parallel", "parallel", "arbitrary"),
        name="stick_breaking",
    )(p, p, p, u)


def _mla_prep_kernel(mq_ref, mkv_ref, kr_ref, tq_ref, tk_ref, qn_ref, kvn_ref, wq_ref, wkv_ref,
                     q_out, kn_out, v_out, kpe_out):
    def rms(ref, g_ref):
        x = jnp.concatenate([ref[0, c].astype(F32) for c in range(MLA_LORA // LANE)], axis=-1)
        y = x * lax.rsqrt(jnp.mean(x * x, axis=-1, keepdims=True) + NORM_EPS)
        return (y * g_ref[...]).astype(BF16)

    qf = _dot(rms(mq_ref, qn_ref), wq_ref[...])
    tq = tq_ref[0]
    for h in range(N_HEADS):
        t = qf[:, h * 2 * LANE:(h + 1) * 2 * LANE] * tq
        u = t[:, LANE:]
        pe = u + pltpu.roll(u, LANE // 2, 1)
        q_out[0, h] = jnp.concatenate([t[:, :LANE], pe], axis=-1).astype(q_out.dtype)
    kv = _dot(rms(mkv_ref, kvn_ref), wkv_ref[...])
    for h in range(N_HEADS):
        kn_out[0, h] = kv[:, h * 2 * LANE:h * 2 * LANE + LANE].astype(kn_out.dtype)
        v_out[0, h] = kv[:, h * 2 * LANE + LANE:(h + 1) * 2 * LANE].astype(v_out.dtype)
    t = kr_ref[0, 0].astype(F32) * tk_ref[0]
    kp = t + pltpu.roll(t, LANE // 2, 1)
    lane = lax.broadcasted_iota(jnp.int32, kp.shape, 1)
    kpe_out[0] = jnp.where(lane < MLA_ROPE, kp, 0.0).astype(kpe_out.dtype)


def _mla_prep(p, kr, tq_tab, tk_tab, q_norm, kv_norm, wq, wkv, tm):
    batch, _, seq, _ = p.shape
    tm = min(tm, seq)
    nl = MLA_LORA // LANE
    head_out = lambda w: pl.BlockSpec((1, N_HEADS, tm, w), lambda b, i: (b, 0, i, 0))
    return pl.pallas_call(
        _mla_prep_kernel,
        grid=(batch, seq // tm),
        in_specs=[pl.BlockSpec((1, nl, tm, LANE), lambda b, i: (b, CB_MQ // nl, i, 0)),
                  pl.BlockSpec((1, nl, tm, LANE), lambda b, i: (b, CB_MKV // nl, i, 0)),
                  pl.BlockSpec((1, 1, tm, LANE), lambda b, i: (b, 0, i, 0)),
                  pl.BlockSpec((1, tm, 2 * LANE), lambda b, i: (b, i, 0)),
                  pl.BlockSpec((1, tm, LANE), lambda b, i: (b, i, 0)),
                  pl.BlockSpec((1, MLA_LORA), lambda b, i: (0, 0)),
                  pl.BlockSpec((1, MLA_LORA), lambda b, i: (0, 0)),
                  pl.BlockSpec(wq.shape, lambda b, i: (0, 0)),
                  pl.BlockSpec(wkv.shape, lambda b, i: (0, 0))],
        out_specs=[head_out(2 * LANE), head_out(LANE), head_out(LANE),
                   pl.BlockSpec((1, tm, LANE), lambda b, i: (b, i, 0))],
        out_shape=[jax.ShapeDtypeStruct((batch, N_HEADS, seq, 2 * LANE), BF16),
                   jax.ShapeDtypeStruct((batch, N_HEADS, seq, LANE), BF16),
                   jax.ShapeDtypeStruct((batch, N_HEADS, seq, LANE), BF16),
                   jax.ShapeDtypeStruct((batch, seq, LANE), BF16)],
        compiler_params=_cparams("parallel", "parallel"),
        name="mla_prep",
    )(p, p, kr, tq_tab, tk_tab, q_norm, kv_norm, wq, wkv)


def _mla_attn_kernel(q_ref, kn_ref, kpe_ref, v_ref, o_ref, *, tq):
    i = pl.program_id(2)
    q = q_ref[0, 0]
    row = lax.broadcasted_iota(jnp.int32, (tq, tq), 0)
    col = lax.broadcasted_iota(jnp.int32, (tq, tq), 1)
    causal = col <= row

    def tile(j, m, l, acc, diag):
        off = pl.multiple_of(j * tq, tq)
        k = jnp.concatenate([kn_ref[0, 0, pl.ds(off, tq), :], kpe_ref[0, pl.ds(off, tq), :]], axis=-1)
        s = _dot_nt(q, k)
        if diag:
            s = jnp.where(causal, s, -1e30)
        m_new = jnp.maximum(m, jnp.max(s, axis=-1, keepdims=True))
        a = jnp.exp(m - m_new)
        pr = jnp.exp(s - m_new)
        l = a * l + jnp.sum(pr, axis=-1, keepdims=True)
        acc = a * acc + _dot(pr.astype(BF16), v_ref[0, 0, pl.ds(off, tq), :])
        return m_new, l, acc

    init = (jnp.full((tq, 1), -1e30, F32), jnp.zeros((tq, 1), F32), jnp.zeros((tq, HEAD_W), F32))
    m, l, acc = lax.fori_loop(0, i, lambda j, c: tile(j, c[0], c[1], c[2], False), init)
    m, l, acc = tile(i, m, l, acc, True)
    o_ref[0, 0] = (acc / l).astype(o_ref.dtype)


def _mla_attn(q, kn, kpe, v, tq):
    batch, _, seq, _ = q.shape
    tq = min(tq, seq)
    return pl.pallas_call(
        functools.partial(_mla_attn_kernel, tq=tq),
        grid=(batch, N_HEADS, seq // tq),
        in_specs=[pl.BlockSpec((1, 1, tq, 2 * LANE), lambda b, h, i: (b, h, i, 0)),
                  pl.BlockSpec((1, 1, seq, LANE), lambda b, h, i: (b, h, 0, 0)),
                  pl.BlockSpec((1, seq, LANE), lambda b, h, i: (b, 0, 0)),
                  pl.BlockSpec((1, 1, seq, LANE), lambda b, h, i: (b, h, 0, 0))],
        out_specs=pl.BlockSpec((1, 1, tq, LANE), lambda b, h, i: (b, h, i, 0)),
        out_shape=jax.ShapeDtypeStruct((batch, N_HEADS, seq, LANE), BF16),
        compiler_params=_cparams("parallel", "parallel", "arbitrary"),
        name="mla_attn",
    )(q, kn, kpe, v)


def _merge_kernel(r_ref, s_ref, m_ref, g0_ref, g1_ref, g2_ref, w_ref, o_ref):
    acc = None
    for n, (b_ref, g_ref) in enumerate(((r_ref, g0_ref), (s_ref, g1_ref), (m_ref, g2_ref))):
        a = jnp.concatenate([b_ref[0, h] for h in range(N_HEADS)], axis=-1)
        y = _dot(a, w_ref[n])
        g = jax.nn.sigmoid(jnp.concatenate([g_ref[0, c].astype(F32) for c in range(g_ref.shape[1])], axis=-1))
        acc = g * y if acc is None else acc + g * y
    o_ref[...] = acc.astype(o_ref.dtype)


def _merge(ret, sb, ml, p, w_branch, tm, tn):
    batch, _, seq, _ = p.shape
    d = w_branch.shape[-1]
    tm = min(tm, seq)
    spb = seq // tm
    gcb = tn // LANE
    bspec = pl.BlockSpec((1, N_HEADS, tm, LANE), lambda j, i: (i // spb, 0, i % spb, 0))

    def gspec(n):
        first = (CB_GATE + n * (d // LANE)) // gcb
        return pl.BlockSpec((1, gcb, tm, LANE), lambda j, i: (i // spb, first + j, i % spb, 0))

    return pl.pallas_call(
        _merge_kernel,
        grid=(d // tn, batch * spb),
        in_specs=[bspec, bspec, bspec, gspec(0), gspec(1), gspec(2),
                  pl.BlockSpec((3, BRANCH_W, tn), lambda j, i: (0, 0, j))],
        out_specs=pl.BlockSpec((tm, tn), lambda j, i: (i, j)),
        out_shape=jax.ShapeDtypeStruct((batch * seq, d), BF16),
        compiler_params=_cparams("parallel", "parallel"),
        name="branch_merge",
    )(ret, sb, ml, p, p, p, w_branch)


def _route(logits):
    lane = lax.broadcasted_iota(jnp.int32, logits.shape, 1)
    neg = -jnp.inf
    big = jnp.int32(1 << 20)
    is_g = lane < N_GROUPS
    gl = jnp.where(is_g, logits, neg)
    gm = jnp.max(gl, axis=-1, keepdims=True)
    g_sel = jnp.min(jnp.where(gl == gm, lane, big), axis=-1, keepdims=True)
    g_w = 1.0 / jnp.sum(jnp.where(is_g, jnp.exp(gl - gm), 0.0), axis=-1, keepdims=True)
    lo = N_GROUPS + EXPERTS_PER_GROUP * g_sel
    el = jnp.where((lane >= lo) & (lane < lo + EXPERTS_PER_GROUP), logits, neg)
    t1 = jnp.max(el, axis=-1, keepdims=True)
    i1 = jnp.min(jnp.where(el == t1, lane, big), axis=-1, keepdims=True)
    el2 = jnp.where(lane == i1, neg, el)
    t2 = jnp.max(el2, axis=-1, keepdims=True)
    i2 = jnp.min(jnp.where(el2 == t2, lane, big), axis=-1, keepdims=True)
    dd = jnp.exp(t2 - t1)
    w1 = g_w / (1.0 + dd)
    w2 = g_w * dd / (1.0 + dd)
    e1 = (i1 - N_GROUPS).astype(F32)
    e2 = (i2 - N_GROUPS).astype(F32)
    return jnp.where(lane == 0, e1, jnp.where(lane == 1, e2, jnp.where(lane == 2, w1, jnp.where(lane == 3, w2, 0.0))))


def _out_ln_router_kernel(m_ref, w_ref, x_ref, g_ref, b_ref, rwh_ref, rwl_ref, rb_ref, x1_ref, r_ref, *, alpha):
    mix = _dot(m_ref[...], w_ref[...])
    x1 = _layer_norm(alpha * x_ref[...] + mix, g_ref[...], b_ref[...])
    x1_ref[...] = x1
    xh = x1.astype(BF16)
    xl = (x1 - xh.astype(F32)).astype(BF16)
    logits = _dot(xh, rwh_ref[...]) + _dot(xl, rwh_ref[...]) + _dot(xh, rwl_ref[...]) + rb_ref[...]
    r_ref[...] = _route(logits)


def _out_ln_router(merged, w_out, x, ln_g, ln_b, rw_hi, rw_lo, rb, alpha, tm):
    n, d = x.shape
    tm = min(tm, n)
    row = lambda w: pl.BlockSpec((tm, w), lambda i: (i, 0))
    full = lambda a: pl.BlockSpec(a.shape, lambda i: (0, 0))
    return pl.pallas_call(
        functools.partial(_out_ln_router_kernel, alpha=alpha),
        grid=(n // tm,),
        in_specs=[row(d), full(w_out), row(d), full(ln_g), full(ln_b), full(rw_hi), full(rw_lo), full(rb)],
        out_specs=[row(d), row(LANE)],
        out_shape=[jax.ShapeDtypeStruct((n, d), F32), jax.ShapeDtypeStruct((n, LANE), F32)],
        compiler_params=_cparams("parallel"),
        name="out_proj_ln_router",
    )(merged, w_out, x, ln_g, ln_b, rw_hi, rw_lo, rb)


def _moe_kernel(bexp_ref, nact_ref, nval_ref, tok_ref, dst_ref,
                x_hbm, gate_ref, w1_ref, w3_ref, w2_ref, out_hbm, xbuf, ybuf, sem_in, sem_out, *, bm):
    del bexp_ref
    i = pl.program_id(0)

    @pl.when(i < nact_ref[0])
    def _():
        base = i * bm

        def gather_row(r, carry):
            pltpu.make_async_copy(x_hbm.at[pl.ds(tok_ref[base + r], 1), :], xbuf.at[pl.ds(r, 1), :], sem_in).start()
            return carry

        lax.fori_loop(0, bm, gather_row, 0)
        pltpu.make_async_copy(x_hbm.at[pl.ds(0, bm), :], xbuf, sem_in).wait()

        xb = xbuf[...].astype(BF16)
        h1 = _dot(xb, w1_ref[0])
        h3 = _dot(xb, w3_ref[0])
        hh = (h1 * jax.nn.sigmoid(h1) * h3).astype(BF16)
        ybuf[...] = _dot(hh, w2_ref[0]) * gate_ref[...]

        def scatter_row(r, carry):
            pltpu.make_async_copy(ybuf.at[pl.ds(r, 1), :], out_hbm.at[pl.ds(dst_ref[base + r], 1), :], sem_out).start()
            return carry

        nv = nval_ref[i]
        lax.fori_loop(0, nv, scatter_row, 0)
        nv8 = pl.multiple_of((nv // SUBLANE) * SUBLANE, SUBLANE)

        @pl.when(nv8 > 0)
        def _():
            pltpu.make_async_copy(ybuf.at[pl.ds(0, nv8), :], out_hbm.at[pl.ds(0, nv8), :], sem_out).wait()

        def wait_row(r, carry):
            pltpu.make_async_copy(ybuf.at[pl.ds(0, 1), :], out_hbm.at[pl.ds(0, 1), :], sem_out).wait()
            return carry

        lax.fori_loop(nv8, nv, wait_row, 0)


def _moe_experts(x1, route, w1, w3, w2, bm):
    n, d = x1.shape
    ne = w1.shape[0]
    a = 2 * n
    p_rows = a + ne * bm
    nb = p_rows // bm
    eid = route[:, :2].astype(jnp.int32).reshape(a)
    gate = route[:, 2:4].reshape(a)
    order = jnp.argsort(eid, stable=True).astype(jnp.int32)
    e_sorted = eid[order]
    counts = jnp.bincount(eid, length=ne).astype(jnp.int32)
    padded = (counts + bm - 1) // bm * bm
    start = jnp.cumsum(counts) - counts
    pend = jnp.cumsum(padded)
    pstart = pend - padded
    dest = pstart[e_sorted] + jnp.arange(a, dtype=jnp.int32) - start[e_sorted]
    row_dst = jnp.zeros((p_rows,), jnp.int32).at[dest].set(order)
    row_tok = row_dst // 2
    row_gate = jnp.zeros((p_rows,), F32).at[dest].set(gate[order]).reshape(p_rows, 1)
    blk_start = jnp.arange(nb, dtype=jnp.int32) * bm
    blk_exp = jnp.minimum(jnp.searchsorted(pend, blk_start, side="right"), ne - 1).astype(jnp.int32)
    n_active = (pend[-1] // bm).astype(jnp.int32).reshape(1)
    blk_valid = jnp.clip(pstart[blk_exp] + counts[blk_exp] - blk_start, 0, bm).astype(jnp.int32)
    blk_exp = jnp.where(jnp.arange(nb) < n_active[0], blk_exp, blk_exp[jnp.maximum(n_active[0] - 1, 0)])

    de = w1.shape[-1]
    grid_spec = pltpu.PrefetchScalarGridSpec(
        num_scalar_prefetch=5,
        grid=(nb,),
        in_specs=[pl.BlockSpec(memory_space=pl.ANY),
                  pl.BlockSpec((bm, 1), lambda i, be, *_: (i, 0)),
                  pl.BlockSpec((1, d, de), lambda i, be, *_: (be[i], 0, 0)),
                  pl.BlockSpec((1, d, de), lambda i, be, *_: (be[i], 0, 0)),
                  pl.BlockSpec((1, de, d), lambda i, be, *_: (be[i], 0, 0))],
        out_specs=pl.BlockSpec(memory_space=pl.ANY),
        scratch_shapes=[pltpu.VMEM((bm, d), F32), pltpu.VMEM((bm, d), F32),
                        pltpu.SemaphoreType.DMA(()), pltpu.SemaphoreType.DMA(())],
    )
    out = pl.pallas_call(
        functools.partial(_moe_kernel, bm=bm),
        grid_spec=grid_spec,
        out_shape=jax.ShapeDtypeStruct((a, d), F32),
        compiler_params=_cparams("arbitrary"),
        name="moe_experts",
    )(blk_exp, n_active, blk_valid, row_tok, row_dst, x1, row_gate, w1, w3, w2)
    return out.reshape(n, 2 * d)


def _combine_ln_kernel(x_ref, y_ref, g_ref, b_ref, o_ref, obf_ref, *, alpha):
    d = x_ref.shape[-1]
    y = y_ref[:, :d] + y_ref[:, d:]
    x2 = _layer_norm(alpha * x_ref[...] + y, g_ref[...], b_ref[...])
    o_ref[...] = x2
    obf_ref[...] = x2.astype(BF16)


def _combine_ln(x1, y2, ln_g, ln_b, alpha, tm):
    n, d = x1.shape
    tm = min(tm, n)
    return pl.pallas_call(
        functools.partial(_combine_ln_kernel, alpha=alpha),
        grid=(n // tm,),
        in_specs=[pl.BlockSpec((tm, d), lambda i: (i, 0)), pl.BlockSpec((tm, 2 * d), lambda i: (i, 0)),
                  pl.BlockSpec((1, d), lambda i: (0, 0)), pl.BlockSpec((1, d), lambda i: (0, 0))],
        out_specs=[pl.BlockSpec((tm, d), lambda i: (i, 0)), pl.BlockSpec((tm, d), lambda i: (i, 0))],
        out_shape=[jax.ShapeDtypeStruct((n, d), F32), jax.ShapeDtypeStruct((n, d), BF16)],
        compiler_params=_cparams("parallel"),
        name="combine_ln",
    )(x1, y2, ln_g, ln_b)


def _rot_half_cols(w):
    half = w.shape[-1] // 2
    return jnp.concatenate([-w[..., half:], w[..., :half]], axis=-1)


def _rope_cos_sin(positions, dim):
    inv = 1.0 / (ROPE_BASE ** (jnp.arange(0, dim, 2, dtype=F32) / dim))
    ang = positions.astype(F32)[..., None] * inv
    return jnp.cos(ang), jnp.sin(ang)


def kernel(x, positions, w_in, mla_q_norm, mla_w_q_b, mla_kv_norm, mla_w_kv_b, w_branch, w_out, ln1_g, ln1_b, router_group_w, router_group_b, router_expert_w, router_expert_b, expert_w1, expert_w3, expert_w2, ln2_g, ln2_b):
    batch, seq, d = x.shape
    depth = w_in.shape[0]
    n = batch * seq
    alpha = (2 * depth) ** 0.25
    n_main = (CB_MQ + 2 * MLA_LORA // LANE) * LANE

    cr, sr = _rope_cos_sin(positions, HEAD_W)
    cos_r = jnp.concatenate([cr, cr], axis=-1)
    sin_r = jnp.concatenate([-sr, sr], axis=-1)
    cm, sm = _rope_cos_sin(positions, MLA_ROPE)
    q_scale = (MLA_NOPE + MLA_ROPE) ** -0.5
    tq_tab = q_scale * jnp.concatenate([jnp.ones((batch, seq, MLA_NOPE), F32), cm, cm, sm, sm], axis=-1)
    tk_tab = jnp.concatenate([cm, cm, sm, sm], axis=-1)

    xf = x.reshape(n, d)
    xb = xf.astype(BF16)
    for l in range(depth):
        w_kr = w_in[l][:, n_main:n_main + MLA_ROPE]
        w_main = jnp.concatenate([w_in[l][:, :n_main], w_in[l][:, n_main + MLA_ROPE:]], axis=-1).astype(BF16)
        w_kr2 = jnp.concatenate([w_kr, _rot_half_cols(w_kr)], axis=-1).astype(BF16)
        wq = mla_w_q_b[l].reshape(MLA_LORA, N_HEADS, MLA_NOPE + MLA_ROPE)
        wq_pe = wq[..., MLA_NOPE:]
        wq = jnp.concatenate([wq[..., :MLA_NOPE], wq_pe, _rot_half_cols(wq_pe)], axis=-1)
        wq = wq.reshape(MLA_LORA, N_HEADS * 2 * LANE).astype(BF16)
        wkv = mla_w_kv_b[l].astype(BF16)
        rw = jnp.concatenate([router_group_w[l], router_expert_w[l],
                              jnp.zeros((d, LANE - N_GROUPS - N_EXPERTS), F32)], axis=-1)
        rw_hi = rw.astype(BF16)
        rw_lo = (rw - rw_hi.astype(F32)).astype(BF16)
        rb = jnp.concatenate([router_group_b[l], router_expert_b[l],
                              jnp.zeros((LANE - N_GROUPS - N_EXPERTS,), F32)]).reshape(1, LANE)

        p = _proj(xb, w_main, batch, seq, tm=1024, tn=1024)
        kr = _proj(xb, w_kr2, batch, seq, tm=1024, tn=LANE)
        ret = _retention(p, cos_r, sin_r, rows_per_step=512)
        sb = _stick_breaking(p, tq=256)
        q, kn, v, kpe = _mla_prep(p, kr, tq_tab, tk_tab, mla_q_norm[l].reshape(1, -1), mla_kv_norm[l].reshape(1, -1),
                                  wq, wkv, tm=512)
        ml = _mla_attn(q, kn, kpe, v, tq=256)
        merged = _merge(ret, sb, ml, p, w_branch[l].astype(BF16), tm=512, tn=1024)
        x1, route = _out_ln_router(merged, w_out[l].astype(BF16), xf, ln1_g[l].reshape(1, d), ln1_b[l].reshape(1, d),
                                   rw_hi, rw_lo, rb, alpha, tm=256)
        y2 = _moe_experts(x1, route, expert_w1[l].astype(BF16), expert_w3[l].astype(BF16),
                          expert_w2[l].astype(BF16), MOE_ROWS)
        xf, xb = _combine_ln(x1, y2, ln2_g[l].reshape(1, d), ln2_b[l].reshape(1, d), alpha, tm=512)
    return xf.reshape(batch, seq, d)
```

```python
import functools

import numpy as np
import jax
import jax.numpy as jnp
from jax import lax
from jax.experimental import pallas as pl
from jax.experimental.pallas import tpu as pltpu

F32 = jnp.float32
BF16 = jnp.bfloat16

LANE = 128
SUBLANE = 8
N_HEADS = 8
HEAD_W = 128
BRANCH_W = N_HEADS * HEAD_W
RET_CHUNK = 128
MLA_LORA = 512
MLA_NOPE = 128
MLA_ROPE = 64
ROPE_BASE = 10000.0
N_GROUPS = 4
EXPERTS_PER_GROUP = 8
N_EXPERTS = N_GROUPS * EXPERTS_PER_GROUP
NORM_EPS = 1e-5
MOE_ROWS = 256
VMEM_LIMIT = 56 * 1024 * 1024

CB_RQ, CB_RK, CB_RV, CB_RG = 0, 8, 16, 24
CB_SQ, CB_SK, CB_SV = 32, 40, 48
CB_MQ, CB_MKV = 56, 60
CB_GATE = 64
N_CB = 112


def _cparams(*sem):
    return pltpu.CompilerParams(dimension_semantics=sem, vmem_limit_bytes=VMEM_LIMIT)


def _dot(a, b):
    return jnp.dot(a, b, preferred_element_type=F32)


def _dot_nt(a, b):
    return lax.dot_general(a, b, (((1,), (1,)), ((), ())), preferred_element_type=F32)


def _layer_norm(y, g, b):
    mu = jnp.mean(y, axis=-1, keepdims=True)
    d = y - mu
    var = jnp.mean(d * d, axis=-1, keepdims=True)
    return d * lax.rsqrt(var + NORM_EPS) * g + b


def _proj_kernel(x_ref, w_ref, o_ref):
    acc = _dot(x_ref[...], w_ref[...])
    for h in range(o_ref.shape[1]):
        o_ref[0, h] = acc[:, h * LANE:(h + 1) * LANE].astype(o_ref.dtype)


def _proj(x_bf, w, batch, seq, tm, tn):
    n, k = x_bf.shape
    c = w.shape[1]
    tm = min(tm, seq)
    spb = seq // tm
    return pl.pallas_call(
        _proj_kernel,
        grid=(n // tm, c // tn),
        in_specs=[pl.BlockSpec((tm, k), lambda i, j: (i, 0)),
                  pl.BlockSpec((k, tn), lambda i, j: (0, j))],
        out_specs=pl.BlockSpec((1, tn // LANE, tm, LANE), lambda i, j: (i // spb, j, i % spb, 0)),
        out_shape=jax.ShapeDtypeStruct((batch, c // LANE, seq, LANE), BF16),
        compiler_params=_cparams("parallel", "parallel"),
        name="in_proj",
    )(x_bf, w)


def _retention_tables():
    h = np.arange(N_HEADS, dtype=np.float64)
    log_gamma = np.log1p(-np.exp2(-5.0 - h))
    pos = np.arange(RET_CHUNK, dtype=np.float64)
    rel = pos[:, None] - pos[None, :]
    intra = np.where(rel >= 0, np.exp(log_gamma[:, None, None] * np.maximum(rel, 0.0)), 0.0)
    ones = np.ones((1, 1, RET_CHUNK))
    qdec = np.exp(log_gamma[:, None] * (pos + 1.0))[:, :, None] * ones
    kdec = np.exp(log_gamma[:, None] * (RET_CHUNK - 1 - pos))[:, :, None] * ones
    cdec = [float(np.float32(np.exp(lg * RET_CHUNK))) for lg in log_gamma]
    return (jnp.asarray(intra, F32), jnp.asarray(qdec, F32), jnp.asarray(kdec, F32), cdec)


def _retention_kernel(q_ref, k_ref, v_ref, g_ref, cos_ref, sin_ref, intra_ref, qdec_ref, kdec_ref,
                      o_ref, state_ref, *, n_chunk, cdec):
    @pl.when(pl.program_id(1) == 0)
    def _():
        state_ref[...] = jnp.zeros_like(state_ref)

    k_scale = HEAD_W ** -0.5
    for c in range(n_chunk):
        rows = slice(c * RET_CHUNK, (c + 1) * RET_CHUNK)
        cos = cos_ref[0, rows, :]
        sin = sin_ref[0, rows, :]
        for h in range(N_HEADS):
            q = q_ref[0, h, rows, :].astype(F32)
            k = k_ref[0, h, rows, :].astype(F32)
            q = q * cos + pltpu.roll(q, HEAD_W // 2, 1) * sin
            k = (k * cos + pltpu.roll(k, HEAD_W // 2, 1) * sin) * k_scale
            v = v_ref[0, h, rows, :]
            st = state_ref[h]
            scores = _dot_nt(q.astype(BF16), k.astype(BF16)) * intra_ref[h]
            o = _dot(scores.astype(BF16), v) + _dot((q * qdec_ref[h]).astype(BF16), st.astype(BF16))
            k_dec_t = jnp.transpose(k * kdec_ref[h]).astype(BF16)
            state_ref[h] = st * cdec[h] + _dot(k_dec_t, v)
            o = o * lax.rsqrt(jnp.mean(o * o, axis=-1, keepdims=True) + NORM_EPS)
            g = g_ref[0, h, rows, :].astype(F32)
            o_ref[0, h, rows, :] = (o * (g * jax.nn.sigmoid(g))).astype(o_ref.dtype)


def _retention(p, cos_r, sin_r, rows_per_step):
    batch, _, seq, _ = p.shape
    r = min(rows_per_step, seq)
    intra, qdec, kdec, cdec = _retention_tables()

    def pspec(cb):
        return pl.BlockSpec((1, N_HEADS, r, LANE), lambda b, i: (b, cb // N_HEADS, i, 0))

    tspec = pl.BlockSpec((1, r, LANE), lambda b, i: (b, i, 0))
    cspec = pl.BlockSpec((N_HEADS, RET_CHUNK, RET_CHUNK), lambda b, i: (0, 0, 0))
    return pl.pallas_call(
        functools.partial(_retention_kernel, n_chunk=r // RET_CHUNK, cdec=cdec),
        grid=(batch, seq // r),
        in_specs=[pspec(CB_RQ), pspec(CB_RK), pspec(CB_RV), pspec(CB_RG), tspec, tspec, cspec, cspec, cspec],
        out_specs=pl.BlockSpec((1, N_HEADS, r, LANE), lambda b, i: (b, 0, i, 0)),
        out_shape=jax.ShapeDtypeStruct((batch, N_HEADS, seq, LANE), BF16),
        scratch_shapes=[pltpu.VMEM((N_HEADS, HEAD_W, HEAD_W), F32)],
        compiler_params=_cparams("parallel", "arbitrary"),
        name="retention",
    )(p, p, p, p, cos_r, sin_r, intra, qdec, kdec)


def _sb_kernel(q_ref, k_ref, v_ref, u2_ref, o_ref, *, tq, nq):
    i = pl.program_id(2)
    q = q_ref[0, 0]
    u2 = u2_ref[...]
    scale = HEAD_W ** -0.5
    row = lax.broadcasted_iota(jnp.int32, (tq, tq), 0)
    col = lax.broadcasted_iota(jnp.int32, (tq, tq), 1)
    strict = col < row

    def attend(nblk):
        z = _dot_nt(q, k_ref[0, 0, :nblk * tq, :]) * scale
        log_beta = jnp.minimum(z, 0.0) - jnp.log(1.0 + jnp.exp(-jnp.abs(z)))
        log_keep = log_beta - z
        c = jnp.zeros((tq, 1), F32)
        ws = [None] * nblk
        for b in reversed(range(nblk)):
            diag = b == nblk - 1
            lk = log_keep[:, b * tq:(b + 1) * tq]
            if diag:
                lk = jnp.where(strict, lk, 0.0)
            hi = lk.astype(BF16)
            lo = (lk - hi.astype(F32)).astype(BF16)
            after = _dot(jnp.concatenate([hi, lo], axis=-1), u2)
            w = jnp.exp(log_beta[:, b * tq:(b + 1) * tq] + after + c)
            if diag:
                w = jnp.where(strict, w, 0.0)
            ws[b] = w.astype(BF16)
            c = c + jnp.sum(lk, axis=-1, keepdims=True)
        w_all = ws[0] if nblk == 1 else jnp.concatenate(ws, axis=-1)
        o_ref[0, 0] = _dot(w_all, v_ref[0, 0, :nblk * tq, :]).astype(o_ref.dtype)

    for nblk in range(1, nq + 1):
        pl.when(i == nblk - 1)(functools.partial(attend, nblk))


def _stick_breaking(p, tq):
    batch, _, seq, _ = p.shape
    tq = min(tq, seq)
    idx = np.arange(tq)
    u = idx[:, None] > idx[None, :]
    u2 = jnp.asarray(np.concatenate([u, u], axis=0), BF16)
    return pl.pallas_call(
        functools.partial(_sb_kernel, tq=tq, nq=seq // tq),
        grid=(batch, N_HEADS, seq // tq),
        in_specs=[pl.BlockSpec((1, 1, tq, LANE), lambda b, h, i: (b, CB_SQ + h, i, 0)),
                  pl.BlockSpec((1, 1, seq, LANE), lambda b, h, i: (b, CB_SK + h, 0, 0)),
                  pl.BlockSpec((1, 1, seq, LANE), lambda b, h, i: (b, CB_SV + h, 0, 0)),
                  pl.BlockSpec((2 * tq, tq), lambda b, h, i: (0, 0))],
        out_specs=pl.BlockSpec((1, 1, tq, LANE), lambda b, h, i: (b, h, i, 0)),
        out_shape=jax.ShapeDtypeStruct((batch, N_HEADS, seq, LANE), BF16),
        compiler_params=_cparams("parallel", "parallel", "arbitrary"),
        name="stick_breaking",
    )(p, p, p, u2)


def _mla_prep_kernel(mq_ref, mkv_ref, kr_ref, tq_ref, tk_ref, qn_ref, kvn_ref, wq_ref, wkv_ref,
                     q_out, kn_out, v_out, kpe_out):
    def rms(ref, g_ref):
        x = jnp.concatenate([ref[0, c].astype(F32) for c in range(MLA_LORA // LANE)], axis=-1)
        y = x * lax.rsqrt(jnp.mean(x * x, axis=-1, keepdims=True) + NORM_EPS)
        return (y * g_ref[...]).astype(BF16)

    qf = _dot(rms(mq_ref, qn_ref), wq_ref[...])
    tq = tq_ref[0]
    for h in range(N_HEADS):
        t = qf[:, h * 2 * LANE:(h + 1) * 2 * LANE] * tq
        u = t[:, LANE:]
        pe = u + pltpu.roll(u, LANE // 2, 1)
        q_out[0, h] = jnp.concatenate([t[:, :LANE], pe], axis=-1).astype(q_out.dtype)
    kv = _dot(rms(mkv_ref, kvn_ref), wkv_ref[...])
    for h in range(N_HEADS):
        kn_out[0, h] = kv[:, h * 2 * LANE:h * 2 * LANE + LANE].astype(kn_out.dtype)
        v_out[0, h] = kv[:, h * 2 * LANE + LANE:(h + 1) * 2 * LANE].astype(v_out.dtype)
    t = kr_ref[0, 0].astype(F32) * tk_ref[0]
    kp = t + pltpu.roll(t, LANE // 2, 1)
    lane = lax.broadcasted_iota(jnp.int32, kp.shape, 1)
    kpe_out[0] = jnp.where(lane < MLA_ROPE, kp, 0.0).astype(kpe_out.dtype)


def _mla_prep(p, kr, tq_tab, tk_tab, q_norm, kv_norm, wq, wkv, tm):
    batch, _, seq, _ = p.shape
    tm = min(tm, seq)
    nl = MLA_LORA // LANE
    head_out = lambda w: pl.BlockSpec((1, N_HEADS, tm, w), lambda b, i: (b, 0, i, 0))
    return pl.pallas_call(
        _mla_prep_kernel,
        grid=(batch, seq // tm),
        in_specs=[pl.BlockSpec((1, nl, tm, LANE), lambda b, i: (b, CB_MQ // nl, i, 0)),
                  pl.BlockSpec((1, nl, tm, LANE), lambda b, i: (b, CB_MKV // nl, i, 0)),
                  pl.BlockSpec((1, 1, tm, LANE), lambda b, i: (b, 0, i, 0)),
                  pl.BlockSpec((1, tm, 2 * LANE), lambda b, i: (b, i, 0)),
                  pl.BlockSpec((1, tm, LANE), lambda b, i: (b, i, 0)),
                  pl.BlockSpec((1, MLA_LORA), lambda b, i: (0, 0)),
                  pl.BlockSpec((1, MLA_LORA), lambda b, i: (0, 0)),
                  pl.BlockSpec(wq.shape, lambda b, i: (0, 0)),
                  pl.BlockSpec(wkv.shape, lambda b, i: (0, 0))],
        out_specs=[head_out(2 * LANE), head_out(LANE), head_out(LANE),
                   pl.BlockSpec((1, tm, LANE), lambda b, i: (b, i, 0))],
        out_shape=[jax.ShapeDtypeStruct((batch, N_HEADS, seq, 2 * LANE), BF16),
                   jax.ShapeDtypeStruct((batch, N_HEADS, seq, LANE), BF16),
                   jax.ShapeDtypeStruct((batch, N_HEADS, seq, LANE), BF16),
                   jax.ShapeDtypeStruct((batch, seq, LANE), BF16)],
        compiler_params=_cparams("parallel", "parallel"),
        name="mla_prep",
    )(p, p, kr, tq_tab, tk_tab, q_norm, kv_norm, wq, wkv)


def _mla_attn_kernel(q_ref, kn_ref, kpe_ref, v_ref, o_ref, *, tq, nq):
    i = pl.program_id(2)
    q = q_ref[0, 0]
    row = lax.broadcasted_iota(jnp.int32, (tq, tq), 0)
    col = lax.broadcasted_iota(jnp.int32, (tq, tq), 1)
    causal = col <= row

    def attend(nblk):
        n_keys = nblk * tq
        k = jnp.concatenate([kn_ref[0, 0, :n_keys, :], kpe_ref[0, :n_keys, :]], axis=-1)
        s = _dot_nt(q, k)
        last = jnp.where(causal, s[:, n_keys - tq:], -1e30)
        s = last if nblk == 1 else jnp.concatenate([s[:, :n_keys - tq], last], axis=-1)
        pr = jnp.exp(s - jnp.max(s, axis=-1, keepdims=True))
        l = jnp.sum(pr, axis=-1, keepdims=True)
        o_ref[0, 0] = (_dot(pr.astype(BF16), v_ref[0, 0, :n_keys, :]) / l).astype(o_ref.dtype)

    for nblk in range(1, nq + 1):
        pl.when(i == nblk - 1)(functools.partial(attend, nblk))


def _mla_attn(q, kn, kpe, v, tq):
    batch, _, seq, _ = q.shape
    tq = min(tq, seq)
    return pl.pallas_call(
        functools.partial(_mla_attn_kernel, tq=tq, nq=seq // tq),
        grid=(batch, N_HEADS, seq // tq),
        in_specs=[pl.BlockSpec((1, 1, tq, 2 * LANE), lambda b, h, i: (b, h, i, 0)),
                  pl.BlockSpec((1, 1, seq, LANE), lambda b, h, i: (b, h, 0, 0)),
                  pl.BlockSpec((1, seq, LANE), lambda b, h, i: (b, 0, 0)),
                  pl.BlockSpec((1, 1, seq, LANE), lambda b, h, i: (b, h, 0, 0))],
        out_specs=pl.BlockSpec((1, 1, tq, LANE), lambda b, h, i: (b, h, i, 0)),
        out_shape=jax.ShapeDtypeStruct((batch, N_HEADS, seq, LANE), BF16),
        compiler_params=_cparams("parallel", "parallel", "arbitrary"),
        name="mla_attn",
    )(q, kn, kpe, v)


def _merge_kernel(r_ref, s_ref, m_ref, g0_ref, g1_ref, g2_ref, w_ref, o_ref):
    acc = None
    for n, (b_ref, g_ref) in enumerate(((r_ref, g0_ref), (s_ref, g1_ref), (m_ref, g2_ref))):
        a = jnp.concatenate([b_ref[0, h] for h in range(N_HEADS)], axis=-1)
        y = _dot(a, w_ref[n])
        g = jax.nn.sigmoid(jnp.concatenate([g_ref[0, c].astype(F32) for c in range(g_ref.shape[1])], axis=-1))
        acc = g * y if acc is None else acc + g * y
    o_ref[...] = acc.astype(o_ref.dtype)


def _merge(ret, sb, ml, p, w_branch, tm, tn):
    batch, _, seq, _ = p.shape
    d = w_branch.shape[-1]
    tm = min(tm, seq)
    spb = seq // tm
    gcb = tn // LANE
    bspec = pl.BlockSpec((1, N_HEADS, tm, LANE), lambda j, i: (i // spb, 0, i % spb, 0))

    def gspec(n):
        first = (CB_GATE + n * (d // LANE)) // gcb
        return pl.BlockSpec((1, gcb, tm, LANE), lambda j, i: (i // spb, first + j, i % spb, 0))

    return pl.pallas_call(
        _merge_kernel,
        grid=(d // tn, batch * spb),
        in_specs=[bspec, bspec, bspec, gspec(0), gspec(1), gspec(2),
                  pl.BlockSpec((3, BRANCH_W, tn), lambda j, i: (0, 0, j))],
        out_specs=pl.BlockSpec((tm, tn), lambda j, i: (i, j)),
        out_shape=jax.ShapeDtypeStruct((batch * seq, d), BF16),
        compiler_params=_cparams("parallel", "parallel"),
        name="branch_merge",
    )(ret, sb, ml, p, p, p, w_branch)


def _route(logits):
    lane = lax.broadcasted_iota(jnp.int32, logits.shape, 1)
    neg = -jnp.inf
    big = jnp.int32(1 << 20)
    is_g = lane < N_GROUPS
    gl = jnp.where(is_g, logits, neg)
    gm = jnp.max(gl, axis=-1, keepdims=True)
    g_sel = jnp.min(jnp.where(gl == gm, lane, big), axis=-1, keepdims=True)
    g_w = 1.0 / jnp.sum(jnp.where(is_g, jnp.exp(gl - gm), 0.0), axis=-1, keepdims=True)
    lo = N_GROUPS + EXPERTS_PER_GROUP * g_sel
    el = jnp.where((lane >= lo) & (lane < lo + EXPERTS_PER_GROUP), logits, neg)
    t1 = jnp.max(el, axis=-1, keepdims=True)
    i1 = jnp.min(jnp.where(el == t1, lane, big), axis=-1, keepdims=True)
    el2 = jnp.where(lane == i1, neg, el)
    t2 = jnp.max(el2, axis=-1, keepdims=True)
    i2 = jnp.min(jnp.where(el2 == t2, lane, big), axis=-1, keepdims=True)
    dd = jnp.exp(t2 - t1)
    w1 = g_w / (1.0 + dd)
    w2 = g_w * dd / (1.0 + dd)
    e1 = (i1 - N_GROUPS).astype(F32)
    e2 = (i2 - N_GROUPS).astype(F32)
    return jnp.where(lane == 0, e1, jnp.where(lane == 1, e2, jnp.where(lane == 2, w1, jnp.where(lane == 3, w2, 0.0))))


def _to_token_major(ref, x):
    rows, width = x.shape
    nc = width // LANE
    for c in range(nc):
        ref[pl.ds(c, rows, stride=nc), :] = x[:, c * LANE:(c + 1) * LANE]


def _from_token_major(ref, rows):
    nc = ref.shape[0] // rows
    return [ref[pl.ds(c, rows, stride=nc), :] for c in range(nc)]


def _out_ln_router_kernel(m_ref, w_ref, x_ref, g_ref, b_ref, rwh_ref, rwl_ref, rb_ref, x1_ref, x1t_ref, r_ref, *, alpha):
    mix = _dot(m_ref[...], w_ref[...])
    x1 = _layer_norm(alpha * x_ref[...] + mix, g_ref[...], b_ref[...])
    x1_ref[...] = x1
    _to_token_major(x1t_ref, x1)
    xh = x1.astype(BF16)
    xl = (x1 - xh.astype(F32)).astype(BF16)
    logits = _dot(xh, rwh_ref[...]) + _dot(xl, rwh_ref[...]) + _dot(xh, rwl_ref[...]) + rb_ref[...]
    r_ref[...] = jnp.transpose(_route(logits))[:SUBLANE]


def _out_ln_router(merged, w_out, x, ln_g, ln_b, rw_hi, rw_lo, rb, alpha, tm):
    n, d = x.shape
    tm = min(tm, n)
    row = lambda w: pl.BlockSpec((tm, w), lambda i: (i, 0))
    full = lambda a: pl.BlockSpec(a.shape, lambda i: (0, 0))
    return pl.pallas_call(
        functools.partial(_out_ln_router_kernel, alpha=alpha),
        grid=(n // tm,),
        in_specs=[row(d), full(w_out), row(d), full(ln_g), full(ln_b), full(rw_hi), full(rw_lo), full(rb)],
        out_specs=[row(d), pl.BlockSpec((tm * (d // LANE), LANE), lambda i: (i, 0)),
                   pl.BlockSpec((SUBLANE, tm), lambda i: (0, i))],
        out_shape=[jax.ShapeDtypeStruct((n, d), F32), jax.ShapeDtypeStruct((n * (d // LANE), LANE), F32),
                   jax.ShapeDtypeStruct((SUBLANE, n), F32)],
        compiler_params=_cparams("parallel"),
        name="out_proj_ln_router",
    )(merged, w_out, x, ln_g, ln_b, rw_hi, rw_lo, rb)


def _moe_kernel(bexp_ref, nact_ref, nval_ref, tok_ref, dst_ref,
                x_hbm, gate_ref, w1_ref, w3_ref, w2_ref, out_hbm, xbuf, xbf, ybuf, sem_in, sem_out, *, bm, nc):
    del bexp_ref
    i = pl.program_id(0)
    nact = nact_ref[0]

    def gather(r, blk):
        src = pl.multiple_of(tok_ref[blk * bm + r], nc)
        dst = pl.multiple_of(r * nc, nc)
        return pltpu.make_async_copy(x_hbm.at[pl.ds(src, nc), :], xbuf.at[pl.ds(dst, nc), :], sem_in)

    def scatter(r, blk):
        src = pl.multiple_of(r * nc, nc)
        dst = pl.multiple_of(dst_ref[blk * bm + r], nc)
        return pltpu.make_async_copy(ybuf.at[pl.ds(src, nc), :], out_hbm.at[pl.ds(dst, nc), :], sem_out)

    def wait_scatter(nv):
        rows = pl.multiple_of(nv * nc, nc)

        @pl.when(nv > 0)
        def _():
            pltpu.make_async_copy(ybuf.at[pl.ds(0, rows), :], out_hbm.at[pl.ds(0, rows), :], sem_out).wait()

    @pl.when(i == 0)
    def _():
        def first(r, carry):
            gather(r, 0).start()
            return carry

        lax.fori_loop(0, bm, first, 0)

    @pl.when(i < nact)
    def _():
        pltpu.make_async_copy(x_hbm.at[pl.ds(0, bm * nc), :], xbuf, sem_in).wait()
        for c, chunk in enumerate(_from_token_major(xbuf, bm)):
            xbf[:, c * LANE:(c + 1) * LANE] = chunk.astype(BF16)
        has_next = i + 1 < nact
        prev = jnp.maximum(i - 1, 0)
        nv_prev = jnp.where(i >= 1, nval_ref[prev], 0)
        for r in range(bm):
            next_row = gather(r, i + 1)
            prev_row = scatter(r, prev)
            pl.when(has_next)(next_row.start)
            pl.when(r < nv_prev)(prev_row.start)

        xb = xbf[...]
        h1 = _dot(xb, w1_ref[0])
        h3 = _dot(xb, w3_ref[0])
        hh = (h1 * jax.nn.sigmoid(h1) * h3).astype(BF16)
        wait_scatter(nv_prev)
        _to_token_major(ybuf, _dot(hh, w2_ref[0]) * gate_ref[...])

    @pl.when(i == nact)
    def _():
        nv = nval_ref[i - 1]

        def last(r, carry):
            scatter(r, i - 1).start()
            return carry

        lax.fori_loop(0, nv, last, 0)
        wait_scatter(nv)


def _moe_experts(x1t, route_t, w1, w3, w2, bm):
    ne, d, de = w1.shape
    nc = d // LANE
    n = x1t.shape[0] // nc
    a = 2 * n
    p_rows = a + ne * bm
    nb = p_rows // bm
    eid = route_t[:2].reshape(a).astype(jnp.int32)
    gate = route_t[2:4].reshape(a)
    order = jnp.argsort(eid, stable=True).astype(jnp.int32)
    experts = jnp.arange(ne, dtype=jnp.int32)
    counts = jnp.sum((eid[None, :] == experts[:, None]).astype(jnp.int32), axis=1)
    padded = (counts + bm - 1) // bm * bm
    start = jnp.cumsum(counts) - counts
    pend = jnp.cumsum(padded)
    pstart = pend - padded
    blk = jnp.arange(nb, dtype=jnp.int32)
    blk_start = blk * bm
    blk_exp = jnp.minimum(jnp.sum((pend[None, :] <= blk_start[:, None]).astype(jnp.int32), axis=1), ne - 1)
    n_active = (pend[-1] // bm).astype(jnp.int32).reshape(1)
    pos = blk_start[:, None] - pstart[blk_exp][:, None] + jnp.arange(bm, dtype=jnp.int32)[None, :]
    valid = (pos < counts[blk_exp][:, None]) & (blk[:, None] < n_active[0])
    blk_valid = jnp.sum(valid.astype(jnp.int32), axis=1)
    src = jnp.clip(start[blk_exp][:, None] + pos, 0, a - 1)
    row_dst = jnp.where(valid, order[src], 0).reshape(p_rows)
    row_tok = jnp.where(row_dst >= n, row_dst - n, row_dst)
    row_gate = jnp.where(valid.reshape(p_rows), gate[row_dst], 0.0).reshape(p_rows, 1)
    blk_exp = jnp.where(blk < n_active[0], blk_exp, blk_exp[jnp.maximum(n_active[0] - 1, 0)])

    grid_spec = pltpu.PrefetchScalarGridSpec(
        num_scalar_prefetch=5,
        grid=(nb,),
        in_specs=[pl.BlockSpec(memory_space=pl.ANY),
                  pl.BlockSpec((bm, 1), lambda i, be, *_: (i, 0)),
                  pl.BlockSpec((1, d, de), lambda i, be, *_: (be[i], 0, 0)),
                  pl.BlockSpec((1, d, de), lambda i, be, *_: (be[i], 0, 0)),
                  pl.BlockSpec((1, de, d), lambda i, be, *_: (be[i], 0, 0))],
        out_specs=pl.BlockSpec(memory_space=pl.ANY),
        scratch_shapes=[pltpu.VMEM((bm * nc, LANE), F32), pltpu.VMEM((bm, d), BF16), pltpu.VMEM((bm * nc, LANE), F32),
                        pltpu.SemaphoreType.DMA(()), pltpu.SemaphoreType.DMA(())],
    )
    return pl.pallas_call(
        functools.partial(_moe_kernel, bm=bm, nc=nc),
        grid_spec=grid_spec,
        out_shape=jax.ShapeDtypeStruct((a * nc, LANE), F32),
        compiler_params=_cparams("arbitrary"),
        name="moe_experts",
    )(blk_exp, n_active, blk_valid, row_tok * nc, row_dst * nc, x1t, row_gate, w1, w3, w2)


def _combine_ln_kernel(x_ref, y0_ref, y1_ref, g_ref, b_ref, o_ref, obf_ref, *, alpha):
    rows = x_ref.shape[0]
    y = jnp.concatenate([a + b for a, b in zip(_from_token_major(y0_ref, rows), _from_token_major(y1_ref, rows))],
                        axis=-1)
    x2 = _layer_norm(alpha * x_ref[...] + y, g_ref[...], b_ref[...])
    o_ref[...] = x2
    obf_ref[...] = x2.astype(BF16)


def _combine_ln(x1, y2, ln_g, ln_b, alpha, tm):
    n, d = x1.shape
    tm = min(tm, n)
    steps = n // tm
    nc = d // LANE
    return pl.pallas_call(
        functools.partial(_combine_ln_kernel, alpha=alpha),
        grid=(steps,),
        in_specs=[pl.BlockSpec((tm, d), lambda i: (i, 0)), pl.BlockSpec((tm * nc, LANE), lambda i: (i, 0)),
                  pl.BlockSpec((tm * nc, LANE), lambda i: (steps + i, 0)),
                  pl.BlockSpec((1, d), lambda i: (0, 0)), pl.BlockSpec((1, d), lambda i: (0, 0))],
        out_specs=[pl.BlockSpec((tm, d), lambda i: (i, 0)), pl.BlockSpec((tm, d), lambda i: (i, 0))],
        out_shape=[jax.ShapeDtypeStruct((n, d), F32), jax.ShapeDtypeStruct((n, d), BF16)],
        compiler_params=_cparams("parallel"),
        name="combine_ln",
    )(x1, y2, y2, ln_g, ln_b)


def _rot_half_cols(w):
    half = w.shape[-1] // 2
    return jnp.concatenate([-w[..., half:], w[..., :half]], axis=-1)


def _rope_cos_sin(positions, dim):
    inv = 1.0 / (ROPE_BASE ** (jnp.arange(0, dim, 2, dtype=F32) / dim))
    ang = positions.astype(F32)[..., None] * inv
    return jnp.cos(ang), jnp.sin(ang)


def kernel(x, positions, w_in, mla_q_norm, mla_w_q_b, mla_kv_norm, mla_w_kv_b, w_branch, w_out, ln1_g, ln1_b, router_group_w, router_group_b, router_expert_w, router_expert_b, expert_w1, expert_w3, expert_w2, ln2_g, ln2_b):
    batch, seq, d = x.shape
    depth = w_in.shape[0]
    n = batch * seq
    alpha = (2 * depth) ** 0.25
    n_main = (CB_MQ + 2 * MLA_LORA // LANE) * LANE

    cr, sr = _rope_cos_sin(positions, HEAD_W)
    cos_r = jnp.concatenate([cr, cr], axis=-1)
    sin_r = jnp.concatenate([-sr, sr], axis=-1)
    cm, sm = _rope_cos_sin(positions, MLA_ROPE)
    q_scale = (MLA_NOPE + MLA_ROPE) ** -0.5
    tq_tab = q_scale * jnp.concatenate([jnp.ones((batch, seq, MLA_NOPE), F32), cm, cm, sm, sm], axis=-1)
    tk_tab = jnp.concatenate([cm, cm, sm, sm], axis=-1)

    xf = x.reshape(n, d)
    xb = xf.astype(BF16)
    for l in range(depth):
        w_kr = w_in[l][:, n_main:n_main + MLA_ROPE]
        w_main = jnp.concatenate([w_in[l][:, :n_main], w_in[l][:, n_main + MLA_ROPE:]], axis=-1).astype(BF16)
        w_kr2 = jnp.concatenate([w_kr, _rot_half_cols(w_kr)], axis=-1).astype(BF16)
        wq = mla_w_q_b[l].reshape(MLA_LORA, N_HEADS, MLA_NOPE + MLA_ROPE)
        wq_pe = wq[..., MLA_NOPE:]
        wq = jnp.concatenate([wq[..., :MLA_NOPE], wq_pe, _rot_half_cols(wq_pe)], axis=-1)
        wq = wq.reshape(MLA_LORA, N_HEADS * 2 * LANE).astype(BF16)
        wkv = mla_w_kv_b[l].astype(BF16)
        rw = jnp.concatenate([router_group_w[l], router_expert_w[l],
                              jnp.zeros((d, LANE - N_GROUPS - N_EXPERTS), F32)], axis=-1)
        rw_hi = rw.astype(BF16)
        rw_lo = (rw - rw_hi.astype(F32)).astype(BF16)
        rb = jnp.concatenate([router_group_b[l], router_expert_b[l],
                              jnp.zeros((LANE - N_GROUPS - N_EXPERTS,), F32)]).reshape(1, LANE)

        p = _proj(xb, w_main, batch, seq, tm=1024, tn=1024)
        kr = _proj(xb, w_kr2, batch, seq, tm=1024, tn=LANE)
        ret = _retention(p, cos_r, sin_r, rows_per_step=512)
        sb = _stick_breaking(p, tq=256)
        q, kn, v, kpe = _mla_prep(p, kr, tq_tab, tk_tab, mla_q_norm[l].reshape(1, -1), mla_kv_norm[l].reshape(1, -1),
                                  wq, wkv, tm=512)
        ml = _mla_attn(q, kn, kpe, v, tq=256)
        merged = _merge(ret, sb, ml, p, w_branch[l].astype(BF16), tm=512, tn=1024)
        x1, x1t, route = _out_ln_router(merged, w_out[l].astype(BF16), xf, ln1_g[l].reshape(1, d),
                                        ln1_b[l].reshape(1, d), rw_hi, rw_lo, rb, alpha, tm=256)
        y2 = _moe_experts(x1t, route, expert_w1[l].astype(BF16), expert_w3[l].astype(BF16),
                          expert_w2[l].astype(BF16), MOE_ROWS)
        xf, xb = _combine_ln(x1, y2, ln2_g[l].reshape(1, d), ln2_b[l].reshape(1, d), alpha, tm=512)
    return xf.reshape(batch, seq, d)
```

```python
import functools

import numpy as np
import jax
import jax.numpy as jnp
from jax import lax
from jax.experimental import pallas as pl
from jax.experimental.pallas import tpu as pltpu

F32 = jnp.float32
BF16 = jnp.bfloat16

LANE = 128
SUBLANE = 8
N_HEADS = 8
HEAD_W = 128
BRANCH_W = N_HEADS * HEAD_W
RET_CHUNK = 128
MLA_LORA = 512
MLA_NOPE = 128
MLA_ROPE = 64
ROPE_BASE = 10000.0
N_GROUPS = 4
EXPERTS_PER_GROUP = 8
N_EXPERTS = N_GROUPS * EXPERTS_PER_GROUP
NORM_EPS = 1e-5
MOE_ROWS = 256
VMEM_LIMIT = 56 * 1024 * 1024

CB_RQ, CB_RK, CB_RV, CB_RG = 0, 8, 16, 24
CB_SQ, CB_SK, CB_SV = 32, 40, 48
CB_MQ, CB_MKV = 56, 60
CB_GATE = 64
N_CB = 112


def _cparams(*sem):
    return pltpu.CompilerParams(dimension_semantics=sem, vmem_limit_bytes=VMEM_LIMIT)


def _dot(a, b):
    return jnp.dot(a, b, preferred_element_type=F32)


def _dot_nt(a, b):
    return lax.dot_general(a, b, (((1,), (1,)), ((), ())), preferred_element_type=F32)


def _layer_norm(y, g, b):
    mu = jnp.mean(y, axis=-1, keepdims=True)
    d = y - mu
    var = jnp.mean(d * d, axis=-1, keepdims=True)
    return d * lax.rsqrt(var + NORM_EPS) * g + b


def _proj_kernel(x_ref, w_ref, o_ref):
    acc = _dot(x_ref[...], w_ref[...])
    for h in range(o_ref.shape[1]):
        o_ref[0, h] = acc[:, h * LANE:(h + 1) * LANE].astype(o_ref.dtype)


def _proj(x_bf, w, batch, seq, tm, tn):
    n, k = x_bf.shape
    c = w.shape[1]
    tm = min(tm, seq)
    spb = seq // tm
    return pl.pallas_call(
        _proj_kernel,
        grid=(n // tm, c // tn),
        in_specs=[pl.BlockSpec((tm, k), lambda i, j: (i, 0)),
                  pl.BlockSpec((k, tn), lambda i, j: (0, j))],
        out_specs=pl.BlockSpec((1, tn // LANE, tm, LANE), lambda i, j: (i // spb, j, i % spb, 0)),
        out_shape=jax.ShapeDtypeStruct((batch, c // LANE, seq, LANE), BF16),
        compiler_params=_cparams("parallel", "parallel"),
        name="in_proj",
    )(x_bf, w)


def _retention_tables():
    h = np.arange(N_HEADS, dtype=np.float64)
    log_gamma = np.log1p(-np.exp2(-5.0 - h))
    pos = np.arange(RET_CHUNK, dtype=np.float64)
    rel = pos[:, None] - pos[None, :]
    intra = np.where(rel >= 0, np.exp(log_gamma[:, None, None] * np.maximum(rel, 0.0)), 0.0)
    ones = np.ones((1, 1, RET_CHUNK))
    qdec = np.exp(log_gamma[:, None] * (pos + 1.0))[:, :, None] * ones
    kdec = np.exp(log_gamma[:, None] * (RET_CHUNK - 1 - pos))[:, :, None] * ones
    cdec = [float(np.float32(np.exp(lg * RET_CHUNK))) for lg in log_gamma]
    return (jnp.asarray(intra, F32), jnp.asarray(qdec, F32), jnp.asarray(kdec, F32), cdec)


def _retention_kernel(q_ref, k_ref, v_ref, g_ref, cos_ref, sin_ref, intra_ref, qdec_ref, kdec_ref,
                      o_ref, state_ref, *, n_chunk, cdec):
    @pl.when(pl.program_id(1) == 0)
    def _():
        state_ref[...] = jnp.zeros_like(state_ref)

    k_scale = HEAD_W ** -0.5
    for c in range(n_chunk):
        rows = slice(c * RET_CHUNK, (c + 1) * RET_CHUNK)
        cos = cos_ref[0, rows, :]
        sin = sin_ref[0, rows, :]
        for h in range(N_HEADS):
            q = q_ref[0, h, rows, :].astype(F32)
            k = k_ref[0, h, rows, :].astype(F32)
            q = q * cos + pltpu.roll(q, HEAD_W // 2, 1) * sin
            k = (k * cos + pltpu.roll(k, HEAD_W // 2, 1) * sin) * k_scale
            v = v_ref[0, h, rows, :]
            st = state_ref[h]
            scores = _dot_nt(q.astype(BF16), k.astype(BF16)) * intra_ref[h]
            o = _dot(scores.astype(BF16), v) + _dot((q * qdec_ref[h]).astype(BF16), st.astype(BF16))
            k_dec_t = jnp.transpose(k * kdec_ref[h]).astype(BF16)
            state_ref[h] = st * cdec[h] + _dot(k_dec_t, v)
            o = o * lax.rsqrt(jnp.mean(o * o, axis=-1, keepdims=True) + NORM_EPS)
            g = g_ref[0, h, rows, :].astype(F32)
            o_ref[0, h, rows, :] = (o * (g * jax.nn.sigmoid(g))).astype(o_ref.dtype)


def _retention(p, cos_r, sin_r, rows_per_step):
    batch, _, seq, _ = p.shape
    r = min(rows_per_step, seq)
    intra, qdec, kdec, cdec = _retention_tables()

    def pspec(cb):
        return pl.BlockSpec((1, N_HEADS, r, LANE), lambda b, i: (b, cb // N_HEADS, i, 0))

    tspec = pl.BlockSpec((1, r, LANE), lambda b, i: (b, i, 0))
    cspec = pl.BlockSpec((N_HEADS, RET_CHUNK, RET_CHUNK), lambda b, i: (0, 0, 0))
    return pl.pallas_call(
        functools.partial(_retention_kernel, n_chunk=r // RET_CHUNK, cdec=cdec),
        grid=(batch, seq // r),
        in_specs=[pspec(CB_RQ), pspec(CB_RK), pspec(CB_RV), pspec(CB_RG), tspec, tspec, cspec, cspec, cspec],
        out_specs=pl.BlockSpec((1, N_HEADS, r, LANE), lambda b, i: (b, 0, i, 0)),
        out_shape=jax.ShapeDtypeStruct((batch, N_HEADS, seq, LANE), BF16),
        scratch_shapes=[pltpu.VMEM((N_HEADS, HEAD_W, HEAD_W), F32)],
        compiler_params=_cparams("parallel", "arbitrary"),
        name="retention",
    )(p, p, p, p, cos_r, sin_r, intra, qdec, kdec)


def _sb_kernel(q_ref, k_ref, v_ref, u2_ref, o_ref, *, tq, nq):
    i = pl.program_id(2)
    u2 = u2_ref[...]
    scale = HEAD_W ** -0.5
    row = lax.broadcasted_iota(jnp.int32, (tq, tq), 0)
    col = lax.broadcasted_iota(jnp.int32, (tq, tq), 1)
    strict = col < row

    def attend(nblk):
        for hh in range(q_ref.shape[1]):
            attend_head(nblk, hh)

    def attend_head(nblk, hh):
        z = _dot_nt(q_ref[0, hh], k_ref[0, hh, :nblk * tq, :]) * scale
        log_beta = jnp.minimum(z, 0.0) - jnp.log(1.0 + jnp.exp(-jnp.abs(z)))
        log_keep = log_beta - z
        c = jnp.zeros((tq, 1), F32)
        ws = [None] * nblk
        for b in reversed(range(nblk)):
            diag = b == nblk - 1
            lk = log_keep[:, b * tq:(b + 1) * tq]
            if diag:
                lk = jnp.where(strict, lk, 0.0)
            hi = lk.astype(BF16)
            lo = (lk - hi.astype(F32)).astype(BF16)
            after = _dot(jnp.concatenate([hi, lo], axis=-1), u2)
            w = jnp.exp(log_beta[:, b * tq:(b + 1) * tq] + after + c)
            if diag:
                w = jnp.where(strict, w, 0.0)
            ws[b] = w.astype(BF16)
            c = c + jnp.sum(lk, axis=-1, keepdims=True)
        w_all = ws[0] if nblk == 1 else jnp.concatenate(ws, axis=-1)
        o_ref[0, hh] = _dot(w_all, v_ref[0, hh, :nblk * tq, :]).astype(o_ref.dtype)

    for nblk in range(1, nq + 1):
        pl.when(i == nblk - 1)(functools.partial(attend, nblk))


def _stick_breaking(p, tq, hp):
    batch, _, seq, _ = p.shape
    tq = min(tq, seq)
    idx = np.arange(tq)
    u = idx[:, None] > idx[None, :]
    u2 = jnp.asarray(np.concatenate([u, u], axis=0), BF16)
    return pl.pallas_call(
        functools.partial(_sb_kernel, tq=tq, nq=seq // tq),
        grid=(batch, N_HEADS // hp, seq // tq),
        in_specs=[pl.BlockSpec((1, hp, tq, LANE), lambda b, h, i: (b, CB_SQ // hp + h, i, 0)),
                  pl.BlockSpec((1, hp, seq, LANE), lambda b, h, i: (b, CB_SK // hp + h, 0, 0)),
                  pl.BlockSpec((1, hp, seq, LANE), lambda b, h, i: (b, CB_SV // hp + h, 0, 0)),
                  pl.BlockSpec((2 * tq, tq), lambda b, h, i: (0, 0))],
        out_specs=pl.BlockSpec((1, hp, tq, LANE), lambda b, h, i: (b, h, i, 0)),
        out_shape=jax.ShapeDtypeStruct((batch, N_HEADS, seq, LANE), BF16),
        compiler_params=_cparams("parallel", "parallel", "arbitrary"),
        name="stick_breaking",
    )(p, p, p, u2)


def _mla_prep_kernel(mq_ref, mkv_ref, kr_ref, tq_ref, tk_ref, qn_ref, kvn_ref, wq_ref, wkv_ref,
                     q_out, kn_out, v_out, kpe_out):
    def rms(ref, g_ref):
        x = jnp.concatenate([ref[0, c].astype(F32) for c in range(MLA_LORA // LANE)], axis=-1)
        y = x * lax.rsqrt(jnp.mean(x * x, axis=-1, keepdims=True) + NORM_EPS)
        return (y * g_ref[...]).astype(BF16)

    qf = _dot(rms(mq_ref, qn_ref), wq_ref[...])
    tq = tq_ref[0]
    for h in range(N_HEADS):
        t = qf[:, h * 2 * LANE:(h + 1) * 2 * LANE] * tq
        u = t[:, LANE:]
        pe = u + pltpu.roll(u, LANE // 2, 1)
        q_out[0, h] = jnp.concatenate([t[:, :LANE], pe], axis=-1).astype(q_out.dtype)
    kv = _dot(rms(mkv_ref, kvn_ref), wkv_ref[...])
    for h in range(N_HEADS):
        kn_out[0, h] = kv[:, h * 2 * LANE:h * 2 * LANE + LANE].astype(kn_out.dtype)
        v_out[0, h] = kv[:, h * 2 * LANE + LANE:(h + 1) * 2 * LANE].astype(v_out.dtype)
    t = kr_ref[0, 0].astype(F32) * tk_ref[0]
    kp = t + pltpu.roll(t, LANE // 2, 1)
    lane = lax.broadcasted_iota(jnp.int32, kp.shape, 1)
    kpe_out[0] = jnp.where(lane < MLA_ROPE, kp, 0.0).astype(kpe_out.dtype)


def _mla_prep(p, kr, tq_tab, tk_tab, q_norm, kv_norm, wq, wkv, tm):
    batch, _, seq, _ = p.shape
    tm = min(tm, seq)
    nl = MLA_LORA // LANE
    head_out = lambda w: pl.BlockSpec((1, N_HEADS, tm, w), lambda b, i: (b, 0, i, 0))
    return pl.pallas_call(
        _mla_prep_kernel,
        grid=(batch, seq // tm),
        in_specs=[pl.BlockSpec((1, nl, tm, LANE), lambda b, i: (b, CB_MQ // nl, i, 0)),
                  pl.BlockSpec((1, nl, tm, LANE), lambda b, i: (b, CB_MKV // nl, i, 0)),
                  pl.BlockSpec((1, 1, tm, LANE), lambda b, i: (b, 0, i, 0)),
                  pl.BlockSpec((1, tm, 2 * LANE), lambda b, i: (b, i, 0)),
                  pl.BlockSpec((1, tm, LANE), lambda b, i: (b, i, 0)),
                  pl.BlockSpec((1, MLA_LORA), lambda b, i: (0, 0)),
                  pl.BlockSpec((1, MLA_LORA), lambda b, i: (0, 0)),
                  pl.BlockSpec(wq.shape, lambda b, i: (0, 0)),
                  pl.BlockSpec(wkv.shape, lambda b, i: (0, 0))],
        out_specs=[head_out(2 * LANE), head_out(LANE), head_out(LANE),
                   pl.BlockSpec((1, tm, LANE), lambda b, i: (b, i, 0))],
        out_shape=[jax.ShapeDtypeStruct((batch, N_HEADS, seq, 2 * LANE), BF16),
                   jax.ShapeDtypeStruct((batch, N_HEADS, seq, LANE), BF16),
                   jax.ShapeDtypeStruct((batch, N_HEADS, seq, LANE), BF16),
                   jax.ShapeDtypeStruct((batch, seq, LANE), BF16)],
        compiler_params=_cparams("parallel", "parallel"),
        name="mla_prep",
    )(p, p, kr, tq_tab, tk_tab, q_norm, kv_norm, wq, wkv)


def _mla_attn_kernel(q_ref, kn_ref, kpe_ref, v_ref, o_ref, *, tq, nq):
    i = pl.program_id(2)
    row = lax.broadcasted_iota(jnp.int32, (tq, tq), 0)
    col = lax.broadcasted_iota(jnp.int32, (tq, tq), 1)
    causal = col <= row

    def attend(nblk):
        for hh in range(q_ref.shape[1]):
            attend_head(nblk, hh)

    def attend_head(nblk, hh):
        n_keys = nblk * tq
        k = jnp.concatenate([kn_ref[0, hh, :n_keys, :], kpe_ref[0, :n_keys, :]], axis=-1)
        s = _dot_nt(q_ref[0, hh], k)
        last = jnp.where(causal, s[:, n_keys - tq:], -1e30)
        s = last if nblk == 1 else jnp.concatenate([s[:, :n_keys - tq], last], axis=-1)
        pr = jnp.exp(s - jnp.max(s, axis=-1, keepdims=True))
        l = jnp.sum(pr, axis=-1, keepdims=True)
        o_ref[0, hh] = (_dot(pr.astype(BF16), v_ref[0, hh, :n_keys, :]) / l).astype(o_ref.dtype)

    for nblk in range(1, nq + 1):
        pl.when(i == nblk - 1)(functools.partial(attend, nblk))


def _mla_attn(q, kn, kpe, v, tq, hp):
    batch, _, seq, _ = q.shape
    tq = min(tq, seq)
    return pl.pallas_call(
        functools.partial(_mla_attn_kernel, tq=tq, nq=seq // tq),
        grid=(batch, N_HEADS // hp, seq // tq),
        in_specs=[pl.BlockSpec((1, hp, tq, 2 * LANE), lambda b, h, i: (b, h, i, 0)),
                  pl.BlockSpec((1, hp, seq, LANE), lambda b, h, i: (b, h, 0, 0)),
                  pl.BlockSpec((1, seq, LANE), lambda b, h, i: (b, 0, 0)),
                  pl.BlockSpec((1, hp, seq, LANE), lambda b, h, i: (b, h, 0, 0))],
        out_specs=pl.BlockSpec((1, hp, tq, LANE), lambda b, h, i: (b, h, i, 0)),
        out_shape=jax.ShapeDtypeStruct((batch, N_HEADS, seq, LANE), BF16),
        compiler_params=_cparams("parallel", "parallel", "arbitrary"),
        name="mla_attn",
    )(q, kn, kpe, v)


def _merge_kernel(r_ref, s_ref, m_ref, g0_ref, g1_ref, g2_ref, w_ref, o_ref):
    acc = None
    for n, (b_ref, g_ref) in enumerate(((r_ref, g0_ref), (s_ref, g1_ref), (m_ref, g2_ref))):
        a = jnp.concatenate([b_ref[0, h] for h in range(N_HEADS)], axis=-1)
        y = _dot(a, w_ref[n])
        g = jax.nn.sigmoid(jnp.concatenate([g_ref[0, c].astype(F32) for c in range(g_ref.shape[1])], axis=-1))
        acc = g * y if acc is None else acc + g * y
    o_ref[...] = acc.astype(o_ref.dtype)


def _merge(ret, sb, ml, p, w_branch, tm, tn):
    batch, _, seq, _ = p.shape
    d = w_branch.shape[-1]
    tm = min(tm, seq)
    spb = seq // tm
    gcb = tn // LANE
    bspec = pl.BlockSpec((1, N_HEADS, tm, LANE), lambda j, i: (i // spb, 0, i % spb, 0))

    def gspec(n):
        first = (CB_GATE + n * (d // LANE)) // gcb
        return pl.BlockSpec((1, gcb, tm, LANE), lambda j, i: (i // spb, first + j, i % spb, 0))

    return pl.pallas_call(
        _merge_kernel,
        grid=(d // tn, batch * spb),
        in_specs=[bspec, bspec, bspec, gspec(0), gspec(1), gspec(2),
                  pl.BlockSpec((3, BRANCH_W, tn), lambda j, i: (0, 0, j))],
        out_specs=pl.BlockSpec((tm, tn), lambda j, i: (i, j)),
        out_shape=jax.ShapeDtypeStruct((batch * seq, d), BF16),
        compiler_params=_cparams("parallel", "parallel"),
        name="branch_merge",
    )(ret, sb, ml, p, p, p, w_branch)


def _route(logits):
    lane = lax.broadcasted_iota(jnp.int32, logits.shape, 1)
    neg = -jnp.inf
    big = jnp.int32(1 << 20)
    is_g = lane < N_GROUPS
    gl = jnp.where(is_g, logits, neg)
    gm = jnp.max(gl, axis=-1, keepdims=True)
    g_sel = jnp.min(jnp.where(gl == gm, lane, big), axis=-1, keepdims=True)
    g_w = 1.0 / jnp.sum(jnp.where(is_g, jnp.exp(gl - gm), 0.0), axis=-1, keepdims=True)
    lo = N_GROUPS + EXPERTS_PER_GROUP * g_sel
    el = jnp.where((lane >= lo) & (lane < lo + EXPERTS_PER_GROUP), logits, neg)
    t1 = jnp.max(el, axis=-1, keepdims=True)
    i1 = jnp.min(jnp.where(el == t1, lane, big), axis=-1, keepdims=True)
    el2 = jnp.where(lane == i1, neg, el)
    t2 = jnp.max(el2, axis=-1, keepdims=True)
    i2 = jnp.min(jnp.where(el2 == t2, lane, big), axis=-1, keepdims=True)
    dd = jnp.exp(t2 - t1)
    w1 = g_w / (1.0 + dd)
    w2 = g_w * dd / (1.0 + dd)
    e1 = (i1 - N_GROUPS).astype(F32)
    e2 = (i2 - N_GROUPS).astype(F32)
    return jnp.where(lane == 0, e1, jnp.where(lane == 1, e2, jnp.where(lane == 2, w1, jnp.where(lane == 3, w2, 0.0))))


def _to_token_major(ref, x):
    rows, width = x.shape
    nc = width // LANE
    for c in range(nc):
        ref[pl.ds(c, rows, stride=nc), :] = x[:, c * LANE:(c + 1) * LANE]


def _from_token_major(ref, rows):
    nc = ref.shape[0] // rows
    return [ref[pl.ds(c, rows, stride=nc), :] for c in range(nc)]


def _out_ln_router_kernel(m_ref, w_ref, x_ref, g_ref, b_ref, rw_ref, rb_ref, x1_ref, x1t_ref, r_ref, *, alpha, sub):
    tm, d = x_ref.shape
    nc = d // LANE
    rw_hi = rw_ref[:, :LANE]
    for s0 in range(0, tm, sub):
        rows = slice(s0, s0 + sub)
        mix = _dot(m_ref[rows, :], w_ref[...])
        x1 = _layer_norm(alpha * x_ref[rows, :] + mix, g_ref[...], b_ref[...])
        x1_ref[rows, :] = x1
        _to_token_major(x1t_ref.at[pl.ds(s0 * nc, sub * nc), :], x1)
        xh = x1.astype(BF16)
        xl = (x1 - xh.astype(F32)).astype(BF16)
        both = _dot(xh, rw_ref[...])
        logits = both[:, :LANE] + both[:, LANE:] + _dot(xl, rw_hi) + rb_ref[...]
        r_ref[:, rows] = jnp.transpose(_route(logits))[:SUBLANE]


def _out_ln_router(merged, w_out, x, ln_g, ln_b, rw, rb, alpha, tm, sub):
    n, d = x.shape
    tm = min(tm, n)
    sub = min(sub, tm)
    row = lambda w: pl.BlockSpec((tm, w), lambda i: (i, 0))
    full = lambda a: pl.BlockSpec(a.shape, lambda i: (0, 0))
    w_spec = pl.BlockSpec(w_out.shape, lambda i: (0, 0), pipeline_mode=pl.Buffered(1))
    return pl.pallas_call(
        functools.partial(_out_ln_router_kernel, alpha=alpha, sub=sub),
        grid=(n // tm,),
        in_specs=[row(d), w_spec, row(d), full(ln_g), full(ln_b), full(rw), full(rb)],
        out_specs=[row(d), pl.BlockSpec((tm * (d // LANE), LANE), lambda i: (i, 0)),
                   pl.BlockSpec((SUBLANE, tm), lambda i: (0, i))],
        out_shape=[jax.ShapeDtypeStruct((n, d), F32), jax.ShapeDtypeStruct((n * (d // LANE), LANE), F32),
                   jax.ShapeDtypeStruct((SUBLANE, n), F32)],
        compiler_params=_cparams("parallel"),
        name="out_proj_ln_router",
    )(merged, w_out, x, ln_g, ln_b, rw, rb)


def _moe_kernel(bexp_ref, nact_ref, nval_ref, tok_ref, dst_ref,
                x_hbm, gate_ref, w1_ref, w3_ref, w2_ref, out_hbm,
                xbuf, xbf, ybuf, w1b, w3b, w2b, sem_in, sem_out, *, bm, nc):
    i = pl.program_id(0)
    nact = nact_ref[0]

    @pl.when((i < nact) & ((i == 0) | (bexp_ref[i] != bexp_ref[jnp.maximum(i - 1, 0)])))
    def _():
        w1b[...] = w1_ref[0, 0].astype(BF16)
        w3b[...] = w3_ref[0, 0].astype(BF16)
        w2b[...] = w2_ref[0, 0].astype(BF16)

    def gather(r, blk):
        src = pl.multiple_of(tok_ref[blk * bm + r], nc)
        return pltpu.make_async_copy(x_hbm.at[pl.ds(src, nc), :], xbuf.at[r // SUBLANE, :, r % SUBLANE, :], sem_in)

    def scatter(r, blk):
        dst = pl.multiple_of(dst_ref[blk * bm + r], nc)
        return pltpu.make_async_copy(ybuf.at[r // SUBLANE, :, r % SUBLANE, :], out_hbm.at[pl.ds(dst, nc), :], sem_out)

    def wait_rows(n_tok, sem):
        rows = pl.multiple_of(n_tok * nc, nc)

        @pl.when(n_tok > 0)
        def _():
            pltpu.make_async_copy(x_hbm.at[pl.ds(0, rows), :], out_hbm.at[pl.ds(0, rows), :], sem).wait()

    @pl.when(i == 0)
    def _():
        for r in range(bm):
            gather(r, 0).start(priority=r % 2)

    @pl.when(i < nact)
    def _():
        has_next = i + 1 < nact
        prev = jnp.maximum(i - 1, 0)
        nv_prev = jnp.where(i >= 1, nval_ref[prev], 0)
        nv_head = jnp.minimum(nv_prev, SUBLANE)
        for r in range(bm):
            pl.when(r < nv_prev)(functools.partial(scatter(r, prev).start, priority=r % 2))
        wait_rows(jnp.int32(bm), sem_in)
        for c in range(nc):
            xbf[:, c * LANE:(c + 1) * LANE] = xbuf[:, c].reshape(bm, LANE).astype(BF16)
        for r in range(bm):
            pl.when(has_next)(functools.partial(gather(r, i + 1).start, priority=r % 2))
        wait_rows(nv_head, sem_out)
        xb = xbf[...]
        h1 = _dot(xb, w1b[...])
        h3 = _dot(xb, w3b[...])
        hh = (h1 * jax.nn.sigmoid(h1) * h3).astype(BF16)
        wait_rows(nv_prev - nv_head, sem_out)
        y = _dot(hh, w2b[...]) * gate_ref[...]
        for c in range(nc):
            ybuf[:, c] = y[:, c * LANE:(c + 1) * LANE].reshape(bm // SUBLANE, SUBLANE, LANE)

    @pl.when(i == nact)
    def _():
        nv = nval_ref[i - 1]
        for r in range(bm):
            pl.when(r < nv)(functools.partial(scatter(r, i - 1).start, priority=r % 2))
        wait_rows(nv, sem_out)


def _moe_experts(x1t, route_t, w1, w3, w2, layer, bm):
    _, ne, d, de = w1.shape
    nc = d // LANE
    n = x1t.shape[0] // nc
    a = 2 * n
    p_rows = a + ne * bm
    nb = p_rows // bm
    eid = route_t[:2].reshape(a).astype(jnp.int32)
    gate = route_t[2:4].reshape(a)
    order = jnp.argsort(eid, stable=True).astype(jnp.int32)
    experts = jnp.arange(ne, dtype=jnp.int32)
    counts = jnp.sum((eid[None, :] == experts[:, None]).astype(jnp.int32), axis=1)
    padded = (counts + bm - 1) // bm * bm
    start = jnp.cumsum(counts) - counts
    pend = jnp.cumsum(padded)
    pstart = pend - padded
    blk = jnp.arange(nb, dtype=jnp.int32)
    blk_start = blk * bm
    blk_exp = jnp.minimum(jnp.sum((pend[None, :] <= blk_start[:, None]).astype(jnp.int32), axis=1), ne - 1)
    n_active = (pend[-1] // bm).astype(jnp.int32).reshape(1)
    pos = blk_start[:, None] - pstart[blk_exp][:, None] + jnp.arange(bm, dtype=jnp.int32)[None, :]
    valid = (pos < counts[blk_exp][:, None]) & (blk[:, None] < n_active[0])
    blk_valid = jnp.sum(valid.astype(jnp.int32), axis=1)
    src = jnp.clip(start[blk_exp][:, None] + pos, 0, a - 1)
    row_dst = jnp.where(valid, order[src], 0).reshape(p_rows)
    row_tok = jnp.where(row_dst >= n, row_dst - n, row_dst)
    row_gate = jnp.where(valid.reshape(p_rows), gate[row_dst], 0.0).reshape(p_rows, 1)
    blk_exp = jnp.where(blk < n_active[0], blk_exp, blk_exp[jnp.maximum(n_active[0] - 1, 0)])

    grid_spec = pltpu.PrefetchScalarGridSpec(
        num_scalar_prefetch=5,
        grid=(nb,),
        in_specs=[pl.BlockSpec(memory_space=pl.ANY),
                  pl.BlockSpec((bm, 1), lambda i, be, *_: (i, 0)),
                  pl.BlockSpec((1, 1, d, de), lambda i, be, *_: (layer, be[i], 0, 0)),
                  pl.BlockSpec((1, 1, d, de), lambda i, be, *_: (layer, be[i], 0, 0)),
                  pl.BlockSpec((1, 1, de, d), lambda i, be, *_: (layer, be[i], 0, 0))],
        out_specs=pl.BlockSpec(memory_space=pl.ANY),
        scratch_shapes=[pltpu.VMEM((bm // SUBLANE, nc, SUBLANE, LANE), F32), pltpu.VMEM((bm, d), BF16),
                        pltpu.VMEM((bm // SUBLANE, nc, SUBLANE, LANE), F32),
                        pltpu.VMEM((d, de), BF16), pltpu.VMEM((d, de), BF16), pltpu.VMEM((de, d), BF16),
                        pltpu.SemaphoreType.DMA(()), pltpu.SemaphoreType.DMA(())],
    )
    return pl.pallas_call(
        functools.partial(_moe_kernel, bm=bm, nc=nc),
        grid_spec=grid_spec,
        out_shape=jax.ShapeDtypeStruct((a * nc, LANE), F32),
        compiler_params=_cparams("arbitrary"),
        name="moe_experts",
    )(blk_exp, n_active, blk_valid, row_tok * nc, row_dst * nc, x1t, row_gate, w1, w3, w2)


def _combine_ln_kernel(x_ref, y0_ref, y1_ref, g_ref, b_ref, o_ref, obf_ref, *, alpha):
    rows = x_ref.shape[0]
    y = jnp.concatenate([a + b for a, b in zip(_from_token_major(y0_ref, rows), _from_token_major(y1_ref, rows))],
                        axis=-1)
    x2 = _layer_norm(alpha * x_ref[...] + y, g_ref[...], b_ref[...])
    o_ref[...] = x2
    obf_ref[...] = x2.astype(BF16)


def _combine_ln(x1, y2, ln_g, ln_b, alpha, tm):
    n, d = x1.shape
    tm = min(tm, n)
    steps = n // tm
    nc = d // LANE
    return pl.pallas_call(
        functools.partial(_combine_ln_kernel, alpha=alpha),
        grid=(steps,),
        in_specs=[pl.BlockSpec((tm, d), lambda i: (i, 0)), pl.BlockSpec((tm * nc, LANE), lambda i: (i, 0)),
                  pl.BlockSpec((tm * nc, LANE), lambda i: (steps + i, 0)),
                  pl.BlockSpec((1, d), lambda i: (0, 0)), pl.BlockSpec((1, d), lambda i: (0, 0))],
        out_specs=[pl.BlockSpec((tm, d), lambda i: (i, 0)), pl.BlockSpec((tm, d), lambda i: (i, 0))],
        out_shape=[jax.ShapeDtypeStruct((n, d), F32), jax.ShapeDtypeStruct((n, d), BF16)],
        compiler_params=_cparams("parallel"),
        name="combine_ln",
    )(x1, y2, y2, ln_g, ln_b)


def _rot_half_cols(w):
    half = w.shape[-1] // 2
    return jnp.concatenate([-w[..., half:], w[..., :half]], axis=-1)


def _rope_cos_sin(positions, dim):
    inv = 1.0 / (ROPE_BASE ** (jnp.arange(0, dim, 2, dtype=F32) / dim))
    ang = positions.astype(F32)[..., None] * inv
    return jnp.cos(ang), jnp.sin(ang)


def kernel(x, positions, w_in, mla_q_norm, mla_w_q_b, mla_kv_norm, mla_w_kv_b, w_branch, w_out, ln1_g, ln1_b, router_group_w, router_group_b, router_expert_w, router_expert_b, expert_w1, expert_w3, expert_w2, ln2_g, ln2_b):
    batch, seq, d = x.shape
    depth = w_in.shape[0]
    n = batch * seq
    alpha = (2 * depth) ** 0.25
    n_main = (CB_MQ + 2 * MLA_LORA // LANE) * LANE

    cr, sr = _rope_cos_sin(positions, HEAD_W)
    cos_r = jnp.concatenate([cr, cr], axis=-1)
    sin_r = jnp.concatenate([-sr, sr], axis=-1)
    cm, sm = _rope_cos_sin(positions, MLA_ROPE)
    q_scale = (MLA_NOPE + MLA_ROPE) ** -0.5
    tq_tab = q_scale * jnp.concatenate([jnp.ones((batch, seq, MLA_NOPE), F32), cm, cm, sm, sm], axis=-1)
    tk_tab = jnp.concatenate([cm, cm, sm, sm], axis=-1)

    xf = x.reshape(n, d)
    xb = xf.astype(BF16)
    for l in range(depth):
        w_kr = w_in[l][:, n_main:n_main + MLA_ROPE]
        w_main = jnp.concatenate([w_in[l][:, :n_main], w_in[l][:, n_main + MLA_ROPE:]], axis=-1).astype(BF16)
        w_kr2 = jnp.concatenate([w_kr, _rot_half_cols(w_kr)], axis=-1).astype(BF16)
        wq = mla_w_q_b[l].reshape(MLA_LORA, N_HEADS, MLA_NOPE + MLA_ROPE)
        wq_pe = wq[..., MLA_NOPE:]
        wq = jnp.concatenate([wq[..., :MLA_NOPE], wq_pe, _rot_half_cols(wq_pe)], axis=-1)
        wq = wq.reshape(MLA_LORA, N_HEADS * 2 * LANE).astype(BF16)
        wkv = mla_w_kv_b[l].astype(BF16)
        rw = jnp.concatenate([router_group_w[l], router_expert_w[l],
                              jnp.zeros((d, LANE - N_GROUPS - N_EXPERTS), F32)], axis=-1)
        rw_hi = rw.astype(BF16)
        rw_cat = jnp.concatenate([rw_hi, (rw - rw_hi.astype(F32)).astype(BF16)], axis=-1)
        rb =jnp.concatenate([router_group_b[l], router_expert_b[l],
                              jnp.zeros((LANE - N_GROUPS - N_EXPERTS,), F32)]).reshape(1, LANE)

        p = _proj(xb, w_main, batch, seq, tm=1024, tn=1024)
        kr = _proj(xb, w_kr2, batch, seq, tm=1024, tn=LANE)
        ret = _retention(p, cos_r, sin_r, rows_per_step=512)
        sb = _stick_breaking(p, tq=256, hp=2)
        q, kn, v, kpe = _mla_prep(p, kr, tq_tab, tk_tab, mla_q_norm[l].reshape(1, -1), mla_kv_norm[l].reshape(1, -1),
                                  wq, wkv, tm=512)
        ml = _mla_attn(q, kn, kpe, v, tq=256, hp=2)
        merged = _merge(ret, sb, ml, p, w_branch[l].astype(BF16), tm=512, tn=1024)
        x1, x1t, route = _out_ln_router(merged, w_out[l].astype(BF16), xf, ln1_g[l].reshape(1, d),
                                        ln1_b[l].reshape(1, d), rw_cat, rb, alpha, tm=512, sub=256)
        y2 = _moe_experts(x1t, route, expert_w1, expert_w3, expert_w2, l, MOE_ROWS)
        xf, xb = _combine_ln(x1, y2, ln2_g[l].reshape(1, d), ln2_b[l].reshape(1, d), alpha, tm=512)
    return xf.reshape(batch, seq, d)
```

```python
import functools

import numpy as np
import jax
import jax.numpy as jnp
from jax import lax
from jax.experimental import pallas as pl
from jax.experimental.pallas import tpu as pltpu

F32 = jnp.float32
BF16 = jnp.bfloat16

LANE = 128
SUBLANE = 8
N_HEADS = 8
HEAD_W = 128
BRANCH_W = N_HEADS * HEAD_W
RET_CHUNK = 128
MLA_LORA = 512
MLA_NOPE = 128
MLA_ROPE = 64
ROPE_BASE = 10000.0
N_GROUPS = 4
EXPERTS_PER_GROUP = 8
N_EXPERTS = N_GROUPS * EXPERTS_PER_GROUP
NORM_EPS = 1e-5
MOE_ROWS = 256
VMEM_LIMIT = 56 * 1024 * 1024

CB_RQ, CB_RK, CB_RV, CB_RG = 0, 8, 16, 24
CB_SQ, CB_SK, CB_SV = 32, 40, 48
CB_MQ, CB_MKV = 56, 60
CB_GATE = 64
N_CB = 112


def _cparams(*sem):
    return pltpu.CompilerParams(dimension_semantics=sem, vmem_limit_bytes=VMEM_LIMIT)


def _dot(a, b):
    return jnp.dot(a, b, preferred_element_type=F32)


def _dot_nt(a, b):
    return lax.dot_general(a, b, (((1,), (1,)), ((), ())), preferred_element_type=F32)


def _layer_norm(y, g, b):
    mu = jnp.mean(y, axis=-1, keepdims=True)
    d = y - mu
    var = jnp.mean(d * d, axis=-1, keepdims=True)
    return d * lax.rsqrt(var + NORM_EPS) * g + b


def _proj_kernel(x_ref, w_ref, o_ref):
    acc = _dot(x_ref[...], w_ref[...])
    for h in range(o_ref.shape[1]):
        o_ref[0, h] = acc[:, h * LANE:(h + 1) * LANE].astype(o_ref.dtype)


def _proj(x_bf, w, batch, seq, tm, tn):
    n, k = x_bf.shape
    c = w.shape[1]
    tm = min(tm, seq)
    spb = seq // tm
    return pl.pallas_call(
        _proj_kernel,
        grid=(n // tm, c // tn),
        in_specs=[pl.BlockSpec((tm, k), lambda i, j: (i, 0)),
                  pl.BlockSpec((k, tn), lambda i, j: (0, j))],
        out_specs=pl.BlockSpec((1, tn // LANE, tm, LANE), lambda i, j: (i // spb, j, i % spb, 0)),
        out_shape=jax.ShapeDtypeStruct((batch, c // LANE, seq, LANE), BF16),
        compiler_params=_cparams("parallel", "parallel"),
        name="in_proj",
    )(x_bf, w)


def _retention_tables():
    h = np.arange(N_HEADS, dtype=np.float64)
    log_gamma = np.log1p(-np.exp2(-5.0 - h))
    pos = np.arange(RET_CHUNK, dtype=np.float64)
    rel = pos[:, None] - pos[None, :]
    intra = np.where(rel >= 0, np.exp(log_gamma[:, None, None] * np.maximum(rel, 0.0)), 0.0)
    ones = np.ones((1, 1, RET_CHUNK))
    qdec = np.exp(log_gamma[:, None] * (pos + 1.0))[:, :, None] * ones
    kdec = np.exp(log_gamma[:, None] * (RET_CHUNK - 1 - pos))[:, :, None] * ones
    cdec = [float(np.float32(np.exp(lg * RET_CHUNK))) for lg in log_gamma]
    return (jnp.asarray(intra, F32), jnp.asarray(qdec, F32), jnp.asarray(kdec, F32), cdec)


def _retention_kernel(q_ref, k_ref, v_ref, g_ref, cos_ref, sin_ref, intra_ref, qdec_ref, kdec_ref,
                      o_ref, state_ref, *, n_chunk, cdec):
    @pl.when(pl.program_id(1) == 0)
    def _():
        state_ref[...] = jnp.zeros_like(state_ref)

    k_scale = HEAD_W ** -0.5
    for c in range(n_chunk):
        rows = slice(c * RET_CHUNK, (c + 1) * RET_CHUNK)
        cos = cos_ref[0, rows, :]
        sin = sin_ref[0, rows, :]
        for h in range(N_HEADS):
            q = q_ref[0, h, rows, :].astype(F32)
            k = k_ref[0, h, rows, :].astype(F32)
            q = q * cos + pltpu.roll(q, HEAD_W // 2, 1) * sin
            k = (k * cos + pltpu.roll(k, HEAD_W // 2, 1) * sin) * k_scale
            v = v_ref[0, h, rows, :]
            st = state_ref[h]
            scores = _dot_nt(q.astype(BF16), k.astype(BF16)) * intra_ref[h]
            o = _dot(scores.astype(BF16), v) + _dot((q * qdec_ref[h]).astype(BF16), st.astype(BF16))
            k_dec_t = jnp.transpose(k * kdec_ref[h]).astype(BF16)
            state_ref[h] = st * cdec[h] + _dot(k_dec_t, v)
            o = o * lax.rsqrt(jnp.mean(o * o, axis=-1, keepdims=True) + NORM_EPS)
            g = g_ref[0, h, rows, :].astype(F32)
            o_ref[0, h, rows, :] = (o * (g * jax.nn.sigmoid(g))).astype(o_ref.dtype)


def _retention(p, cos_r, sin_r, rows_per_step):
    batch, _, seq, _ = p.shape
    r = min(rows_per_step, seq)
    intra, qdec, kdec, cdec = _retention_tables()

    def pspec(cb):
        return pl.BlockSpec((1, N_HEADS, r, LANE), lambda b, i: (b, cb // N_HEADS, i, 0))

    tspec = pl.BlockSpec((1, r, LANE), lambda b, i: (b, i, 0))
    cspec = pl.BlockSpec((N_HEADS, RET_CHUNK, RET_CHUNK), lambda b, i: (0, 0, 0))
    return pl.pallas_call(
        functools.partial(_retention_kernel, n_chunk=r // RET_CHUNK, cdec=cdec),
        grid=(batch, seq // r),
        in_specs=[pspec(CB_RQ), pspec(CB_RK), pspec(CB_RV), pspec(CB_RG), tspec, tspec, cspec, cspec, cspec],
        out_specs=pl.BlockSpec((1, N_HEADS, r, LANE), lambda b, i: (b, 0, i, 0)),
        out_shape=jax.ShapeDtypeStruct((batch, N_HEADS, seq, LANE), BF16),
        scratch_shapes=[pltpu.VMEM((N_HEADS, HEAD_W, HEAD_W), F32)],
        compiler_params=_cparams("parallel", "arbitrary"),
        name="retention",
    )(p, p, p, p, cos_r, sin_r, intra, qdec, kdec)


def _sb_kernel(q_ref, k_ref, v_ref, u_ref, o_ref, *, tq, nq):
    i = pl.program_id(2)
    u = u_ref[...]
    row = lax.broadcasted_iota(jnp.int32, (tq, tq), 0)
    col = lax.broadcasted_iota(jnp.int32, (tq, tq), 1)
    strict = col < row

    def attend(nblk):
        for hh in range(q_ref.shape[1]):
            attend_head(nblk, hh)

    def attend_head(nblk, hh):
        z = _dot_nt(q_ref[0, hh], k_ref[0, hh, :nblk * tq, :])
        log_beta = jnp.minimum(z, 0.0) - jnp.log(1.0 + jnp.exp(-jnp.abs(z)))
        log_keep = log_beta - z
        c = jnp.zeros((tq, 1), F32)
        ws = [None] * nblk
        for b in reversed(range(nblk)):
            diag = b == nblk - 1
            lk = log_keep[:, b * tq:(b + 1) * tq]
            if diag:
                lk = jnp.where(strict, lk, 0.0)
            after = _dot(lk.astype(BF16), u)
            w = jnp.exp(log_beta[:, b * tq:(b + 1) * tq] + after + c)
            if diag:
                w = jnp.where(strict, w, 0.0)
            ws[b] = w.astype(BF16)
            c = c + jnp.sum(lk, axis=-1, keepdims=True)
        w_all = ws[0] if nblk == 1 else jnp.concatenate(ws, axis=-1)
        o_ref[0, hh] = _dot(w_all, v_ref[0, hh, :nblk * tq, :]).astype(o_ref.dtype)

    for nblk in range(1, nq + 1):
        pl.when(i == nblk - 1)(functools.partial(attend, nblk))


def _stick_breaking(p, tq, hp):
    batch, _, seq, _ = p.shape
    tq = min(tq, seq)
    idx = np.arange(tq)
    u = jnp.asarray(idx[:, None] > idx[None, :], BF16)
    return pl.pallas_call(
        functools.partial(_sb_kernel, tq=tq, nq=seq // tq),
        grid=(batch, N_HEADS // hp, seq // tq),
        in_specs=[pl.BlockSpec((1, hp, tq, LANE), lambda b, h, i: (b, CB_SQ // hp + h, i, 0)),
                  pl.BlockSpec((1, hp, seq, LANE), lambda b, h, i: (b, CB_SK // hp + h, 0, 0)),
                  pl.BlockSpec((1, hp, seq, LANE), lambda b, h, i: (b, CB_SV // hp + h, 0, 0)),
                  pl.BlockSpec((tq, tq), lambda b, h, i: (0, 0))],
        out_specs=pl.BlockSpec((1, hp, tq, LANE), lambda b, h, i: (b, h, i, 0)),
        out_shape=jax.ShapeDtypeStruct((batch, N_HEADS, seq, LANE), BF16),
        compiler_params=_cparams("parallel", "parallel", "arbitrary"),
        name="stick_breaking",
    )(p, p, p, u)


def _mla_prep_kernel(mq_ref, mkv_ref, kr_ref, tq_ref, tk_ref, qn_ref, kvn_ref, wq_ref, wkv_ref,
                     q_out, kn_out, v_out, kpe_out):
    def rms(ref, g_ref):
        x = jnp.concatenate([ref[0, c].astype(F32) for c in range(MLA_LORA // LANE)], axis=-1)
        y = x * lax.rsqrt(jnp.mean(x * x, axis=-1, keepdims=True) + NORM_EPS)
        return (y * g_ref[...]).astype(BF16)

    qf = _dot(rms(mq_ref, qn_ref), wq_ref[...])
    tq = tq_ref[0]
    for h in range(N_HEADS):
        t = qf[:, h * 2 * LANE:(h + 1) * 2 * LANE] * tq
        u = t[:, LANE:]
        pe = u + pltpu.roll(u, LANE // 2, 1)
        q_out[0, h] = jnp.concatenate([t[:, :LANE], pe], axis=-1).astype(q_out.dtype)
    kv = _dot(rms(mkv_ref, kvn_ref), wkv_ref[...])
    for h in range(N_HEADS):
        kn_out[0, h] = kv[:, h * 2 * LANE:h * 2 * LANE + LANE].astype(kn_out.dtype)
        v_out[0, h] = kv[:, h * 2 * LANE + LANE:(h + 1) * 2 * LANE].astype(v_out.dtype)
    t = kr_ref[0, 0].astype(F32) * tk_ref[0]
    kp = t + pltpu.roll(t, LANE // 2, 1)
    lane = lax.broadcasted_iota(jnp.int32, kp.shape, 1)
    kpe_out[0] = jnp.where(lane < MLA_ROPE, kp, 0.0).astype(kpe_out.dtype)


def _mla_prep(p, kr, tq_tab, tk_tab, q_norm, kv_norm, wq, wkv, tm):
    batch, _, seq, _ = p.shape
    tm = min(tm, seq)
    nl = MLA_LORA // LANE
    head_out = lambda w: pl.BlockSpec((1, N_HEADS, tm, w), lambda b, i: (b, 0, i, 0))
    return pl.pallas_call(
        _mla_prep_kernel,
        grid=(batch, seq // tm),
        in_specs=[pl.BlockSpec((1, nl, tm, LANE), lambda b, i: (b, CB_MQ // nl, i, 0)),
                  pl.BlockSpec((1, nl, tm, LANE), lambda b, i: (b, CB_MKV // nl, i, 0)),
                  pl.BlockSpec((1, 1, tm, LANE), lambda b, i: (b, 0, i, 0)),
                  pl.BlockSpec((1, tm, 2 * LANE), lambda b, i: (b, i, 0)),
                  pl.BlockSpec((1, tm, LANE), lambda b, i: (b, i, 0)),
                  pl.BlockSpec((1, MLA_LORA), lambda b, i: (0, 0)),
                  pl.BlockSpec((1, MLA_LORA), lambda b, i: (0, 0)),
                  pl.BlockSpec(wq.shape, lambda b, i: (0, 0)),
                  pl.BlockSpec(wkv.shape, lambda b, i: (0, 0))],
        out_specs=[head_out(2 * LANE), head_out(LANE), head_out(LANE),
                   pl.BlockSpec((1, tm, LANE), lambda b, i: (b, i, 0))],
        out_shape=[jax.ShapeDtypeStruct((batch, N_HEADS, seq, 2 * LANE), BF16),
                   jax.ShapeDtypeStruct((batch, N_HEADS, seq, LANE), BF16),
                   jax.ShapeDtypeStruct((batch, N_HEADS, seq, LANE), BF16),
                   jax.ShapeDtypeStruct((batch, seq, LANE), BF16)],
        compiler_params=_cparams("parallel", "parallel"),
        name="mla_prep",
    )(p, p, kr, tq_tab, tk_tab, q_norm, kv_norm, wq, wkv)


def _mla_attn_kernel(q_ref, kn_ref, kpe_ref, v_ref, o_ref, *, tq, nq):
    i = pl.program_id(2)
    row = lax.broadcasted_iota(jnp.int32, (tq, tq), 0)
    col = lax.broadcasted_iota(jnp.int32, (tq, tq), 1)
    causal = col <= row

    def attend(nblk):
        for hh in range(q_ref.shape[1]):
            attend_head(nblk, hh)

    def attend_head(nblk, hh):
        n_keys = nblk * tq
        k = jnp.concatenate([kn_ref[0, hh, :n_keys, :], kpe_ref[0, :n_keys, :]], axis=-1)
        s = _dot_nt(q_ref[0, hh], k)
        last = jnp.where(causal, s[:, n_keys - tq:], -1e30)
        s = last if nblk == 1 else jnp.concatenate([s[:, :n_keys - tq], last], axis=-1)
        pr = jnp.exp(s - jnp.max(s, axis=-1, keepdims=True))
        l = jnp.sum(pr, axis=-1, keepdims=True)
        o_ref[0, hh] = (_dot(pr.astype(BF16), v_ref[0, hh, :n_keys, :]) / l).astype(o_ref.dtype)

    for nblk in range(1, nq + 1):
        pl.when(i == nblk - 1)(functools.partial(attend, nblk))


def _mla_attn(q, kn, kpe, v, tq, hp):
    batch, _, seq, _ = q.shape
    tq = min(tq, seq)
    return pl.pallas_call(
        functools.partial(_mla_attn_kernel, tq=tq, nq=seq // tq),
        grid=(batch, N_HEADS // hp, seq // tq),
        in_specs=[pl.BlockSpec((1, hp, tq, 2 * LANE), lambda b, h, i: (b, h, i, 0)),
                  pl.BlockSpec((1, hp, seq, LANE), lambda b, h, i: (b, h, 0, 0)),
                  pl.BlockSpec((1, seq, LANE), lambda b, h, i: (b, 0, 0)),
                  pl.BlockSpec((1, hp, seq, LANE), lambda b, h, i: (b, h, 0, 0))],
        out_specs=pl.BlockSpec((1, hp, tq, LANE), lambda b, h, i: (b, h, i, 0)),
        out_shape=jax.ShapeDtypeStruct((batch, N_HEADS, seq, LANE), BF16),
        compiler_params=_cparams("parallel", "parallel", "arbitrary"),
        name="mla_attn",
    )(q, kn, kpe, v)


def _merge_kernel(r_ref, s_ref, m_ref, g0_ref, g1_ref, g2_ref, w_ref, o_ref):
    acc = None
    for n, (b_ref, g_ref) in enumerate(((r_ref, g0_ref), (s_ref, g1_ref), (m_ref, g2_ref))):
        a = jnp.concatenate([b_ref[0, h] for h in range(N_HEADS)], axis=-1)
        y = _dot(a, w_ref[n])
        g = jax.nn.sigmoid(jnp.concatenate([g_ref[0, c].astype(F32) for c in range(g_ref.shape[1])], axis=-1))
        acc = g * y if acc is None else acc + g * y
    o_ref[...] = acc.astype(o_ref.dtype)


def _merge(ret, sb, ml, p, w_branch, tm, tn):
    batch, _, seq, _ = p.shape
    d = w_branch.shape[-1]
    tm = min(tm, seq)
    spb = seq // tm
    gcb = tn // LANE
    bspec = pl.BlockSpec((1, N_HEADS, tm, LANE), lambda j, i: (i // spb, 0, i % spb, 0))

    def gspec(n):
        first = (CB_GATE + n * (d // LANE)) // gcb
        return pl.BlockSpec((1, gcb, tm, LANE), lambda j, i: (i // spb, first + j, i % spb, 0))

    return pl.pallas_call(
        _merge_kernel,
        grid=(d // tn, batch * spb),
        in_specs=[bspec, bspec, bspec, gspec(0), gspec(1), gspec(2),
                  pl.BlockSpec((3, BRANCH_W, tn), lambda j, i: (0, 0, j))],
        out_specs=pl.BlockSpec((tm, tn), lambda j, i: (i, j)),
        out_shape=jax.ShapeDtypeStruct((batch * seq, d), BF16),
        compiler_params=_cparams("parallel", "parallel"),
        name="branch_merge",
    )(ret, sb, ml, p, p, p, w_branch)


def _route(logits):
    lane = lax.broadcasted_iota(jnp.int32, logits.shape, 1)
    neg = -jnp.inf
    big = jnp.int32(1 << 20)
    is_g = lane < N_GROUPS
    gl = jnp.where(is_g, logits, neg)
    gm = jnp.max(gl, axis=-1, keepdims=True)
    g_sel = jnp.min(jnp.where(gl == gm, lane, big), axis=-1, keepdims=True)
    g_w = 1.0 / jnp.sum(jnp.where(is_g, jnp.exp(gl - gm), 0.0), axis=-1, keepdims=True)
    lo = N_GROUPS + EXPERTS_PER_GROUP * g_sel
    el = jnp.where((lane >= lo) & (lane < lo + EXPERTS_PER_GROUP), logits, neg)
    t1 = jnp.max(el, axis=-1, keepdims=True)
    i1 = jnp.min(jnp.where(el == t1, lane, big), axis=-1, keepdims=True)
    el2 = jnp.where(lane == i1, neg, el)
    t2 = jnp.max(el2, axis=-1, keepdims=True)
    i2 = jnp.min(jnp.where(el2 == t2, lane, big), axis=-1, keepdims=True)
    dd = jnp.exp(t2 - t1)
    w1 = g_w / (1.0 + dd)
    w2 = g_w * dd / (1.0 + dd)
    e1 = (i1 - N_GROUPS).astype(F32)
    e2 = (i2 - N_GROUPS).astype(F32)
    return jnp.where(lane == 0, e1, jnp.where(lane == 1, e2, jnp.where(lane == 2, w1, jnp.where(lane == 3, w2, 0.0))))


def _pack_halves(x):
    half = x.shape[1] // 2
    hi = lax.bitcast_convert_type(x[:, :half].astype(BF16).astype(F32), jnp.uint32)
    lo = lax.bitcast_convert_type(x[:, half:].astype(BF16).astype(F32), jnp.uint32)
    return hi | (lo >> 16)


def _unpack_halves(w):
    hi = lax.bitcast_convert_type(w & jnp.uint32(0xFFFF0000), F32)
    lo = lax.bitcast_convert_type(w << 16, F32)
    return hi, lo


def _to_token_major(ref, x):
    rows, width = x.shape
    nc = width // LANE
    for c in range(nc):
        ref[pl.ds(c, rows, stride=nc), :] = x[:, c * LANE:(c + 1) * LANE]


def _from_token_major(ref, rows):
    nc = ref.shape[0] // rows
    return [ref[pl.ds(c, rows, stride=nc), :] for c in range(nc)]


def _out_ln_router_kernel(m_ref, w_ref, x_ref, g_ref, b_ref, rw_ref, rb_ref, x1_ref, x1t_ref, r_ref, *, alpha, sub):
    tm, d = x_ref.shape
    nc = d // (2 * LANE)
    rw_hi = rw_ref[:, :LANE]
    for s0 in range(0, tm, sub):
        rows = slice(s0, s0 + sub)
        mix = _dot(m_ref[rows, :], w_ref[...])
        x1 = _layer_norm(alpha * x_ref[rows, :] + mix, g_ref[...], b_ref[...])
        x1_ref[rows, :] = x1
        _to_token_major(x1t_ref.at[pl.ds(s0 * nc, sub * nc), :], _pack_halves(x1))
        xh = x1.astype(BF16)
        xl = (x1 - xh.astype(F32)).astype(BF16)
        both = _dot(xh, rw_ref[...])
        logits = both[:, :LANE] + both[:, LANE:] + _dot(xl, rw_hi) + rb_ref[...]
        r_ref[:, rows] = jnp.transpose(_route(logits))[:SUBLANE]


def _out_ln_router(merged, w_out, x, ln_g, ln_b, rw, rb, alpha, tm, sub):
    n, d = x.shape
    tm = min(tm, n)
    sub = min(sub, tm)
    row = lambda w: pl.BlockSpec((tm, w), lambda i: (i, 0))
    full = lambda a: pl.BlockSpec(a.shape, lambda i: (0, 0))
    w_spec = pl.BlockSpec(w_out.shape, lambda i: (0, 0), pipeline_mode=pl.Buffered(1))
    return pl.pallas_call(
        functools.partial(_out_ln_router_kernel, alpha=alpha, sub=sub),
        grid=(n // tm,),
        in_specs=[row(d), w_spec, row(d), full(ln_g), full(ln_b), full(rw), full(rb)],
        out_specs=[row(d), pl.BlockSpec((tm * (d // (2 * LANE)), LANE), lambda i: (i, 0)),
                   pl.BlockSpec((SUBLANE, tm), lambda i: (0, i))],
        out_shape=[jax.ShapeDtypeStruct((n, d), F32), jax.ShapeDtypeStruct((n * (d // (2 * LANE)), LANE), jnp.uint32),
                   jax.ShapeDtypeStruct((SUBLANE, n), F32)],
        compiler_params=_cparams("parallel"),
        name="out_proj_ln_router",
    )(merged, w_out, x, ln_g, ln_b, rw, rb)


def _moe_kernel(bexp_ref, nact_ref, nval_ref, tok_ref, dst_ref,
                x_hbm, gate_ref, w1_ref, w3_ref, w2_ref, out_hbm,
                xbuf, xbf, ybuf, w1b, w3b, w2b, sem_in, sem_out, *, bm, nc):
    i = pl.program_id(0)
    nact = nact_ref[0]

    @pl.when((i < nact) & ((i == 0) | (bexp_ref[i] != bexp_ref[jnp.maximum(i - 1, 0)])))
    def _():
        w1b[...] = w1_ref[0, 0].astype(BF16)
        w3b[...] = w3_ref[0, 0].astype(BF16)
        w2b[...] = w2_ref[0, 0].astype(BF16)

    def gather(r, blk):
        src = pl.multiple_of(tok_ref[blk * bm + r], nc)
        return pltpu.make_async_copy(x_hbm.at[pl.ds(src, nc), :], xbuf.at[r // SUBLANE, :, r % SUBLANE, :], sem_in)

    def scatter(r, blk):
        dst = pl.multiple_of(dst_ref[blk * bm + r], nc)
        return pltpu.make_async_copy(ybuf.at[r // SUBLANE, :, r % SUBLANE, :], out_hbm.at[pl.ds(dst, nc), :], sem_out)

    def wait_rows(n_tok, sem):
        rows = pl.multiple_of(n_tok * nc, nc)

        @pl.when(n_tok > 0)
        def _():
            pltpu.make_async_copy(x_hbm.at[pl.ds(0, rows), :], out_hbm.at[pl.ds(0, rows), :], sem).wait()

    @pl.when(i == 0)
    def _():
        for r in range(bm):
            gather(r, 0).start(priority=r % 2)

    @pl.when(i < nact)
    def _():
        has_next = i + 1 < nact
        prev = jnp.maximum(i - 1, 0)
        nv_prev = jnp.where(i >= 1, nval_ref[prev], 0)
        nv_head = jnp.minimum(nv_prev, SUBLANE)
        for r in range(bm):
            pl.when(r < nv_prev)(functools.partial(scatter(r, prev).start, priority=r % 2))
        wait_rows(jnp.int32(bm), sem_in)
        half = nc * LANE
        for c in range(nc):
            hi, lo = _unpack_halves(xbuf[:, c].reshape(bm, LANE))
            xbf[:, c * LANE:(c + 1) * LANE] = hi.astype(BF16)
            xbf[:, half + c * LANE:half + (c + 1) * LANE] = lo.astype(BF16)
        for r in range(bm):
            pl.when(has_next)(functools.partial(gather(r, i + 1).start, priority=r % 2))
        wait_rows(nv_head, sem_out)
        xb = xbf[...]
        h1 = _dot(xb, w1b[...])
        h3 = _dot(xb, w3b[...])
        hh = (h1 * jax.nn.sigmoid(h1) * h3).astype(BF16)
        wait_rows(nv_prev - nv_head, sem_out)
        y = _pack_halves(_dot(hh, w2b[...]) * gate_ref[...])
        for c in range(nc):
            ybuf[:, c] = y[:, c * LANE:(c + 1) * LANE].reshape(bm // SUBLANE, SUBLANE, LANE)

    @pl.when(i == nact)
    def _():
        nv = nval_ref[i - 1]
        for r in range(bm):
            pl.when(r < nv)(functools.partial(scatter(r, i - 1).start, priority=r % 2))
        wait_rows(nv, sem_out)


def _moe_experts(x1t, route_t, w1, w3, w2, layer, bm):
    _, ne, d, de = w1.shape
    nc = d // (2 * LANE)
    n = x1t.shape[0] // nc
    a = 2 * n
    p_rows = a + ne * bm
    nb = p_rows // bm
    eid = route_t[:2].reshape(a).astype(jnp.int32)
    gate = route_t[2:4].reshape(a)
    order = jnp.argsort(eid, stable=True).astype(jnp.int32)
    experts = jnp.arange(ne, dtype=jnp.int32)
    counts = jnp.sum((eid[None, :] == experts[:, None]).astype(jnp.int32), axis=1)
    padded = (counts + bm - 1) // bm * bm
    start = jnp.cumsum(counts) - counts
    pend = jnp.cumsum(padded)
    pstart = pend - padded
    blk = jnp.arange(nb, dtype=jnp.int32)
    blk_start = blk * bm
    blk_exp = jnp.minimum(jnp.sum((pend[None, :] <= blk_start[:, None]).astype(jnp.int32), axis=1), ne - 1)
    n_active = (pend[-1] // bm).astype(jnp.int32).reshape(1)
    pos = blk_start[:, None] - pstart[blk_exp][:, None] + jnp.arange(bm, dtype=jnp.int32)[None, :]
    valid = (pos < counts[blk_exp][:, None]) & (blk[:, None] < n_active[0])
    blk_valid = jnp.sum(valid.astype(jnp.int32), axis=1)
    src = jnp.clip(start[blk_exp][:, None] + pos, 0, a - 1)
    row_dst = jnp.where(valid, order[src], 0).reshape(p_rows)
    row_tok = jnp.where(row_dst >= n, row_dst - n, row_dst)
    row_gate = jnp.where(valid.reshape(p_rows), gate[row_dst], 0.0).reshape(p_rows, 1)
    blk_exp = jnp.where(blk < n_active[0], blk_exp, blk_exp[jnp.maximum(n_active[0] - 1, 0)])

    grid_spec = pltpu.PrefetchScalarGridSpec(
        num_scalar_prefetch=5,
        grid=(nb,),
        in_specs=[pl.BlockSpec(memory_space=pl.ANY),
                  pl.BlockSpec((bm, 1), lambda i, be, *_: (i, 0)),
                  pl.BlockSpec((1, 1, d, de), lambda i, be, *_: (layer, be[i], 0, 0)),
                  pl.BlockSpec((1, 1, d, de), lambda i, be, *_: (layer, be[i], 0, 0)),
                  pl.BlockSpec((1, 1, de, d), lambda i, be, *_: (layer, be[i], 0, 0))],
        out_specs=pl.BlockSpec(memory_space=pl.ANY),
        scratch_shapes=[pltpu.VMEM((bm // SUBLANE, nc, SUBLANE, LANE), jnp.uint32), pltpu.VMEM((bm, d), BF16),
                        pltpu.VMEM((bm // SUBLANE, nc, SUBLANE, LANE), jnp.uint32),
                        pltpu.VMEM((d, de), BF16), pltpu.VMEM((d, de), BF16), pltpu.VMEM((de, d), BF16),
                        pltpu.SemaphoreType.DMA(()), pltpu.SemaphoreType.DMA(())],
    )
    return pl.pallas_call(
        functools.partial(_moe_kernel, bm=bm, nc=nc),
        grid_spec=grid_spec,
        out_shape=jax.ShapeDtypeStruct((a * nc, LANE), jnp.uint32),
        compiler_params=_cparams("arbitrary"),
        name="moe_experts",
    )(blk_exp, n_active, blk_valid, row_tok * nc, row_dst * nc, x1t, row_gate, w1, w3, w2)


def _combine_ln_kernel(x_ref, y0_ref, y1_ref, g_ref, b_ref, o_ref, obf_ref, *, alpha):
    rows = x_ref.shape[0]
    his, los = [], []
    for w0, w1 in zip(_from_token_major(y0_ref, rows), _from_token_major(y1_ref, rows)):
        hi0, lo0 = _unpack_halves(w0)
        hi1, lo1 = _unpack_halves(w1)
        his.append(hi0 + hi1)
        los.append(lo0 + lo1)
    y = jnp.concatenate(his + los, axis=-1)
    x2 = _layer_norm(alpha * x_ref[...] + y, g_ref[...], b_ref[...])
    o_ref[...] = x2
    obf_ref[...] = x2.astype(BF16)


def _combine_ln(x1, y2, ln_g, ln_b, alpha, tm):
    n, d = x1.shape
    tm = min(tm, n)
    steps = n // tm
    nc = d // (2 * LANE)
    return pl.pallas_call(
        functools.partial(_combine_ln_kernel, alpha=alpha),
        grid=(steps,),
        in_specs=[pl.BlockSpec((tm, d), lambda i: (i, 0)), pl.BlockSpec((tm * nc, LANE), lambda i: (i, 0)),
                  pl.BlockSpec((tm * nc, LANE), lambda i: (steps + i, 0)),
                  pl.BlockSpec((1, d), lambda i: (0, 0)), pl.BlockSpec((1, d), lambda i: (0, 0))],
        out_specs=[pl.BlockSpec((tm, d), lambda i: (i, 0)), pl.BlockSpec((tm, d), lambda i: (i, 0))],
        out_shape=[jax.ShapeDtypeStruct((n, d), F32), jax.ShapeDtypeStruct((n, d), BF16)],
        compiler_params=_cparams("parallel"),
        name="combine_ln",
    )(x1, y2, y2, ln_g, ln_b)


def _rot_half_cols(w):
    half = w.shape[-1] // 2
    return jnp.concatenate([-w[..., half:], w[..., :half]], axis=-1)


def _rope_cos_sin(positions, dim):
    inv = 1.0 / (ROPE_BASE ** (jnp.arange(0, dim, 2, dtype=F32) / dim))
    ang = positions.astype(F32)[..., None] * inv
    return jnp.cos(ang), jnp.sin(ang)


def kernel(x, positions, w_in, mla_q_norm, mla_w_q_b, mla_kv_norm, mla_w_kv_b, w_branch, w_out, ln1_g, ln1_b, router_group_w, router_group_b, router_expert_w, router_expert_b, expert_w1, expert_w3, expert_w2, ln2_g, ln2_b):
    batch, seq, d = x.shape
    depth = w_in.shape[0]
    n = batch * seq
    alpha = (2 * depth) ** 0.25
    n_main = (CB_MQ + 2 * MLA_LORA // LANE) * LANE

    cr, sr = _rope_cos_sin(positions, HEAD_W)
    cos_r = jnp.concatenate([cr, cr], axis=-1)
    sin_r = jnp.concatenate([-sr, sr], axis=-1)
    cm, sm = _rope_cos_sin(positions, MLA_ROPE)
    q_scale = (MLA_NOPE + MLA_ROPE) ** -0.5
    tq_tab = q_scale * jnp.concatenate([jnp.ones((batch, seq, MLA_NOPE), F32), cm, cm, sm, sm], axis=-1)
    tk_tab = jnp.concatenate([cm, cm, sm, sm], axis=-1)

    xf = x.reshape(n, d)
    xb = xf.astype(BF16)
    for l in range(depth):
        w_kr = w_in[l][:, n_main:n_main + MLA_ROPE]
        sq0, sq1 = CB_SQ * LANE, CB_SK * LANE
        w_main = jnp.concatenate([w_in[l][:, :sq0], w_in[l][:, sq0:sq1] * (HEAD_W ** -0.5), w_in[l][:, sq1:n_main],
                                  w_in[l][:, n_main + MLA_ROPE:]], axis=-1).astype(BF16)
        w_kr2 = jnp.concatenate([w_kr, _rot_half_cols(w_kr)], axis=-1).astype(BF16)
        wq = mla_w_q_b[l].reshape(MLA_LORA, N_HEADS, MLA_NOPE + MLA_ROPE)
        wq_pe = wq[..., MLA_NOPE:]
        wq = jnp.concatenate([wq[..., :MLA_NOPE], wq_pe, _rot_half_cols(wq_pe)], axis=-1)
        wq = wq.reshape(MLA_LORA, N_HEADS * 2 * LANE).astype(BF16)
        wkv = mla_w_kv_b[l].astype(BF16)
        rw = jnp.concatenate([router_group_w[l], router_expert_w[l],
                              jnp.zeros((d, LANE - N_GROUPS - N_EXPERTS), F32)], axis=-1)
        rw_hi = rw.astype(BF16)
        rw_cat = jnp.concatenate([rw_hi, (rw - rw_hi.astype(F32)).astype(BF16)], axis=-1)
        rb =jnp.concatenate([router_group_b[l], router_expert_b[l],
                              jnp.zeros((LANE - N_GROUPS - N_EXPERTS,), F32)]).reshape(1, LANE)

        p = _proj(xb, w_main, batch, seq, tm=1024, tn=1024)
        kr = _proj(xb, w_kr2, batch, seq, tm=1024, tn=LANE)
        ret = _retention(p, cos_r, sin_r, rows_per_step=512)
        sb = _stick_breaking(p, tq=256, hp=2)
        q, kn, v, kpe = _mla_prep(p, kr, tq_tab, tk_tab, mla_q_norm[l].reshape(1, -1), mla_kv_norm[l].reshape(1, -1),
                                  wq, wkv, tm=512)
        ml = _mla_attn(q, kn, kpe, v, tq=256, hp=2)
        merged = _merge(ret, sb, ml, p, w_branch[l].astype(BF16), tm=512, tn=1024)
        x1, x1t, route = _out_ln_router(merged, w_out[l].astype(BF16), xf, ln1_g[l].reshape(1, d),
                                        ln1_b[l].reshape(1, d), rw_cat, rb, alpha, tm=512, sub=256)
        y2 = _moe_experts(x1t, route, expert_w1, expert_w3, expert_w2, l, MOE_ROWS)
        xf, xb = _combine_ln(x1, y2, ln2_g[l].reshape(1, d), ln2_b[l].reshape(1, d), alpha, tm=512)
    return xf.reshape(batch, seq, d)
```

```python
import functools

import numpy as np
import jax
import jax.numpy as jnp
from jax import lax
from jax.experimental import pallas as pl
from jax.experimental.pallas import tpu as pltpu

F32 = jnp.float32
BF16 = jnp.bfloat16

LANE = 128
SUBLANE = 8
N_HEADS = 8
HEAD_W = 128
BRANCH_W = N_HEADS * HEAD_W
RET_CHUNK = 128
MLA_LORA = 512
MLA_NOPE = 128
MLA_ROPE = 64
ROPE_BASE = 10000.0
N_GROUPS = 4
EXPERTS_PER_GROUP = 8
N_EXPERTS = N_GROUPS * EXPERTS_PER_GROUP
NORM_EPS = 1e-5
MOE_ROWS = 256
VMEM_LIMIT = 56 * 1024 * 1024

CB_RQ, CB_RK, CB_RV, CB_RG = 0, 8, 16, 24
CB_SQ, CB_SK, CB_SV = 32, 40, 48
CB_MQ, CB_MKV = 56, 60
CB_GATE = 64
N_CB = 112


def _cparams(*sem):
    return pltpu.CompilerParams(dimension_semantics=sem, vmem_limit_bytes=VMEM_LIMIT)


def _dot(a, b):
    return jnp.dot(a, b, preferred_element_type=F32)


def _dot_nt(a, b):
    return lax.dot_general(a, b, (((1,), (1,)), ((), ())), preferred_element_type=F32)


def _layer_norm(y, g, b):
    mu = jnp.mean(y, axis=-1, keepdims=True)
    d = y - mu
    var = jnp.mean(d * d, axis=-1, keepdims=True)
    return d * lax.rsqrt(var + NORM_EPS) * g + b


def _proj_kernel(x_ref, w_ref, o_ref):
    acc = _dot(x_ref[...], w_ref[...])
    for h in range(o_ref.shape[1]):
        o_ref[0, h] = acc[:, h * LANE:(h + 1) * LANE].astype(o_ref.dtype)


def _proj(x_bf, w, batch, seq, tm, tn):
    n, k = x_bf.shape
    c = w.shape[1]
    tm = min(tm, seq)
    spb = seq // tm
    return pl.pallas_call(
        _proj_kernel,
        grid=(n // tm, c // tn),
        in_specs=[pl.BlockSpec((tm, k), lambda i, j: (i, 0)),
                  pl.BlockSpec((k, tn), lambda i, j: (0, j))],
        out_specs=pl.BlockSpec((1, tn // LANE, tm, LANE), lambda i, j: (i // spb, j, i % spb, 0)),
        out_shape=jax.ShapeDtypeStruct((batch, c // LANE, seq, LANE), BF16),
        compiler_params=_cparams("parallel", "parallel"),
        name="in_proj",
    )(x_bf, w)


def _retention_tables():
    h = np.arange(N_HEADS, dtype=np.float64)
    log_gamma = np.log1p(-np.exp2(-5.0 - h))
    pos = np.arange(RET_CHUNK, dtype=np.float64)
    rel = pos[:, None] - pos[None, :]
    intra = np.where(rel >= 0, np.exp(log_gamma[:, None, None] * np.maximum(rel, 0.0)), 0.0)
    ones = np.ones((1, 1, RET_CHUNK))
    qdec = np.exp(log_gamma[:, None] * (pos + 1.0))[:, :, None] * ones
    kdec = np.exp(log_gamma[:, None] * (RET_CHUNK - 1 - pos))[:, :, None] * ones
    cdec = [float(np.float32(np.exp(lg * RET_CHUNK))) for lg in log_gamma]
    return (jnp.asarray(intra, F32), jnp.asarray(qdec, F32), jnp.asarray(kdec, F32), cdec)


def _retention_kernel(q_ref, k_ref, v_ref, g_ref, cos_ref, sin_ref, intra_ref, qdec_ref, kdec_ref,
                      o_ref, state_ref, *, n_chunk, cdec):
    @pl.when(pl.program_id(1) == 0)
    def _():
        state_ref[...] = jnp.zeros_like(state_ref)

    k_scale = HEAD_W ** -0.5
    for c in range(n_chunk):
        rows = slice(c * RET_CHUNK, (c + 1) * RET_CHUNK)
        cos = cos_ref[0, rows, :]
        sin = sin_ref[0, rows, :]
        for h in range(N_HEADS):
            q = q_ref[0, h, rows, :].astype(F32)
            k = k_ref[0, h, rows, :].astype(F32)
            q = q * cos + pltpu.roll(q, HEAD_W // 2, 1) * sin
            k = (k * cos + pltpu.roll(k, HEAD_W // 2, 1) * sin) * k_scale
            v = v_ref[0, h, rows, :]
            st = state_ref[h]
            scores = _dot_nt(q.astype(BF16), k.astype(BF16)) * intra_ref[h]
            o = _dot(scores.astype(BF16), v) + _dot((q * qdec_ref[h]).astype(BF16), st.astype(BF16))
            k_dec_t = jnp.transpose(k * kdec_ref[h]).astype(BF16)
            state_ref[h] = st * cdec[h] + _dot(k_dec_t, v)
            o = o * lax.rsqrt(jnp.mean(o * o, axis=-1, keepdims=True) + NORM_EPS)
            g = g_ref[0, h, rows, :].astype(F32)
            o_ref[0, h, rows, :] = (o * (g * jax.nn.sigmoid(g))).astype(o_ref.dtype)


def _retention(p, cos_r, sin_r, rows_per_step):
    batch, _, seq, _ = p.shape
    r = min(rows_per_step, seq)
    intra, qdec, kdec, cdec = _retention_tables()

    def pspec(cb):
        return pl.BlockSpec((1, N_HEADS, r, LANE), lambda b, i: (b, cb // N_HEADS, i, 0))

    tspec = pl.BlockSpec((1, r, LANE), lambda b, i: (b, i, 0))
    cspec = pl.BlockSpec((N_HEADS, RET_CHUNK, RET_CHUNK), lambda b, i: (0, 0, 0))
    return pl.pallas_call(
        functools.partial(_retention_kernel, n_chunk=r // RET_CHUNK, cdec=cdec),
        grid=(batch, seq // r),
        in_specs=[pspec(CB_RQ), pspec(CB_RK), pspec(CB_RV), pspec(CB_RG), tspec, tspec, cspec, cspec, cspec],
        out_specs=pl.BlockSpec((1, N_HEADS, r, LANE), lambda b, i: (b, 0, i, 0)),
        out_shape=jax.ShapeDtypeStruct((batch, N_HEADS, seq, LANE), BF16),
        scratch_shapes=[pltpu.VMEM((N_HEADS, HEAD_W, HEAD_W), F32)],
        compiler_params=_cparams("parallel", "arbitrary"),
        name="retention",
    )(p, p, p, p, cos_r, sin_r, intra, qdec, kdec)


def _sb_kernel(q_ref, k_ref, v_ref, u_ref, o_ref, *, tq, nq):
    i = pl.program_id(2)
    u = u_ref[...]
    row = lax.broadcasted_iota(jnp.int32, (tq, tq), 0)
    col = lax.broadcasted_iota(jnp.int32, (tq, tq), 1)
    strict = col < row

    def attend(nblk):
        for hh in range(q_ref.shape[1]):
            attend_head(nblk, hh)

    def attend_head(nblk, hh):
        z = _dot_nt(q_ref[0, hh], k_ref[0, hh, :nblk * tq, :])
        log_beta = jnp.minimum(z, 0.0) - jnp.log(1.0 + jnp.exp(-jnp.abs(z)))
        log_keep = log_beta - z
        c = jnp.zeros((tq, 1), F32)
        ws = [None] * nblk
        for b in reversed(range(nblk)):
            diag = b == nblk - 1
            lk = log_keep[:, b * tq:(b + 1) * tq]
            if diag:
                lk = jnp.where(strict, lk, 0.0)
            after = _dot(lk.astype(BF16), u)
            w = jnp.exp(log_beta[:, b * tq:(b + 1) * tq] + after + c)
            if diag:
                w = jnp.where(strict, w, 0.0)
            ws[b] = w.astype(BF16)
            c = c + jnp.sum(lk, axis=-1, keepdims=True)
        w_all = ws[0] if nblk == 1 else jnp.concatenate(ws, axis=-1)
        o_ref[0, hh] = _dot(w_all, v_ref[0, hh, :nblk * tq, :]).astype(o_ref.dtype)

    for nblk in range(1, nq + 1):
        pl.when(i == nblk - 1)(functools.partial(attend, nblk))


def _stick_breaking(p, tq, hp):
    batch, _, seq, _ = p.shape
    tq = min(tq, seq)
    idx = np.arange(tq)
    u = jnp.asarray(idx[:, None] > idx[None, :], BF16)
    return pl.pallas_call(
        functools.partial(_sb_kernel, tq=tq, nq=seq // tq),
        grid=(batch, N_HEADS // hp, seq // tq),
        in_specs=[pl.BlockSpec((1, hp, tq, LANE), lambda b, h, i: (b, CB_SQ // hp + h, i, 0)),
                  pl.BlockSpec((1, hp, seq, LANE), lambda b, h, i: (b, CB_SK // hp + h, 0, 0)),
                  pl.BlockSpec((1, hp, seq, LANE), lambda b, h, i: (b, CB_SV // hp + h, 0, 0)),
                  pl.BlockSpec((tq, tq), lambda b, h, i: (0, 0))],
        out_specs=pl.BlockSpec((1, hp, tq, LANE), lambda b, h, i: (b, h, i, 0)),
        out_shape=jax.ShapeDtypeStruct((batch, N_HEADS, seq, LANE), BF16),
        compiler_params=_cparams("parallel", "parallel", "arbitrary"),
        name="stick_breaking",
    )(p, p, p, u)


def _mla_prep_kernel(mq_ref, mkv_ref, kr_ref, tq_ref, tk_ref, qn_ref, kvn_ref, wq_ref, wkv_ref,
                     q_out, kn_out, v_out, kpe_out):
    def rms(ref, g_ref):
        x = jnp.concatenate([ref[0, c].astype(F32) for c in range(MLA_LORA // LANE)], axis=-1)
        y = x * lax.rsqrt(jnp.mean(x * x, axis=-1, keepdims=True) + NORM_EPS)
        return (y * g_ref[...]).astype(BF16)

    qf = _dot(rms(mq_ref, qn_ref), wq_ref[...])
    tq = tq_ref[0]
    for h in range(N_HEADS):
        t = qf[:, h * 2 * LANE:(h + 1) * 2 * LANE] * tq
        u = t[:, LANE:]
        pe = u + pltpu.roll(u, LANE // 2, 1)
        q_out[0, h] = jnp.concatenate([t[:, :LANE], pe], axis=-1).astype(q_out.dtype)
    kv = _dot(rms(mkv_ref, kvn_ref), wkv_ref[...])
    for h in range(N_HEADS):
        kn_out[0, h] = kv[:, h * 2 * LANE:h * 2 * LANE + LANE].astype(kn_out.dtype)
        v_out[0, h] = kv[:, h * 2 * LANE + LANE:(h + 1) * 2 * LANE].astype(v_out.dtype)
    t = kr_ref[0, 0].astype(F32) * tk_ref[0]
    kp = t + pltpu.roll(t, LANE // 2, 1)
    lane = lax.broadcasted_iota(jnp.int32, kp.shape, 1)
    kpe_out[0] = jnp.where(lane < MLA_ROPE, kp, 0.0).astype(kpe_out.dtype)


def _mla_prep(p, kr, tq_tab, tk_tab, q_norm, kv_norm, wq, wkv, tm):
    batch, _, seq, _ = p.shape
    tm = min(tm, seq)
    nl = MLA_LORA // LANE
    head_out = lambda w: pl.BlockSpec((1, N_HEADS, tm, w), lambda b, i: (b, 0, i, 0))
    return pl.pallas_call(
        _mla_prep_kernel,
        grid=(batch, seq // tm),
        in_specs=[pl.BlockSpec((1, nl, tm, LANE), lambda b, i: (b, CB_MQ // nl, i, 0)),
                  pl.BlockSpec((1, nl, tm, LANE), lambda b, i: (b, CB_MKV // nl, i, 0)),
                  pl.BlockSpec((1, 1, tm, LANE), lambda b, i: (b, 0, i, 0)),
                  pl.BlockSpec((1, tm, 2 * LANE), lambda b, i: (b, i, 0)),
                  pl.BlockSpec((1, tm, LANE), lambda b, i: (b, i, 0)),
                  pl.BlockSpec((1, MLA_LORA), lambda b, i: (0, 0)),
                  pl.BlockSpec((1, MLA_LORA), lambda b, i: (0, 0)),
                  pl.BlockSpec(wq.shape, lambda b, i: (0, 0)),
                  pl.BlockSpec(wkv.shape, lambda b, i: (0, 0))],
        out_specs=[head_out(2 * LANE), head_out(LANE), head_out(LANE),
                   pl.BlockSpec((1, tm, LANE), lambda b, i: (b, i, 0))],
        out_shape=[jax.ShapeDtypeStruct((batch, N_HEADS, seq, 2 * LANE), BF16),
                   jax.ShapeDtypeStruct((batch, N_HEADS, seq, LANE), BF16),
                   jax.ShapeDtypeStruct((batch, N_HEADS, seq, LANE), BF16),
                   jax.ShapeDtypeStruct((batch, seq, LANE), BF16)],
        compiler_params=_cparams("parallel", "parallel"),
        name="mla_prep",
    )(p, p, kr, tq_tab, tk_tab, q_norm, kv_norm, wq, wkv)


def _mla_attn_kernel(q_ref, kn_ref, kpe_ref, v_ref, o_ref, *, tq, nq):
    i = pl.program_id(2)
    row = lax.broadcasted_iota(jnp.int32, (tq, tq), 0)
    col = lax.broadcasted_iota(jnp.int32, (tq, tq), 1)
    causal = col <= row

    def attend(nblk):
        for hh in range(q_ref.shape[1]):
            attend_head(nblk, hh)

    def attend_head(nblk, hh):
        n_keys = nblk * tq
        k = jnp.concatenate([kn_ref[0, hh, :n_keys, :], kpe_ref[0, :n_keys, :]], axis=-1)
        s = _dot_nt(q_ref[0, hh], k)
        last = jnp.where(causal, s[:, n_keys - tq:], -1e30)
        s = last if nblk == 1 else jnp.concatenate([s[:, :n_keys - tq], last], axis=-1)
        pr = jnp.exp(s - jnp.max(s, axis=-1, keepdims=True))
        l = jnp.sum(pr, axis=-1, keepdims=True)
        o_ref[0, hh] = (_dot(pr.astype(BF16), v_ref[0, hh, :n_keys, :]) / l).astype(o_ref.dtype)

    for nblk in range(1, nq + 1):
        pl.when(i == nblk - 1)(functools.partial(attend, nblk))


def _mla_attn(q, kn, kpe, v, tq, hp):
    batch, _, seq, _ = q.shape
    tq = min(tq, seq)
    return pl.pallas_call(
        functools.partial(_mla_attn_kernel, tq=tq, nq=seq // tq),
        grid=(batch, N_HEADS // hp, seq // tq),
        in_specs=[pl.BlockSpec((1, hp, tq, 2 * LANE), lambda b, h, i: (b, h, i, 0)),
                  pl.BlockSpec((1, hp, seq, LANE), lambda b, h, i: (b, h, 0, 0)),
                  pl.BlockSpec((1, seq, LANE), lambda b, h, i: (b, 0, 0)),
                  pl.BlockSpec((1, hp, seq, LANE), lambda b, h, i: (b, h, 0, 0))],
        out_specs=pl.BlockSpec((1, hp, tq, LANE), lambda b, h, i: (b, h, i, 0)),
        out_shape=jax.ShapeDtypeStruct((batch, N_HEADS, seq, LANE), BF16),
        compiler_params=_cparams("parallel", "parallel", "arbitrary"),
        name="mla_attn",
    )(q, kn, kpe, v)


def _merge_kernel(r_ref, s_ref, m_ref, g0_ref, g1_ref, g2_ref, w_ref, o_ref):
    acc = None
    for n, (b_ref, g_ref) in enumerate(((r_ref, g0_ref), (s_ref, g1_ref), (m_ref, g2_ref))):
        a = jnp.concatenate([b_ref[0, h] for h in range(N_HEADS)], axis=-1)
        y = _dot(a, w_ref[n])
        g = jax.nn.sigmoid(jnp.concatenate([g_ref[0, c].astype(F32) for c in range(g_ref.shape[1])], axis=-1))
        acc = g * y if acc is None else acc + g * y
    o_ref[...] = acc.astype(o_ref.dtype)


def _merge(ret, sb, ml, p, w_branch, tm, tn):
    batch, _, seq, _ = p.shape
    d = w_branch.shape[-1]
    tm = min(tm, seq)
    spb = seq // tm
    gcb = tn // LANE
    bspec = pl.BlockSpec((1, N_HEADS, tm, LANE), lambda j, i: (i // spb, 0, i % spb, 0))

    def gspec(n):
        first = (CB_GATE + n * (d // LANE)) // gcb
        return pl.BlockSpec((1, gcb, tm, LANE), lambda j, i: (i // spb, first + j, i % spb, 0))

    return pl.pallas_call(
        _merge_kernel,
        grid=(d // tn, batch * spb),
        in_specs=[bspec, bspec, bspec, gspec(0), gspec(1), gspec(2),
                  pl.BlockSpec((3, BRANCH_W, tn), lambda j, i: (0, 0, j))],
        out_specs=pl.BlockSpec((tm, tn), lambda j, i: (i, j)),
        out_shape=jax.ShapeDtypeStruct((batch * seq, d), BF16),
        compiler_params=_cparams("parallel", "parallel"),
        name="branch_merge",
    )(ret, sb, ml, p, p, p, w_branch)


def _route(logits):
    lane = lax.broadcasted_iota(jnp.int32, logits.shape, 1)
    neg = -jnp.inf
    big = jnp.int32(1 << 20)
    is_g = lane < N_GROUPS
    gl = jnp.where(is_g, logits, neg)
    gm = jnp.max(gl, axis=-1, keepdims=True)
    g_sel = jnp.min(jnp.where(gl == gm, lane, big), axis=-1, keepdims=True)
    g_w = 1.0 / jnp.sum(jnp.where(is_g, jnp.exp(gl - gm), 0.0), axis=-1, keepdims=True)
    lo = N_GROUPS + EXPERTS_PER_GROUP * g_sel
    el = jnp.where((lane >= lo) & (lane < lo + EXPERTS_PER_GROUP), logits, neg)
    t1 = jnp.max(el, axis=-1, keepdims=True)
    i1 = jnp.min(jnp.where(el == t1, lane, big), axis=-1, keepdims=True)
    el2 = jnp.where(lane == i1, neg, el)
    t2 = jnp.max(el2, axis=-1, keepdims=True)
    i2 = jnp.min(jnp.where(el2 == t2, lane, big), axis=-1, keepdims=True)
    dd = jnp.exp(t2 - t1)
    w1 = g_w / (1.0 + dd)
    w2 = g_w * dd / (1.0 + dd)
    e1 = (i1 - N_GROUPS).astype(F32)
    e2 = (i2 - N_GROUPS).astype(F32)
    return jnp.where(lane == 0, e1, jnp.where(lane == 1, e2, jnp.where(lane == 2, w1, jnp.where(lane == 3, w2, 0.0))))


def _pack_halves(x):
    half = x.shape[1] // 2
    hi = lax.bitcast_convert_type(x[:, :half].astype(BF16).astype(F32), jnp.uint32)
    lo = lax.bitcast_convert_type(x[:, half:].astype(BF16).astype(F32), jnp.uint32)
    return hi | (lo >> 16)


def _unpack_halves(w):
    hi = lax.bitcast_convert_type(w & jnp.uint32(0xFFFF0000), F32)
    lo = lax.bitcast_convert_type(w << 16, F32)
    return hi, lo


def _to_token_major(ref, x):
    rows, width = x.shape
    nc = width // LANE
    for c in range(nc):
        ref[pl.ds(c, rows, stride=nc), :] = x[:, c * LANE:(c + 1) * LANE]


def _from_token_major(ref, rows):
    nc = ref.shape[0] // rows
    return [ref[pl.ds(c, rows, stride=nc), :] for c in range(nc)]


def _out_ln_router_kernel(m_ref, w_ref, x_ref, g_ref, b_ref, rw_ref, rb_ref, x1_ref, x1t_ref, r_ref, eid_ref, *,
                          alpha, sub):
    tm, d = x_ref.shape
    nc = d // (2 * LANE)
    rw_hi = rw_ref[:, :LANE]
    for s0 in range(0, tm, sub):
        rows = slice(s0, s0 + sub)
        mix = _dot(m_ref[rows, :], w_ref[...])
        x1 = _layer_norm(alpha * x_ref[rows, :] + mix, g_ref[...], b_ref[...])
        x1_ref[rows, :] = x1
        _to_token_major(x1t_ref.at[pl.ds(s0 * nc, sub * nc), :], _pack_halves(x1))
        xh = x1.astype(BF16)
        xl = (x1 - xh.astype(F32)).astype(BF16)
        both = _dot(xh, rw_ref[...])
        logits = both[:, :LANE] + both[:, LANE:] + _dot(xl, rw_hi) + rb_ref[...]
        route = _route(logits)
        r_ref[rows, :] = route
        eid_ref[:, rows] = jnp.transpose(route)[:SUBLANE].astype(jnp.int32)


def _out_ln_router(merged, w_out, x, ln_g, ln_b, rw, rb, alpha, tm, sub):
    n, d = x.shape
    tm = min(tm, n)
    sub = min(sub, tm)
    row = lambda w: pl.BlockSpec((tm, w), lambda i: (i, 0))
    full = lambda a: pl.BlockSpec(a.shape, lambda i: (0, 0))
    w_spec = pl.BlockSpec(w_out.shape, lambda i: (0, 0), pipeline_mode=pl.Buffered(1))
    return pl.pallas_call(
        functools.partial(_out_ln_router_kernel, alpha=alpha, sub=sub),
        grid=(n // tm,),
        in_specs=[row(d), w_spec, row(d), full(ln_g), full(ln_b), full(rw), full(rb)],
        out_specs=[row(d), pl.BlockSpec((tm * (d // (2 * LANE)), LANE), lambda i: (i, 0)), row(LANE),
                   pl.BlockSpec((SUBLANE, tm), lambda i: (0, i))],
        out_shape=[jax.ShapeDtypeStruct((n, d), F32), jax.ShapeDtypeStruct((n * (d // (2 * LANE)), LANE), jnp.uint32),
                   jax.ShapeDtypeStruct((n, LANE), F32), jax.ShapeDtypeStruct((SUBLANE, n), jnp.int32)],
        compiler_params=_cparams("parallel"),
        name="out_proj_ln_router",
    )(merged, w_out, x, ln_g, ln_b, rw, rb)


def _moe_kernel(bexp_ref, nact_ref, nval_ref, sbase_ref, tok_ref, dst_ref,
                x_hbm, w1_ref, w3_ref, w2_ref, out_hbm,
                xbuf, xbf, ybuf, w1b, w3b, w2b, sem_in, sem_out, *, bm, nc):
    i = pl.program_id(0)
    nact = nact_ref[0]

    @pl.when((i < nact) & ((i == 0) | (bexp_ref[i] != bexp_ref[jnp.maximum(i - 1, 0)])))
    def _():
        w1b[...] = w1_ref[0, 0].astype(BF16)
        w3b[...] = w3_ref[0, 0].astype(BF16)
        w2b[...] = w2_ref[0, 0].astype(BF16)

    def gather(r, base):
        src = pl.multiple_of(tok_ref[base + r], nc)
        return pltpu.make_async_copy(x_hbm.at[pl.ds(src, nc), :], xbuf.at[r // SUBLANE, :, r % SUBLANE, :], sem_in)

    def scatter(r, base):
        dst = pl.multiple_of(dst_ref[base + r], nc)
        return pltpu.make_async_copy(ybuf.at[r // SUBLANE, :, r % SUBLANE, :], out_hbm.at[pl.ds(dst, nc), :], sem_out)

    def wait_rows(n_tok, sem):
        rows = pl.multiple_of(n_tok * nc, nc)

        @pl.when(n_tok > 0)
        def _():
            pltpu.make_async_copy(x_hbm.at[pl.ds(0, rows), :], out_hbm.at[pl.ds(0, rows), :], sem).wait()

    @pl.when(i == 0)
    def _():
        for r in range(bm):
            gather(r, sbase_ref[0]).start(priority=r % 2)

    @pl.when(i < nact)
    def _():
        has_next = i + 1 < nact
        prev = jnp.maximum(i - 1, 0)
        nv_prev = jnp.where(i >= 1, nval_ref[prev], 0)
        nv_head = jnp.minimum(nv_prev, SUBLANE)
        base_prev = sbase_ref[prev]
        base_next = sbase_ref[i + 1]
        for r in range(bm):
            pl.when(r < nv_prev)(functools.partial(scatter(r, base_prev).start, priority=r % 2))
        wait_rows(jnp.int32(bm), sem_in)
        half = nc * LANE
        for c in range(nc):
            hi, lo = _unpack_halves(xbuf[:, c].reshape(bm, LANE))
            xbf[:, c * LANE:(c + 1) * LANE] = hi.astype(BF16)
            xbf[:, half + c * LANE:half + (c + 1) * LANE] = lo.astype(BF16)
        for r in range(bm):
            pl.when(has_next)(functools.partial(gather(r, base_next).start, priority=r % 2))
        wait_rows(nv_head, sem_out)
        xb = xbf[...]
        h1 = _dot(xb, w1b[...])
        h3 = _dot(xb, w3b[...])
        hh = (h1 * jax.nn.sigmoid(h1) * h3).astype(BF16)
        wait_rows(nv_prev - nv_head, sem_out)
        y = _pack_halves(_dot(hh, w2b[...]))
        for c in range(nc):
            ybuf[:, c] = y[:, c * LANE:(c + 1) * LANE].reshape(bm // SUBLANE, SUBLANE, LANE)

    @pl.when(i == nact)
    def _():
        nv = nval_ref[i - 1]
        base = sbase_ref[i - 1]
        for r in range(bm):
            pl.when(r < nv)(functools.partial(scatter(r, base).start, priority=r % 2))
        wait_rows(nv, sem_out)


def _moe_experts(x1t, eid_t, w1, w3, w2, layer, bm):
    _, ne, d, de = w1.shape
    nc = d // (2 * LANE)
    n = x1t.shape[0] // nc
    a = 2 * n
    nb = a // bm + ne
    eid = eid_t[:2].reshape(a)
    order = jnp.argsort(eid, stable=True).astype(jnp.int32)
    experts = jnp.arange(ne, dtype=jnp.int32)
    counts = jnp.sum((eid[None, :] == experts[:, None]).astype(jnp.int32), axis=1)
    padded = (counts + bm - 1) // bm * bm
    upto = experts[None, :] <= experts[:, None]
    start = jnp.sum(jnp.where(upto, counts[None, :], 0), axis=1) - counts
    pend = jnp.sum(jnp.where(upto, padded[None, :], 0), axis=1)
    pstart = pend - padded
    blk = jnp.arange(nb, dtype=jnp.int32)
    blk_start = blk * bm
    n_active = pend[ne - 1] // bm
    active = blk < n_active
    blk_exp = jnp.minimum(jnp.sum((pend[None, :] <= blk_start[:, None]).astype(jnp.int32), axis=1), ne - 1)
    mine = blk_exp[:, None] == experts[None, :]
    pick = lambda v: jnp.sum(jnp.where(mine, v[None, :], 0), axis=1)
    offset = blk_start - pick(pstart)
    blk_valid = jnp.where(active, jnp.clip(pick(counts) - offset, 0, bm), 0).astype(jnp.int32)
    blk_base = jnp.where(active, pick(start) + offset, 0).astype(jnp.int32)
    blk_exp = jnp.where(active, blk_exp, jnp.max(jnp.where(active, blk_exp, 0))).astype(jnp.int32)
    tail = jnp.zeros((bm,), jnp.int32)
    row_tok = jnp.concatenate([jnp.where(order >= n, order - n, order) * nc, tail])
    row_dst = jnp.concatenate([order * nc, tail])

    grid_spec = pltpu.PrefetchScalarGridSpec(
        num_scalar_prefetch=6,
        grid=(nb,),
        in_specs=[pl.BlockSpec(memory_space=pl.ANY),
                  pl.BlockSpec((1, 1, d, de), lambda i, be, *_: (layer, be[i], 0, 0)),
                  pl.BlockSpec((1, 1, d, de), lambda i, be, *_: (layer, be[i], 0, 0)),
                  pl.BlockSpec((1, 1, de, d), lambda i, be, *_: (layer, be[i], 0, 0))],
        out_specs=pl.BlockSpec(memory_space=pl.ANY),
        scratch_shapes=[pltpu.VMEM((bm // SUBLANE, nc, SUBLANE, LANE), jnp.uint32), pltpu.VMEM((bm, d), BF16),
                        pltpu.VMEM((bm // SUBLANE, nc, SUBLANE, LANE), jnp.uint32),
                        pltpu.VMEM((d, de), BF16), pltpu.VMEM((d, de), BF16), pltpu.VMEM((de, d), BF16),
                        pltpu.SemaphoreType.DMA(()), pltpu.SemaphoreType.DMA(())],
    )
    return pl.pallas_call(
        functools.partial(_moe_kernel, bm=bm, nc=nc),
        grid_spec=grid_spec,
        out_shape=jax.ShapeDtypeStruct((a * nc, LANE), jnp.uint32),
        compiler_params=_cparams("arbitrary"),
        name="moe_experts",
    )(blk_exp, n_active.reshape(1).astype(jnp.int32), blk_valid, blk_base, row_tok, row_dst, x1t, w1, w3, w2)


def _combine_ln_kernel(x_ref, y0_ref, y1_ref, r_ref, g_ref, b_ref, o_ref, obf_ref, *, alpha):
    rows = x_ref.shape[0]
    gate0 = r_ref[:, 2:3]
    gate1 = r_ref[:, 3:4]
    his, los = [], []
    for w0, w1 in zip(_from_token_major(y0_ref, rows), _from_token_major(y1_ref, rows)):
        hi0, lo0 = _unpack_halves(w0)
        hi1, lo1 = _unpack_halves(w1)
        his.append(gate0 * hi0 + gate1 * hi1)
        los.append(gate0 * lo0 + gate1 * lo1)
    y = jnp.concatenate(his + los, axis=-1)
    x2 = _layer_norm(alpha * x_ref[...] + y, g_ref[...], b_ref[...])
    o_ref[...] = x2
    obf_ref[...] = x2.astype(BF16)


def _combine_ln(x1, y2, route, ln_g, ln_b, alpha, tm):
    n, d = x1.shape
    tm = min(tm, n)
    steps = n // tm
    nc = d // (2 * LANE)
    return pl.pallas_call(
        functools.partial(_combine_ln_kernel, alpha=alpha),
        grid=(steps,),
        in_specs=[pl.BlockSpec((tm, d), lambda i: (i, 0)), pl.BlockSpec((tm * nc, LANE), lambda i: (i, 0)),
                  pl.BlockSpec((tm * nc, LANE), lambda i: (steps + i, 0)),
                  pl.BlockSpec((tm, LANE), lambda i: (i, 0)),
                  pl.BlockSpec((1, d), lambda i: (0, 0)), pl.BlockSpec((1, d), lambda i: (0, 0))],
        out_specs=[pl.BlockSpec((tm, d), lambda i: (i, 0)), pl.BlockSpec((tm, d), lambda i: (i, 0))],
        out_shape=[jax.ShapeDtypeStruct((n, d), F32), jax.ShapeDtypeStruct((n, d), BF16)],
        compiler_params=_cparams("parallel"),
        name="combine_ln",
    )(x1, y2, y2, route, ln_g, ln_b)


def _rot_half_cols(w):
    half = w.shape[-1] // 2
    return jnp.concatenate([-w[..., half:], w[..., :half]], axis=-1)


def _rope_cos_sin(positions, dim):
    inv = 1.0 / (ROPE_BASE ** (jnp.arange(0, dim, 2, dtype=F32) / dim))
    ang = positions.astype(F32)[..., None] * inv
    return jnp.cos(ang), jnp.sin(ang)


def kernel(x, positions, w_in, mla_q_norm, mla_w_q_b, mla_kv_norm, mla_w_kv_b, w_branch, w_out, ln1_g, ln1_b, router_group_w, router_group_b, router_expert_w, router_expert_b, expert_w1, expert_w3, expert_w2, ln2_g, ln2_b):
    batch, seq, d = x.shape
    depth = w_in.shape[0]
    n = batch * seq
    alpha = (2 * depth) ** 0.25
    n_main = (CB_MQ + 2 * MLA_LORA // LANE) * LANE

    cr, sr = _rope_cos_sin(positions, HEAD_W)
    cos_r = jnp.concatenate([cr, cr], axis=-1)
    sin_r = jnp.concatenate([-sr, sr], axis=-1)
    cm, sm = _rope_cos_sin(positions, MLA_ROPE)
    q_scale = (MLA_NOPE + MLA_ROPE) ** -0.5
    tq_tab = q_scale * jnp.concatenate([jnp.ones((batch, seq, MLA_NOPE), F32), cm, cm, sm, sm], axis=-1)
    tk_tab = jnp.concatenate([cm, cm, sm, sm], axis=-1)

    xf = x.reshape(n, d)
    xb = xf.astype(BF16)
    for l in range(depth):
        w_kr = w_in[l][:, n_main:n_main + MLA_ROPE]
        sq0, sq1 = CB_SQ * LANE, CB_SK * LANE
        w_main = jnp.concatenate([w_in[l][:, :sq0], w_in[l][:, sq0:sq1] * (HEAD_W ** -0.5), w_in[l][:, sq1:n_main],
                                  w_in[l][:, n_main + MLA_ROPE:]], axis=-1).astype(BF16)
        w_kr2 = jnp.concatenate([w_kr, _rot_half_cols(w_kr)], axis=-1).astype(BF16)
        wq = mla_w_q_b[l].reshape(MLA_LORA, N_HEADS, MLA_NOPE + MLA_ROPE)
        wq_pe = wq[..., MLA_NOPE:]
        wq = jnp.concatenate([wq[..., :MLA_NOPE], wq_pe, _rot_half_cols(wq_pe)], axis=-1)
        wq = wq.reshape(MLA_LORA, N_HEADS * 2 * LANE).astype(BF16)
        wkv = mla_w_kv_b[l].astype(BF16)
        rw = jnp.concatenate([router_group_w[l], router_expert_w[l],
                              jnp.zeros((d, LANE - N_GROUPS - N_EXPERTS), F32)], axis=-1)
        rw_hi = rw.astype(BF16)
        rw_cat = jnp.concatenate([rw_hi, (rw - rw_hi.astype(F32)).astype(BF16)], axis=-1)
        rb =jnp.concatenate([router_group_b[l], router_expert_b[l],
                              jnp.zeros((LANE - N_GROUPS - N_EXPERTS,), F32)]).reshape(1, LANE)

        p = _proj(xb, w_main, batch, seq, tm=1024, tn=1024)
        kr = _proj(xb, w_kr2, batch, seq, tm=1024, tn=LANE)
        ret = _retention(p, cos_r, sin_r, rows_per_step=512)
        sb = _stick_breaking(p, tq=256, hp=2)
        q, kn, v, kpe = _mla_prep(p, kr, tq_tab, tk_tab, mla_q_norm[l].reshape(1, -1), mla_kv_norm[l].reshape(1, -1),
                                  wq, wkv, tm=512)
        ml = _mla_attn(q, kn, kpe, v, tq=256, hp=4)
        merged = _merge(ret, sb, ml, p, w_branch[l].astype(BF16), tm=512, tn=1024)
        x1, x1t, route, eid_t = _out_ln_router(merged, w_out[l].astype(BF16), xf, ln1_g[l].reshape(1, d),
                                               ln1_b[l].reshape(1, d), rw_cat, rb, alpha, tm=512, sub=256)
        y2 = _moe_experts(x1t, eid_t, expert_w1, expert_w3, expert_w2, l, MOE_ROWS)
        xf, xb = _combine_ln(x1, y2, route, ln2_g[l].reshape(1, d), ln2_b[l].reshape(1, d), alpha, tm=512)
    return xf.reshape(batch, seq, d)
```

```python
import functools

import numpy as np
import jax
import jax.numpy as jnp
from jax import lax
from jax.experimental import pallas as pl
from jax.experimental.pallas import tpu as pltpu

F32 = jnp.float32
BF16 = jnp.bfloat16

LANE = 128
SUBLANE = 8
N_HEADS = 8
HEAD_W = 128
BRANCH_W = N_HEADS * HEAD_W
N_BRANCH = 3
RET_CHUNK = 128
MLA_LORA = 512
MLA_NOPE = 128
MLA_ROPE = 64
ROPE_BASE = 10000.0
N_GROUPS = 4
EXPERTS_PER_GROUP = 8
N_EXPERTS = N_GROUPS * EXPERTS_PER_GROUP
NORM_EPS = 1e-5
LOG2E = 1.4426950408889634
MOE_ROWS = 256
VMEM_LIMIT = 56 * 1024 * 1024

CB_RQ, CB_RK, CB_RV, CB_RG = 0, 8, 16, 24
CB_SQ, CB_SK, CB_SV = 32, 40, 48
CB_MQ, CB_MKV = 56, 60
CB_GATE = 64
N_CB = 112


def _cparams(*sem):
    return pltpu.CompilerParams(dimension_semantics=sem, vmem_limit_bytes=VMEM_LIMIT)


def _dot(a, b):
    return jnp.dot(a, b, preferred_element_type=F32)


def _dot_nt(a, b):
    return lax.dot_general(a, b, (((1,), (1,)), ((), ())), preferred_element_type=F32)


def _layer_norm(y, g, b):
    mu = jnp.mean(y, axis=-1, keepdims=True)
    d = y - mu
    var = jnp.mean(d * d, axis=-1, keepdims=True)
    return d * lax.rsqrt(var + NORM_EPS) * g + b


def _proj_kernel(x_ref, w_ref, o_ref):
    acc = _dot(x_ref[...], w_ref[...])
    for h in range(o_ref.shape[1]):
        o_ref[0, h] = acc[:, h * LANE:(h + 1) * LANE].astype(o_ref.dtype)


def _proj_f32w_kernel(x_ref, w_ref, o_ref, wb_ref, *, col_scale):
    @pl.when(pl.program_id(1) == 0)
    def _():
        w = w_ref[0]
        if col_scale:
            j = pl.program_id(0)
            scale = jnp.float32(1.0)
            for blk, s in col_scale.items():
                scale = jnp.where(j == blk, jnp.float32(s), scale)
            w = w * scale
        wb_ref[...] = w.astype(BF16)

    acc = _dot(x_ref[...], wb_ref[...])
    for h in range(o_ref.shape[1]):
        o_ref[0, h] = acc[:, h * LANE:(h + 1) * LANE].astype(o_ref.dtype)


def _proj_f32w(x_bf, w_all, layer, n_cols, batch, seq, tm, tn, col_scale):
    n, k = x_bf.shape
    tm = min(tm, seq)
    spb = seq // tm
    return pl.pallas_call(
        functools.partial(_proj_f32w_kernel, col_scale=col_scale),
        grid=(n_cols // tn, n // tm),
        in_specs=[pl.BlockSpec((tm, k), lambda j, i: (i, 0)),
                  pl.BlockSpec((1, k, tn), lambda j, i: (layer, 0, j))],
        out_specs=pl.BlockSpec((1, tn // LANE, tm, LANE), lambda j, i: (i // spb, j, i % spb, 0)),
        out_shape=jax.ShapeDtypeStruct((batch, n_cols // LANE, seq, LANE), BF16),
        scratch_shapes=[pltpu.VMEM((k, tn), BF16)],
        compiler_params=_cparams("arbitrary", "arbitrary"),
        name="in_proj_main",
    )(x_bf, w_all)


def _proj(x_bf, w, batch, seq, tm, tn):
    n, k = x_bf.shape
    c = w.shape[1]
    tm = min(tm, seq)
    spb = seq // tm
    return pl.pallas_call(
        _proj_kernel,
        grid=(n // tm, c // tn),
        in_specs=[pl.BlockSpec((tm, k), lambda i, j: (i, 0)),
                  pl.BlockSpec((k, tn), lambda i, j: (0, j))],
        out_specs=pl.BlockSpec((1, tn // LANE, tm, LANE), lambda i, j: (i // spb, j, i % spb, 0)),
        out_shape=jax.ShapeDtypeStruct((batch, c // LANE, seq, LANE), BF16),
        compiler_params=_cparams("parallel", "parallel"),
        name="in_proj",
    )(x_bf, w)


def _retention_tables():
    h = np.arange(N_HEADS, dtype=np.float64)
    log_gamma = np.log1p(-np.exp2(-5.0 - h))
    pos = np.arange(RET_CHUNK, dtype=np.float64)
    rel = pos[:, None] - pos[None, :]
    intra = np.where(rel >= 0, np.exp(log_gamma[:, None, None] * np.maximum(rel, 0.0)), 0.0)
    ones = np.ones((1, 1, RET_CHUNK))
    qdec = np.exp(log_gamma[:, None] * (pos + 1.0))[:, :, None] * ones
    kdec = np.exp(log_gamma[:, None] * (RET_CHUNK - 1 - pos))[:, :, None] * ones
    cdec = [float(np.float32(np.exp(lg * RET_CHUNK))) for lg in log_gamma]
    return (jnp.asarray(intra, F32), jnp.asarray(qdec, F32), jnp.asarray(kdec, F32), cdec)


def _retention_kernel(q_ref, k_ref, v_ref, g_ref, cos_ref, sin_ref, intra_ref, qdec_ref, kdec_ref,
                      o_ref, state_ref, *, n_chunk, cdec):
    @pl.when(pl.program_id(1) == 0)
    def _():
        state_ref[...] = jnp.zeros_like(state_ref)

    k_scale = HEAD_W ** -0.5
    for c in range(n_chunk):
        rows = slice(c * RET_CHUNK, (c + 1) * RET_CHUNK)
        cos = cos_ref[0, rows, :]
        sin = sin_ref[0, rows, :]
        for h in range(N_HEADS):
            q = q_ref[0, h, rows, :].astype(F32)
            k = k_ref[0, h, rows, :].astype(F32)
            q = q * cos + pltpu.roll(q, HEAD_W // 2, 1) * sin
            k = (k * cos + pltpu.roll(k, HEAD_W // 2, 1) * sin) * k_scale
            v = v_ref[0, h, rows, :]
            st = state_ref[h]
            scores = _dot_nt(q.astype(BF16), k.astype(BF16)) * intra_ref[h]
            o = _dot(scores.astype(BF16), v) + _dot((q * qdec_ref[h]).astype(BF16), st.astype(BF16))
            k_dec_t = jnp.transpose(k * kdec_ref[h]).astype(BF16)
            state_ref[h] = st * cdec[h] + _dot(k_dec_t, v)
            o = o * lax.rsqrt(jnp.mean(o * o, axis=-1, keepdims=True) + NORM_EPS)
            g = g_ref[0, h, rows, :].astype(F32)
            o_ref[0, h, rows, :] = (o * (g * jax.nn.sigmoid(g))).astype(o_ref.dtype)


def _retention(p, cos_r, sin_r, rows_per_step):
    batch, _, seq, _ = p.shape
    r = min(rows_per_step, seq)
    intra, qdec, kdec, cdec = _retention_tables()

    def pspec(cb):
        return pl.BlockSpec((1, N_HEADS, r, LANE), lambda b, i: (b, cb // N_HEADS, i, 0))

    tspec = pl.BlockSpec((1, r, LANE), lambda b, i: (b, i, 0))
    cspec = pl.BlockSpec((N_HEADS, RET_CHUNK, RET_CHUNK), lambda b, i: (0, 0, 0))
    return pl.pallas_call(
        functools.partial(_retention_kernel, n_chunk=r // RET_CHUNK, cdec=cdec),
        grid=(batch, seq // r),
        in_specs=[pspec(CB_RQ), pspec(CB_RK), pspec(CB_RV), pspec(CB_RG), tspec, tspec, cspec, cspec, cspec],
        out_specs=pl.BlockSpec((1, N_HEADS, r, LANE), lambda b, i: (b, 0, i, 0)),
        out_shape=jax.ShapeDtypeStruct((batch, N_HEADS, seq, LANE), BF16),
        scratch_shapes=[pltpu.VMEM((N_HEADS, HEAD_W, HEAD_W), F32)],
        compiler_params=_cparams("parallel", "arbitrary"),
        name="retention",
    )(p, p, p, p, cos_r, sin_r, intra, qdec, kdec)


def _sb_kernel(q_ref, k_ref, v_ref, u_ref, o_ref, *, tq, nq):
    i = pl.program_id(2)
    u = u_ref[...]
    row = lax.broadcasted_iota(jnp.int32, (tq, tq), 0)
    col = lax.broadcasted_iota(jnp.int32, (tq, tq), 1)
    strict = col < row

    def attend(nblk):
        for hh in range(q_ref.shape[1]):
            attend_head(nblk, hh)

    def attend_head(nblk, hh):
        z = _dot_nt(q_ref[0, hh], k_ref[0, hh, :nblk * tq, :])
        log_beta = jnp.minimum(z, 0.0) - jnp.log(1.0 + jnp.exp2(-jnp.abs(z))) * LOG2E
        log_keep = log_beta - z
        c = jnp.zeros((tq, 1), F32)
        ws = [None] * nblk
        for b in reversed(range(nblk)):
            diag = b == nblk - 1
            lk = log_keep[:, b * tq:(b + 1) * tq]
            if diag:
                lk = jnp.where(strict, lk, 0.0)
            after = _dot(lk.astype(BF16), u)
            w = jnp.exp2(log_beta[:, b * tq:(b + 1) * tq] + after + c)
            if diag:
                w = jnp.where(strict, w, 0.0)
            ws[b] = w.astype(BF16)
            c = c + jnp.sum(lk, axis=-1, keepdims=True)
        w_all = ws[0] if nblk == 1 else jnp.concatenate(ws, axis=-1)
        o_ref[0, hh] = _dot(w_all, v_ref[0, hh, :nblk * tq, :]).astype(o_ref.dtype)

    for nblk in range(1, nq + 1):
        pl.when(i == nblk - 1)(functools.partial(attend, nblk))


def _stick_breaking(p, tq, hp):
    batch, _, seq, _ = p.shape
    tq = min(tq, seq)
    idx = np.arange(tq)
    u = jnp.asarray(idx[:, None] > idx[None, :], BF16)
    return pl.pallas_call(
        functools.partial(_sb_kernel, tq=tq, nq=seq // tq),
        grid=(batch, N_HEADS // hp, seq // tq),
        in_specs=[pl.BlockSpec((1, hp, tq, LANE), lambda b, h, i: (b, CB_SQ // hp + h, i, 0)),
                  pl.BlockSpec((1, hp, seq, LANE), lambda b, h, i: (b, CB_SK // hp + h, 0, 0)),
                  pl.BlockSpec((1, hp, seq, LANE), lambda b, h, i: (b, CB_SV // hp + h, 0, 0)),
                  pl.BlockSpec((tq, tq), lambda b, h, i: (0, 0))],
        out_specs=pl.BlockSpec((1, hp, tq, LANE), lambda b, h, i: (b, h, i, 0)),
        out_shape=jax.ShapeDtypeStruct((batch, N_HEADS, seq, LANE), BF16),
        compiler_params=_cparams("parallel", "parallel", "arbitrary"),
        name="stick_breaking",
    )(p, p, p, u)


def _mla_prep_kernel(mq_ref, mkv_ref, kr_ref, tq_ref, tk_ref, qn_ref, kvn_ref, wq_ref, wkv_ref,
                     q_out, kn_out, v_out, kpe_out):
    def rms(ref, g_ref):
        x = jnp.concatenate([ref[0, c].astype(F32) for c in range(MLA_LORA // LANE)], axis=-1)
        y = x * lax.rsqrt(jnp.mean(x * x, axis=-1, keepdims=True) + NORM_EPS)
        return (y * g_ref[...]).astype(BF16)

    qf = _dot(rms(mq_ref, qn_ref), wq_ref[...])
    tq = tq_ref[0]
    for h in range(N_HEADS):
        t = qf[:, h * 2 * LANE:(h + 1) * 2 * LANE] * tq
        u = t[:, LANE:]
        pe = u + pltpu.roll(u, LANE // 2, 1)
        q_out[0, h] = jnp.concatenate([t[:, :LANE], pe], axis=-1).astype(q_out.dtype)
    kv = _dot(rms(mkv_ref, kvn_ref), wkv_ref[...])
    for h in range(N_HEADS):
        kn_out[0, h] = kv[:, h * 2 * LANE:h * 2 * LANE + LANE].astype(kn_out.dtype)
        v_out[0, h] = kv[:, h * 2 * LANE + LANE:(h + 1) * 2 * LANE].astype(v_out.dtype)
    t = kr_ref[0, 0].astype(F32) * tk_ref[0]
    kp = t + pltpu.roll(t, LANE // 2, 1)
    lane = lax.broadcasted_iota(jnp.int32, kp.shape, 1)
    kpe_out[0] = jnp.where(lane < MLA_ROPE, kp, 0.0).astype(kpe_out.dtype)


def _mla_prep(p, kr, tq_tab, tk_tab, q_norm, kv_norm, wq, wkv, tm):
    batch, _, seq, _ = p.shape
    tm = min(tm, seq)
    nl = MLA_LORA // LANE
    head_out = lambda w: pl.BlockSpec((1, N_HEADS, tm, w), lambda b, i: (b, 0, i, 0))
    return pl.pallas_call(
        _mla_prep_kernel,
        grid=(batch, seq // tm),
        in_specs=[pl.BlockSpec((1, nl, tm, LANE), lambda b, i: (b, CB_MQ // nl, i, 0)),
                  pl.BlockSpec((1, nl, tm, LANE), lambda b, i: (b, CB_MKV // nl, i, 0)),
                  pl.BlockSpec((1, 1, tm, LANE), lambda b, i: (b, 0, i, 0)),
                  pl.BlockSpec((1, tm, 2 * LANE), lambda b, i: (b, i, 0)),
                  pl.BlockSpec((1, tm, LANE), lambda b, i: (b, i, 0)),
                  pl.BlockSpec((1, MLA_LORA), lambda b, i: (0, 0)),
                  pl.BlockSpec((1, MLA_LORA), lambda b, i: (0, 0)),
                  pl.BlockSpec(wq.shape, lambda b, i: (0, 0)),
                  pl.BlockSpec(wkv.shape, lambda b, i: (0, 0))],
        out_specs=[head_out(2 * LANE), head_out(LANE), head_out(LANE),
                   pl.BlockSpec((1, tm, LANE), lambda b, i: (b, i, 0))],
        out_shape=[jax.ShapeDtypeStruct((batch, N_HEADS, seq, 2 * LANE), BF16),
                   jax.ShapeDtypeStruct((batch, N_HEADS, seq, LANE), BF16),
                   jax.ShapeDtypeStruct((batch, N_HEADS, seq, LANE), BF16),
                   jax.ShapeDtypeStruct((batch, seq, LANE), BF16)],
        compiler_params=_cparams("parallel", "parallel"),
        name="mla_prep",
    )(p, p, kr, tq_tab, tk_tab, q_norm, kv_norm, wq, wkv)


def _mla_attn_kernel(q_ref, kn_ref, kpe_ref, v_ref, o_ref, *, tq, nq):
    i = pl.program_id(2)
    row = lax.broadcasted_iota(jnp.int32, (tq, tq), 0)
    col = lax.broadcasted_iota(jnp.int32, (tq, tq), 1)
    causal = col <= row

    def attend(nblk):
        for hh in range(q_ref.shape[1]):
            attend_head(nblk, hh)

    def attend_head(nblk, hh):
        n_keys = nblk * tq
        k = jnp.concatenate([kn_ref[0, hh, :n_keys, :], kpe_ref[0, :n_keys, :]], axis=-1)
        s = _dot_nt(q_ref[0, hh], k)
        last = jnp.where(causal, s[:, n_keys - tq:], -1e30)
        s = last if nblk == 1 else jnp.concatenate([s[:, :n_keys - tq], last], axis=-1)
        pr = jnp.exp(s - jnp.max(s, axis=-1, keepdims=True))
        l = jnp.sum(pr, axis=-1, keepdims=True)
        o_ref[0, hh] = (_dot(pr.astype(BF16), v_ref[0, hh, :n_keys, :]) / l).astype(o_ref.dtype)

    for nblk in range(1, nq + 1):
        pl.when(i == nblk - 1)(functools.partial(attend, nblk))


def _mla_attn(q, kn, kpe, v, tq, hp):
    batch, _, seq, _ = q.shape
    tq = min(tq, seq)
    return pl.pallas_call(
        functools.partial(_mla_attn_kernel, tq=tq, nq=seq // tq),
        grid=(batch, N_HEADS // hp, seq // tq),
        in_specs=[pl.BlockSpec((1, hp, tq, 2 * LANE), lambda b, h, i: (b, h, i, 0)),
                  pl.BlockSpec((1, hp, seq, LANE), lambda b, h, i: (b, h, 0, 0)),
                  pl.BlockSpec((1, seq, LANE), lambda b, h, i: (b, 0, 0)),
                  pl.BlockSpec((1, hp, seq, LANE), lambda b, h, i: (b, h, 0, 0))],
        out_specs=pl.BlockSpec((1, hp, tq, LANE), lambda b, h, i: (b, h, i, 0)),
        out_shape=jax.ShapeDtypeStruct((batch, N_HEADS, seq, LANE), BF16),
        compiler_params=_cparams("parallel", "parallel", "arbitrary"),
        name="mla_attn",
    )(q, kn, kpe, v)


def _merge_kernel(r_ref, s_ref, m_ref, g0_ref, g1_ref, g2_ref, w_ref, o_ref, wb_ref):
    @pl.when(pl.program_id(1) == 0)
    def _():
        wb_ref[...] = w_ref[0].astype(BF16)

    acc = None
    for n, (b_ref, g_ref) in enumerate(((r_ref, g0_ref), (s_ref, g1_ref), (m_ref, g2_ref))):
        a = jnp.concatenate([b_ref[0, h] for h in range(N_HEADS)], axis=-1)
        y = _dot(a, wb_ref[n])
        g = jax.nn.sigmoid(jnp.concatenate([g_ref[0, c].astype(F32) for c in range(g_ref.shape[1])], axis=-1))
        acc = g * y if acc is None else acc + g * y
    o_ref[...] = acc.astype(o_ref.dtype)


def _merge(ret, sb, ml, pg, w_branch_all, layer, tm, tn):
    batch, _, seq, _ = pg.shape
    d = w_branch_all.shape[-1]
    tm = min(tm, seq)
    spb = seq // tm
    gcb = tn // LANE
    bspec = pl.BlockSpec((1, N_HEADS, tm, LANE), lambda j, i: (i // spb, 0, i % spb, 0))

    def gspec(n):
        first = n * (d // LANE) // gcb
        return pl.BlockSpec((1, gcb, tm, LANE), lambda j, i: (i // spb, first + j, i % spb, 0))

    return pl.pallas_call(
        _merge_kernel,
        grid=(d // tn, batch * spb),
        in_specs=[bspec, bspec, bspec, gspec(0), gspec(1), gspec(2),
                  pl.BlockSpec((1, N_BRANCH, BRANCH_W, tn), lambda j, i: (layer, 0, 0, j))],
        out_specs=pl.BlockSpec((tm, tn), lambda j, i: (i, j)),
        out_shape=jax.ShapeDtypeStruct((batch * seq, d), BF16),
        scratch_shapes=[pltpu.VMEM((N_BRANCH, BRANCH_W, tn), BF16)],
        compiler_params=_cparams("arbitrary", "arbitrary"),
        name="branch_merge",
    )(ret, sb, ml, pg, pg, pg, w_branch_all)


def _route(logits):
    lane = lax.broadcasted_iota(jnp.int32, logits.shape, 1)
    neg = -jnp.inf
    big = jnp.int32(1 << 20)
    is_g = lane < N_GROUPS
    gl = jnp.where(is_g, logits, neg)
    gm = jnp.max(gl, axis=-1, keepdims=True)
    g_sel = jnp.min(jnp.where(gl == gm, lane, big), axis=-1, keepdims=True)
    g_w = 1.0 / jnp.sum(jnp.where(is_g, jnp.exp(gl - gm), 0.0), axis=-1, keepdims=True)
    lo = N_GROUPS + EXPERTS_PER_GROUP * g_sel
    el = jnp.where((lane >= lo) & (lane < lo + EXPERTS_PER_GROUP), logits, neg)
    t1 = jnp.max(el, axis=-1, keepdims=True)
    i1 = jnp.min(jnp.where(el == t1, lane, big), axis=-1, keepdims=True)
    el2 = jnp.where(lane == i1, neg, el)
    t2 = jnp.max(el2, axis=-1, keepdims=True)
    i2 = jnp.min(jnp.where(el2 == t2, lane, big), axis=-1, keepdims=True)
    dd = jnp.exp(t2 - t1)
    w1 = g_w / (1.0 + dd)
    w2 = g_w * dd / (1.0 + dd)
    e1 = (i1 - N_GROUPS).astype(F32)
    e2 = (i2 - N_GROUPS).astype(F32)
    return jnp.where(lane == 0, e1, jnp.where(lane == 1, e2, jnp.where(lane == 2, w1, jnp.where(lane == 3, w2, 0.0))))


def _pack_halves(x):
    half = x.shape[1] // 2
    hi = lax.bitcast_convert_type(x[:, :half].astype(BF16).astype(F32), jnp.uint32)
    lo = lax.bitcast_convert_type(x[:, half:].astype(BF16).astype(F32), jnp.uint32)
    return hi | (lo >> 16)


def _unpack_halves(w):
    hi = lax.bitcast_convert_type(w & jnp.uint32(0xFFFF0000), F32)
    lo = lax.bitcast_convert_type(w << 16, F32)
    return hi, lo


def _to_token_major(ref, x):
    rows, width = x.shape
    nc = width // LANE
    for c in range(nc):
        ref[pl.ds(c, rows, stride=nc), :] = x[:, c * LANE:(c + 1) * LANE]


def _from_token_major(ref, rows):
    nc = ref.shape[0] // rows
    return [ref[pl.ds(c, rows, stride=nc), :] for c in range(nc)]


def _out_ln_router_kernel(m_ref, w_ref, x_ref, g_ref, b_ref, rw_ref, rb_ref, x1_ref, x1t_ref, r_ref, eid_ref, *,
                          alpha, sub):
    tm, d = x_ref.shape
    nc = d // (2 * LANE)
    rw_hi = rw_ref[:, :LANE]
    for s0 in range(0, tm, sub):
        rows = slice(s0, s0 + sub)
        mix = _dot(m_ref[rows, :], w_ref[...])
        x1 = _layer_norm(alpha * x_ref[rows, :] + mix, g_ref[...], b_ref[...])
        x1_ref[rows, :] = x1
        _to_token_major(x1t_ref.at[pl.ds(s0 * nc, sub * nc), :], _pack_halves(x1))
        xh = x1.astype(BF16)
        xl = (x1 - xh.astype(F32)).astype(BF16)
        both = _dot(xh, rw_ref[...])
        logits = both[:, :LANE] + both[:, LANE:] + _dot(xl, rw_hi) + rb_ref[...]
        route = _route(logits)
        r_ref[rows, :] = route
        eid_ref[:, rows] = jnp.transpose(route)[:SUBLANE].astype(jnp.int32)


def _out_ln_router(merged, w_out, x, ln_g, ln_b, rw, rb, alpha, tm, sub):
    n, d = x.shape
    tm = min(tm, n)
    sub = min(sub, tm)
    row = lambda w: pl.BlockSpec((tm, w), lambda i: (i, 0))
    full = lambda a: pl.BlockSpec(a.shape, lambda i: (0, 0))
    w_spec = pl.BlockSpec(w_out.shape, lambda i: (0, 0), pipeline_mode=pl.Buffered(1))
    return pl.pallas_call(
        functools.partial(_out_ln_router_kernel, alpha=alpha, sub=sub),
        grid=(n // tm,),
        in_specs=[row(d), w_spec, row(d), full(ln_g), full(ln_b), full(rw), full(rb)],
        out_specs=[row(d), pl.BlockSpec((tm * (d // (2 * LANE)), LANE), lambda i: (i, 0)), row(LANE),
                   pl.BlockSpec((SUBLANE, tm), lambda i: (0, i))],
        out_shape=[jax.ShapeDtypeStruct((n, d), F32), jax.ShapeDtypeStruct((n * (d // (2 * LANE)), LANE), jnp.uint32),
                   jax.ShapeDtypeStruct((n, LANE), F32), jax.ShapeDtypeStruct((SUBLANE, n), jnp.int32)],
        compiler_params=_cparams("parallel"),
        name="out_proj_ln_router",
    )(merged, w_out, x, ln_g, ln_b, rw, rb)


def _moe_kernel(bexp_ref, nact_ref, nval_ref, sbase_ref, tok_ref, dst_ref,
                x_hbm, w1_ref, w3_ref, w2_ref, out_hbm,
                xbuf, xbf, ybuf, w1b, w3b, w2b, sem_in, sem_out, *, bm, nc):
    i = pl.program_id(0)
    nact = nact_ref[0]

    @pl.when((i < nact) & ((i == 0) | (bexp_ref[i] != bexp_ref[jnp.maximum(i - 1, 0)])))
    def _():
        w1b[...] = w1_ref[0, 0].astype(BF16)
        w3b[...] = w3_ref[0, 0].astype(BF16)
        w2b[...] = w2_ref[0, 0].astype(BF16)

    def gather(r, base):
        src = pl.multiple_of(tok_ref[base + r], nc)
        return pltpu.make_async_copy(x_hbm.at[pl.ds(src, nc), :], xbuf.at[r // SUBLANE, :, r % SUBLANE, :], sem_in)

    def scatter(r, base):
        dst = pl.multiple_of(dst_ref[base + r], nc)
        return pltpu.make_async_copy(ybuf.at[r // SUBLANE, :, r % SUBLANE, :], out_hbm.at[pl.ds(dst, nc), :], sem_out)

    def wait_rows(n_tok, sem):
        rows = pl.multiple_of(n_tok * nc, nc)

        @pl.when(n_tok > 0)
        def _():
            pltpu.make_async_copy(x_hbm.at[pl.ds(0, rows), :], out_hbm.at[pl.ds(0, rows), :], sem).wait()

    @pl.when(i == 0)
    def _():
        for r in range(bm):
            gather(r, sbase_ref[0]).start(priority=r % 2)

    @pl.when(i < nact)
    def _():
        has_next = i + 1 < nact
        prev = jnp.maximum(i - 1, 0)
        nv_prev = jnp.where(i >= 1, nval_ref[prev], 0)
        nv_head = jnp.minimum(nv_prev, SUBLANE)
        base_prev = sbase_ref[prev]
        base_next = sbase_ref[i + 1]
        for r in range(bm):
            pl.when(r < nv_prev)(functools.partial(scatter(r, base_prev).start, priority=r % 2))
        wait_rows(jnp.int32(bm), sem_in)
        half = nc * LANE
        for c in range(nc):
            hi, lo = _unpack_halves(xbuf[:, c].reshape(bm, LANE))
            xbf[:, c * LANE:(c + 1) * LANE] = hi.astype(BF16)
            xbf[:, half + c * LANE:half + (c + 1) * LANE] = lo.astype(BF16)
        for r in range(bm):
            pl.when(has_next)(functools.partial(gather(r, base_next).start, priority=r % 2))
        wait_rows(nv_head, sem_out)
        xb = xbf[...]
        h1 = _dot(xb, w1b[...])
        h3 = _dot(xb, w3b[...])
        hh = (h1 * jax.nn.sigmoid(h1) * h3).astype(BF16)
        wait_rows(nv_prev - nv_head, sem_out)
        y = _pack_halves(_dot(hh, w2b[...]))
        for c in range(nc):
            ybuf[:, c] = y[:, c * LANE:(c + 1) * LANE].reshape(bm // SUBLANE, SUBLANE, LANE)

    @pl.when(i == nact)
    def _():
        nv = nval_ref[i - 1]
        base = sbase_ref[i - 1]
        for r in range(bm):
            pl.when(r < nv)(functools.partial(scatter(r, base).start, priority=r % 2))
        wait_rows(nv, sem_out)


def _moe_experts(x1t, eid_t, w1, w3, w2, layer, bm):
    _, ne, d, de = w1.shape
    nc = d // (2 * LANE)
    n = x1t.shape[0] // nc
    a = 2 * n
    nb = a // bm + ne
    eid = eid_t[:2].reshape(a)
    order = jnp.sort(eid * a + jnp.arange(a, dtype=jnp.int32)) % a
    experts = jnp.arange(ne, dtype=jnp.int32)
    counts = jnp.sum((eid[None, :] == experts[:, None]).astype(jnp.int32), axis=1)
    padded = (counts + bm - 1) // bm * bm
    upto = experts[None, :] <= experts[:, None]
    start = jnp.sum(jnp.where(upto, counts[None, :], 0), axis=1) - counts
    pend = jnp.sum(jnp.where(upto, padded[None, :], 0), axis=1)
    pstart = pend - padded
    blk = jnp.arange(nb, dtype=jnp.int32)
    blk_start = blk * bm
    n_active = pend[ne - 1] // bm
    active = blk < n_active
    blk_exp = jnp.minimum(jnp.sum((pend[None, :] <= blk_start[:, None]).astype(jnp.int32), axis=1), ne - 1)
    mine = blk_exp[:, None] == experts[None, :]
    pick = lambda v: jnp.sum(jnp.where(mine, v[None, :], 0), axis=1)
    offset = blk_start - pick(pstart)
    blk_valid = jnp.where(active, jnp.clip(pick(counts) - offset, 0, bm), 0).astype(jnp.int32)
    blk_base = jnp.where(active, pick(start) + offset, 0).astype(jnp.int32)
    blk_exp = jnp.where(active, blk_exp, jnp.max(jnp.where(active, blk_exp, 0))).astype(jnp.int32)
    tail = jnp.zeros((bm,), jnp.int32)
    row_tok = jnp.concatenate([jnp.where(order >= n, order - n, order) * nc, tail])
    row_dst = jnp.concatenate([order * nc, tail])

    grid_spec = pltpu.PrefetchScalarGridSpec(
        num_scalar_prefetch=6,
        grid=(nb,),
        in_specs=[pl.BlockSpec(memory_space=pl.ANY),
                  pl.BlockSpec((1, 1, d, de), lambda i, be, *_: (layer, be[i], 0, 0)),
                  pl.BlockSpec((1, 1, d, de), lambda i, be, *_: (layer, be[i], 0, 0)),
                  pl.BlockSpec((1, 1, de, d), lambda i, be, *_: (layer, be[i], 0, 0))],
        out_specs=pl.BlockSpec(memory_space=pl.ANY),
        scratch_shapes=[pltpu.VMEM((bm // SUBLANE, nc, SUBLANE, LANE), jnp.uint32), pltpu.VMEM((bm, d), BF16),
                        pltpu.VMEM((bm // SUBLANE, nc, SUBLANE, LANE), jnp.uint32),
                        pltpu.VMEM((d, de), BF16), pltpu.VMEM((d, de), BF16), pltpu.VMEM((de, d), BF16),
                        pltpu.SemaphoreType.DMA(()), pltpu.SemaphoreType.DMA(())],
    )
    return pl.pallas_call(
        functools.partial(_moe_kernel, bm=bm, nc=nc),
        grid_spec=grid_spec,
        out_shape=jax.ShapeDtypeStruct((a * nc, LANE), jnp.uint32),
        compiler_params=_cparams("arbitrary"),
        name="moe_experts",
    )(blk_exp, n_active.reshape(1).astype(jnp.int32), blk_valid, blk_base, row_tok, row_dst, x1t, w1, w3, w2)


def _combine_ln_kernel(x_ref, y0_ref, y1_ref, r_ref, g_ref, b_ref, o_ref, obf_ref, *, alpha):
    rows = x_ref.shape[0]
    gate0 = r_ref[:, 2:3]
    gate1 = r_ref[:, 3:4]
    his, los = [], []
    for w0, w1 in zip(_from_token_major(y0_ref, rows), _from_token_major(y1_ref, rows)):
        hi0, lo0 = _unpack_halves(w0)
        hi1, lo1 = _unpack_halves(w1)
        his.append(gate0 * hi0 + gate1 * hi1)
        los.append(gate0 * lo0 + gate1 * lo1)
    y = jnp.concatenate(his + los, axis=-1)
    x2 = _layer_norm(alpha * x_ref[...] + y, g_ref[...], b_ref[...])
    o_ref[...] = x2
    obf_ref[...] = x2.astype(BF16)


def _combine_ln(x1, y2, route, ln_g, ln_b, alpha, tm):
    n, d = x1.shape
    tm = min(tm, n)
    steps = n // tm
    nc = d // (2 * LANE)
    return pl.pallas_call(
        functools.partial(_combine_ln_kernel, alpha=alpha),
        grid=(steps,),
        in_specs=[pl.BlockSpec((tm, d), lambda i: (i, 0)), pl.BlockSpec((tm * nc, LANE), lambda i: (i, 0)),
                  pl.BlockSpec((tm * nc, LANE), lambda i: (steps + i, 0)),
                  pl.BlockSpec((tm, LANE), lambda i: (i, 0)),
                  pl.BlockSpec((1, d), lambda i: (0, 0)), pl.BlockSpec((1, d), lambda i: (0, 0))],
        out_specs=[pl.BlockSpec((tm, d), lambda i: (i, 0)), pl.BlockSpec((tm, d), lambda i: (i, 0))],
        out_shape=[jax.ShapeDtypeStruct((n, d), F32), jax.ShapeDtypeStruct((n, d), BF16)],
        compiler_params=_cparams("parallel"),
        name="combine_ln",
    )(x1, y2, y2, route, ln_g, ln_b)


def _rot_half_cols(w):
    half = w.shape[-1] // 2
    return jnp.concatenate([-w[..., half:], w[..., :half]], axis=-1)


def _rope_cos_sin(positions, dim):
    inv = 1.0 / (ROPE_BASE ** (jnp.arange(0, dim, 2, dtype=F32) / dim))
    ang = positions.astype(F32)[..., None] * inv
    return jnp.cos(ang), jnp.sin(ang)


def kernel(x, positions, w_in, mla_q_norm, mla_w_q_b, mla_kv_norm, mla_w_kv_b, w_branch, w_out, ln1_g, ln1_b, router_group_w, router_group_b, router_expert_w, router_expert_b, expert_w1, expert_w3, expert_w2, ln2_g, ln2_b):
    batch, seq, d = x.shape
    depth = w_in.shape[0]
    n = batch * seq
    alpha = (2 * depth) ** 0.25
    n_main = (CB_MQ + 2 * MLA_LORA // LANE) * LANE

    cr, sr = _rope_cos_sin(positions, HEAD_W)
    cos_r = jnp.concatenate([cr, cr], axis=-1)
    sin_r = jnp.concatenate([-sr, sr], axis=-1)
    cm, sm = _rope_cos_sin(positions, MLA_ROPE)
    q_scale = (MLA_NOPE + MLA_ROPE) ** -0.5
    tq_tab = q_scale * jnp.concatenate([jnp.ones((batch, seq, MLA_NOPE), F32), cm, cm, sm, sm], axis=-1)
    tk_tab = jnp.concatenate([cm, cm, sm, sm], axis=-1)

    xf = x.reshape(n, d)
    xb = xf.astype(BF16)
    for l in range(depth):
        w_kr = w_in[l][:, n_main:n_main + MLA_ROPE]
        w_gate = w_in[l][:, n_main + MLA_ROPE:].astype(BF16)
        w_kr2 = jnp.concatenate([w_kr, _rot_half_cols(w_kr)], axis=-1).astype(BF16)
        wq = mla_w_q_b[l].reshape(MLA_LORA, N_HEADS, MLA_NOPE + MLA_ROPE)
        wq_pe = wq[..., MLA_NOPE:]
        wq = jnp.concatenate([wq[..., :MLA_NOPE], wq_pe, _rot_half_cols(wq_pe)], axis=-1)
        wq = wq.reshape(MLA_LORA, N_HEADS * 2 * LANE).astype(BF16)
        wkv = mla_w_kv_b[l].astype(BF16)
        rw = jnp.concatenate([router_group_w[l], router_expert_w[l],
                              jnp.zeros((d, LANE - N_GROUPS - N_EXPERTS), F32)], axis=-1)
        rw_hi = rw.astype(BF16)
        rw_cat = jnp.concatenate([rw_hi, (rw - rw_hi.astype(F32)).astype(BF16)], axis=-1)
        rb =jnp.concatenate([router_group_b[l], router_expert_b[l],
                              jnp.zeros((LANE - N_GROUPS - N_EXPERTS,), F32)]).reshape(1, LANE)

        p = _proj_f32w(xb, w_in, l, n_main, batch, seq, tm=1024, tn=N_HEADS * LANE,
                       col_scale={CB_SQ // N_HEADS: LOG2E * HEAD_W ** -0.5})
        pg = _proj(xb, w_gate, batch, seq, tm=1024, tn=1024)
        kr = _proj(xb, w_kr2, batch, seq, tm=1024, tn=LANE)
        ret = _retention(p, cos_r, sin_r, rows_per_step=512)
        sb = _stick_breaking(p, tq=256, hp=2)
        q, kn, v, kpe = _mla_prep(p, kr, tq_tab, tk_tab, mla_q_norm[l].reshape(1, -1), mla_kv_norm[l].reshape(1, -1),
                                  wq, wkv, tm=512)
        ml = _mla_attn(q, kn, kpe, v, tq=256, hp=4)
        merged = _merge(ret, sb, ml, pg, w_branch, l, tm=512, tn=1024)
        x1, x1t, route, eid_t = _out_ln_router(merged, w_out[l].astype(BF16), xf, ln1_g[l].reshape(1, d),
                                               ln1_b[l].reshape(1, d), rw_cat, rb, alpha, tm=512, sub=256)
        y2 = _moe_experts(x1t, eid_t, expert_w1, expert_w3, expert_w2, l, MOE_ROWS)
        xf, xb = _combine_ln(x1, y2, route, ln2_g[l].reshape(1, d), ln2_b[l].reshape(1, d), alpha, tm=512)
    return xf.reshape(batch, seq, d)
```

```python
import functools

import numpy as np
import jax
import jax.numpy as jnp
from jax import lax
from jax.experimental import pallas as pl
from jax.experimental.pallas import tpu as pltpu

F32 = jnp.float32
BF16 = jnp.bfloat16

LANE = 128
SUBLANE = 8
N_HEADS = 8
HEAD_W = 128
BRANCH_W = N_HEADS * HEAD_W
N_BRANCH = 3
RET_CHUNK = 128
MLA_LORA = 512
MLA_NOPE = 128
MLA_ROPE = 64
ROPE_BASE = 10000.0
N_GROUPS = 4
EXPERTS_PER_GROUP = 8
N_EXPERTS = N_GROUPS * EXPERTS_PER_GROUP
NORM_EPS = 1e-5
LOG2E = 1.4426950408889634
MOE_ROWS = 256
VMEM_LIMIT = 56 * 1024 * 1024

CB_RQ, CB_RK, CB_RV, CB_RG = 0, 8, 16, 24
CB_SQ, CB_SK, CB_SV = 32, 40, 48
CB_MQ, CB_MKV = 56, 60
CB_GATE = 64
N_CB = 112


def _cparams(*sem):
    return pltpu.CompilerParams(dimension_semantics=sem, vmem_limit_bytes=VMEM_LIMIT)


def _dot(a, b):
    return jnp.dot(a, b, preferred_element_type=F32)


def _dot_nt(a, b):
    return lax.dot_general(a, b, (((1,), (1,)), ((), ())), preferred_element_type=F32)


def _layer_norm(y, g, b):
    mu = jnp.mean(y, axis=-1, keepdims=True)
    d = y - mu
    var = jnp.mean(d * d, axis=-1, keepdims=True)
    return d * lax.rsqrt(var + NORM_EPS) * g + b


def _proj_kernel(x_ref, w_ref, o_ref):
    acc = _dot(x_ref[...], w_ref[...])
    for h in range(o_ref.shape[1]):
        o_ref[0, h] = acc[:, h * LANE:(h + 1) * LANE].astype(o_ref.dtype)


def _proj(x_bf, w, batch, seq, tm, tn):
    n, k = x_bf.shape
    c = w.shape[1]
    tm = min(tm, seq)
    spb = seq // tm
    return pl.pallas_call(
        _proj_kernel,
        grid=(n // tm, c // tn),
        in_specs=[pl.BlockSpec((tm, k), lambda i, j: (i, 0)),
                  pl.BlockSpec((k, tn), lambda i, j: (0, j))],
        out_specs=pl.BlockSpec((1, tn // LANE, tm, LANE), lambda i, j: (i // spb, j, i % spb, 0)),
        out_shape=jax.ShapeDtypeStruct((batch, c // LANE, seq, LANE), BF16),
        compiler_params=_cparams("parallel", "parallel"),
        name="in_proj",
    )(x_bf, w)


def _retention_tables():
    h = np.arange(N_HEADS, dtype=np.float64)
    log_gamma = np.log1p(-np.exp2(-5.0 - h))
    pos = np.arange(RET_CHUNK, dtype=np.float64)
    rel = pos[:, None] - pos[None, :]
    intra = np.where(rel >= 0, np.exp(log_gamma[:, None, None] * np.maximum(rel, 0.0)), 0.0)
    ones = np.ones((1, 1, RET_CHUNK))
    qdec = np.exp(log_gamma[:, None] * (pos + 1.0))[:, :, None] * ones
    kdec = np.exp(log_gamma[:, None] * (RET_CHUNK - 1 - pos))[:, :, None] * ones
    cdec = [float(np.float32(np.exp(lg * RET_CHUNK))) for lg in log_gamma]
    return (jnp.asarray(intra, F32), jnp.asarray(qdec, F32), jnp.asarray(kdec, F32), cdec)


def _retention_kernel(q_ref, k_ref, v_ref, g_ref, cos_ref, sin_ref, intra_ref, qdec_ref, kdec_ref,
                      o_ref, state_ref, *, n_chunk, cdec):
    @pl.when(pl.program_id(1) == 0)
    def _():
        state_ref[...] = jnp.zeros_like(state_ref)

    k_scale = HEAD_W ** -0.5
    for c in range(n_chunk):
        rows = slice(c * RET_CHUNK, (c + 1) * RET_CHUNK)
        cos = cos_ref[0, rows, :]
        sin = sin_ref[0, rows, :]
        for h in range(N_HEADS):
            q = q_ref[0, h, rows, :].astype(F32)
            k = k_ref[0, h, rows, :].astype(F32)
            q = q * cos + pltpu.roll(q, HEAD_W // 2, 1) * sin
            k = (k * cos + pltpu.roll(k, HEAD_W // 2, 1) * sin) * k_scale
            v = v_ref[0, h, rows, :]
            st = state_ref[h]
            scores = _dot_nt(q.astype(BF16), k.astype(BF16)) * intra_ref[h]
            o = _dot(scores.astype(BF16), v) + _dot((q * qdec_ref[h]).astype(BF16), st.astype(BF16))
            k_dec_t = jnp.transpose(k * kdec_ref[h]).astype(BF16)
            state_ref[h] = st * cdec[h] + _dot(k_dec_t, v)
            o = o * lax.rsqrt(jnp.mean(o * o, axis=-1, keepdims=True) + NORM_EPS)
            g = g_ref[0, h, rows, :].astype(F32)
            o_ref[0, h, rows, :] = (o * (g * jax.nn.sigmoid(g))).astype(o_ref.dtype)


def _retention(p, cos_r, sin_r, rows_per_step):
    batch, _, seq, _ = p.shape
    r = min(rows_per_step, seq)
    intra, qdec, kdec, cdec = _retention_tables()

    def pspec(cb):
        return pl.BlockSpec((1, N_HEADS, r, LANE), lambda b, i: (b, cb // N_HEADS, i, 0))

    tspec = pl.BlockSpec((1, r, LANE), lambda b, i: (b, i, 0))
    cspec = pl.BlockSpec((N_HEADS, RET_CHUNK, RET_CHUNK), lambda b, i: (0, 0, 0))
    return pl.pallas_call(
        functools.partial(_retention_kernel, n_chunk=r // RET_CHUNK, cdec=cdec),
        grid=(batch, seq // r),
        in_specs=[pspec(CB_RQ), pspec(CB_RK), pspec(CB_RV), pspec(CB_RG), tspec, tspec, cspec, cspec, cspec],
        out_specs=pl.BlockSpec((1, N_HEADS, r, LANE), lambda b, i: (b, 0, i, 0)),
        out_shape=jax.ShapeDtypeStruct((batch, N_HEADS, seq, LANE), BF16),
        scratch_shapes=[pltpu.VMEM((N_HEADS, HEAD_W, HEAD_W), F32)],
        compiler_params=_cparams("parallel", "arbitrary"),
        name="retention",
    )(p, p, p, p, cos_r, sin_r, intra, qdec, kdec)


def _sb_kernel(q_ref, k_ref, v_ref, u_ref, o_ref, *, tq, nq):
    i = pl.program_id(2)
    u = u_ref[...]
    row = lax.broadcasted_iota(jnp.int32, (tq, tq), 0)
    col = lax.broadcasted_iota(jnp.int32, (tq, tq), 1)
    strict = col < row

    def attend(nblk):
        for hh in range(q_ref.shape[1]):
            attend_head(nblk, hh)

    def attend_head(nblk, hh):
        z = _dot_nt(q_ref[0, hh], k_ref[0, hh, :nblk * tq, :])
        log_keep = jnp.log(1.0 + jnp.exp2(-jnp.abs(z))) * (-LOG2E) - jnp.maximum(z, 0.0)
        c = jnp.zeros((tq, 1), F32)
        ws = [None] * nblk
        for b in reversed(range(nblk)):
            diag = b == nblk - 1
            lk = log_keep[:, b * tq:(b + 1) * tq]
            if diag:
                lk = jnp.where(strict, lk, 0.0)
            w = jnp.exp2(_dot(lk.astype(BF16), u) + (z[:, b * tq:(b + 1) * tq] + c))
            if diag:
                w = jnp.where(strict, w, 0.0)
            ws[b] = w.astype(BF16)
            c = c + jnp.sum(lk, axis=-1, keepdims=True)
        w_all = ws[0] if nblk == 1 else jnp.concatenate(ws, axis=-1)
        o_ref[0, hh] = _dot(w_all, v_ref[0, hh, :nblk * tq, :]).astype(o_ref.dtype)

    for nblk in range(1, nq + 1):
        pl.when(i == nblk - 1)(functools.partial(attend, nblk))


def _stick_breaking(p, tq, hp):
    batch, _, seq, _ = p.shape
    tq = min(tq, seq)
    idx = np.arange(tq)
    u = jnp.asarray(idx[:, None] >= idx[None, :], BF16)
    return pl.pallas_call(
        functools.partial(_sb_kernel, tq=tq, nq=seq // tq),
        grid=(batch, N_HEADS // hp, seq // tq),
        in_specs=[pl.BlockSpec((1, hp, tq, LANE), lambda b, h, i: (b, CB_SQ // hp + h, i, 0)),
                  pl.BlockSpec((1, hp, seq, LANE), lambda b, h, i: (b, CB_SK // hp + h, 0, 0)),
                  pl.BlockSpec((1, hp, seq, LANE), lambda b, h, i: (b, CB_SV // hp + h, 0, 0)),
                  pl.BlockSpec((tq, tq), lambda b, h, i: (0, 0))],
        out_specs=pl.BlockSpec((1, hp, tq, LANE), lambda b, h, i: (b, h, i, 0)),
        out_shape=jax.ShapeDtypeStruct((batch, N_HEADS, seq, LANE), BF16),
        compiler_params=_cparams("parallel", "parallel", "arbitrary"),
        name="stick_breaking",
    )(p, p, p, u)


def _mla_prep_kernel(mq_ref, mkv_ref, kr_ref, tq_ref, tk_ref, qn_ref, kvn_ref, wq_ref, wkv_ref,
                     q_out, kn_out, v_out, kpe_out):
    def rms(ref, g_ref):
        x = jnp.concatenate([ref[0, c].astype(F32) for c in range(MLA_LORA // LANE)], axis=-1)
        y = x * lax.rsqrt(jnp.mean(x * x, axis=-1, keepdims=True) + NORM_EPS)
        return (y * g_ref[...]).astype(BF16)

    qf = _dot(rms(mq_ref, qn_ref), wq_ref[...])
    tq = tq_ref[0]
    for h in range(N_HEADS):
        t = qf[:, h * 2 * LANE:(h + 1) * 2 * LANE] * tq
        u = t[:, LANE:]
        pe = u + pltpu.roll(u, LANE // 2, 1)
        q_out[0, h] = jnp.concatenate([t[:, :LANE], pe], axis=-1).astype(q_out.dtype)
    kv = _dot(rms(mkv_ref, kvn_ref), wkv_ref[...])
    for h in range(N_HEADS):
        kn_out[0, h] = kv[:, h * 2 * LANE:h * 2 * LANE + LANE].astype(kn_out.dtype)
        v_out[0, h] = kv[:, h * 2 * LANE + LANE:(h + 1) * 2 * LANE].astype(v_out.dtype)
    t = kr_ref[0, 0].astype(F32) * tk_ref[0]
    kp = t + pltpu.roll(t, LANE // 2, 1)
    lane = lax.broadcasted_iota(jnp.int32, kp.shape, 1)
    kpe_out[0] = jnp.where(lane < MLA_ROPE, kp, 0.0).astype(kpe_out.dtype)


def _mla_prep(p, kr, tq_tab, tk_tab, q_norm, kv_norm, wq, wkv, tm):
    batch, _, seq, _ = p.shape
    tm = min(tm, seq)
    nl = MLA_LORA // LANE
    head_out = lambda w: pl.BlockSpec((1, N_HEADS, tm, w), lambda b, i: (b, 0, i, 0))
    return pl.pallas_call(
        _mla_prep_kernel,
        grid=(batch, seq // tm),
        in_specs=[pl.BlockSpec((1, nl, tm, LANE), lambda b, i: (b, CB_MQ // nl, i, 0)),
                  pl.BlockSpec((1, nl, tm, LANE), lambda b, i: (b, CB_MKV // nl, i, 0)),
                  pl.BlockSpec((1, 1, tm, LANE), lambda b, i: (b, 0, i, 0)),
                  pl.BlockSpec((1, tm, 2 * LANE), lambda b, i: (b, i, 0)),
                  pl.BlockSpec((1, tm, LANE), lambda b, i: (b, i, 0)),
                  pl.BlockSpec((1, MLA_LORA), lambda b, i: (0, 0)),
                  pl.BlockSpec((1, MLA_LORA), lambda b, i: (0, 0)),
                  pl.BlockSpec(wq.shape, lambda b, i: (0, 0)),
                  pl.BlockSpec(wkv.shape, lambda b, i: (0, 0))],
        out_specs=[head_out(2 * LANE), head_out(LANE), head_out(LANE),
                   pl.BlockSpec((1, tm, LANE), lambda b, i: (b, i, 0))],
        out_shape=[jax.ShapeDtypeStruct((batch, N_HEADS, seq, 2 * LANE), BF16),
                   jax.ShapeDtypeStruct((batch, N_HEADS, seq, LANE), BF16),
                   jax.ShapeDtypeStruct((batch, N_HEADS, seq, LANE), BF16),
                   jax.ShapeDtypeStruct((batch, seq, LANE), BF16)],
        compiler_params=_cparams("parallel", "parallel"),
        name="mla_prep",
    )(p, p, kr, tq_tab, tk_tab, q_norm, kv_norm, wq, wkv)


def _mla_attn_kernel(q_ref, kn_ref, kpe_ref, v_ref, o_ref, *, tq, nq):
    i = pl.program_id(2)
    row = lax.broadcasted_iota(jnp.int32, (tq, tq), 0)
    col = lax.broadcasted_iota(jnp.int32, (tq, tq), 1)
    causal = col <= row

    def attend(nblk):
        for hh in range(q_ref.shape[1]):
            attend_head(nblk, hh)

    def attend_head(nblk, hh):
        n_keys = nblk * tq
        k = jnp.concatenate([kn_ref[0, hh, :n_keys, :], kpe_ref[0, :n_keys, :]], axis=-1)
        s = _dot_nt(q_ref[0, hh], k)
        last = jnp.where(causal, s[:, n_keys - tq:], -1e30)
        s = last if nblk == 1 else jnp.concatenate([s[:, :n_keys - tq], last], axis=-1)
        pr = jnp.exp(s - jnp.max(s, axis=-1, keepdims=True))
        l = jnp.sum(pr, axis=-1, keepdims=True)
        o_ref[0, hh] = (_dot(pr.astype(BF16), v_ref[0, hh, :n_keys, :]) / l).astype(o_ref.dtype)

    for nblk in range(1, nq + 1):
        pl.when(i == nblk - 1)(functools.partial(attend, nblk))


def _mla_attn(q, kn, kpe, v, tq, hp):
    batch, _, seq, _ = q.shape
    tq = min(tq, seq)
    return pl.pallas_call(
        functools.partial(_mla_attn_kernel, tq=tq, nq=seq // tq),
        grid=(batch, N_HEADS // hp, seq // tq),
        in_specs=[pl.BlockSpec((1, hp, tq, 2 * LANE), lambda b, h, i: (b, h, i, 0)),
                  pl.BlockSpec((1, hp, seq, LANE), lambda b, h, i: (b, h, 0, 0)),
                  pl.BlockSpec((1, seq, LANE), lambda b, h, i: (b, 0, 0)),
                  pl.BlockSpec((1, hp, seq, LANE), lambda b, h, i: (b, h, 0, 0))],
        out_specs=pl.BlockSpec((1, hp, tq, LANE), lambda b, h, i: (b, h, i, 0)),
        out_shape=jax.ShapeDtypeStruct((batch, N_HEADS, seq, LANE), BF16),
        compiler_params=_cparams("parallel", "parallel", "arbitrary"),
        name="mla_attn",
    )(q, kn, kpe, v)


def _merge_kernel(r_ref, s_ref, m_ref, g0_ref, g1_ref, g2_ref, w_ref, o_ref, wb_ref):
    @pl.when(pl.program_id(1) == 0)
    def _():
        wb_ref[...] = w_ref[0].astype(BF16)

    acc = None
    for n, (b_ref, g_ref) in enumerate(((r_ref, g0_ref), (s_ref, g1_ref), (m_ref, g2_ref))):
        a = jnp.concatenate([b_ref[0, h] for h in range(N_HEADS)], axis=-1)
        y = _dot(a, wb_ref[n])
        g = jax.nn.sigmoid(jnp.concatenate([g_ref[0, c].astype(F32) for c in range(g_ref.shape[1])], axis=-1))
        acc = g * y if acc is None else acc + g * y
    o_ref[...] = acc.astype(o_ref.dtype)


def _merge(ret, sb, ml, pg, gate_cb0, w_branch_all, layer, tm, tn):
    batch, _, seq, _ = pg.shape
    d = w_branch_all.shape[-1]
    tm = min(tm, seq)
    spb = seq // tm
    gcb = tn // LANE
    bspec = pl.BlockSpec((1, N_HEADS, tm, LANE), lambda j, i: (i // spb, 0, i % spb, 0))

    def gspec(n):
        first = (gate_cb0 + n * (d // LANE)) // gcb
        return pl.BlockSpec((1, gcb, tm, LANE), lambda j, i: (i // spb, first + j, i % spb, 0))

    return pl.pallas_call(
        _merge_kernel,
        grid=(d // tn, batch * spb),
        in_specs=[bspec, bspec, bspec, gspec(0), gspec(1), gspec(2),
                  pl.BlockSpec((1, N_BRANCH, BRANCH_W, tn), lambda j, i: (layer, 0, 0, j))],
        out_specs=pl.BlockSpec((tm, tn), lambda j, i: (i, j)),
        out_shape=jax.ShapeDtypeStruct((batch * seq, d), BF16),
        scratch_shapes=[pltpu.VMEM((N_BRANCH, BRANCH_W, tn), BF16)],
        compiler_params=_cparams("arbitrary", "arbitrary"),
        name="branch_merge",
    )(ret, sb, ml, pg, pg, pg, w_branch_all)


def _route(logits):
    lane = lax.broadcasted_iota(jnp.int32, logits.shape, 1)
    neg = -jnp.inf
    big = jnp.int32(1 << 20)
    is_g = lane < N_GROUPS
    gl = jnp.where(is_g, logits, neg)
    gm = jnp.max(gl, axis=-1, keepdims=True)
    g_sel = jnp.min(jnp.where(gl == gm, lane, big), axis=-1, keepdims=True)
    g_w = 1.0 / jnp.sum(jnp.where(is_g, jnp.exp(gl - gm), 0.0), axis=-1, keepdims=True)
    lo = N_GROUPS + EXPERTS_PER_GROUP * g_sel
    el = jnp.where((lane >= lo) & (lane < lo + EXPERTS_PER_GROUP), logits, neg)
    t1 = jnp.max(el, axis=-1, keepdims=True)
    i1 = jnp.min(jnp.where(el == t1, lane, big), axis=-1, keepdims=True)
    el2 = jnp.where(lane == i1, neg, el)
    t2 = jnp.max(el2, axis=-1, keepdims=True)
    i2 = jnp.min(jnp.where(el2 == t2, lane, big), axis=-1, keepdims=True)
    dd = jnp.exp(t2 - t1)
    w1 = g_w / (1.0 + dd)
    w2 = g_w * dd / (1.0 + dd)
    e1 = (i1 - N_GROUPS).astype(F32)
    e2 = (i2 - N_GROUPS).astype(F32)
    return jnp.where(lane == 0, e1, jnp.where(lane == 1, e2, jnp.where(lane == 2, w1, jnp.where(lane == 3, w2, 0.0))))


def _pack_halves(x):
    half = x.shape[1] // 2
    hi = lax.bitcast_convert_type(x[:, :half].astype(BF16).astype(F32), jnp.uint32)
    lo = lax.bitcast_convert_type(x[:, half:].astype(BF16).astype(F32), jnp.uint32)
    return hi | (lo >> 16)


def _unpack_halves(w):
    hi = lax.bitcast_convert_type(w & jnp.uint32(0xFFFF0000), F32)
    lo = lax.bitcast_convert_type(w << 16, F32)
    return hi, lo


def _to_token_major(ref, x):
    rows, width = x.shape
    nc = width // LANE
    for c in range(nc):
        ref[pl.ds(c, rows, stride=nc), :] = x[:, c * LANE:(c + 1) * LANE]


def _from_token_major(ref, rows):
    nc = ref.shape[0] // rows
    return [ref[pl.ds(c, rows, stride=nc), :] for c in range(nc)]


def _out_ln_router_kernel(m_ref, w_ref, x_ref, g_ref, b_ref, rw_ref, rb_ref, x1_ref, x1t_ref, r_ref, eid_ref, *,
                          alpha, sub):
    tm, d = x_ref.shape
    nc = d // (2 * LANE)
    rw_hi = rw_ref[:, :LANE]
    for s0 in range(0, tm, sub):
        rows = slice(s0, s0 + sub)
        mix = _dot(m_ref[rows, :], w_ref[...])
        x1 = _layer_norm(alpha * x_ref[rows, :] + mix, g_ref[...], b_ref[...])
        x1_ref[rows, :] = x1
        _to_token_major(x1t_ref.at[pl.ds(s0 * nc, sub * nc), :], _pack_halves(x1))
        xh = x1.astype(BF16)
        xl = (x1 - xh.astype(F32)).astype(BF16)
        both = _dot(xh, rw_ref[...])
        logits = both[:, :LANE] + both[:, LANE:] + _dot(xl, rw_hi) + rb_ref[...]
        route = _route(logits)
        r_ref[rows, :] = route
        eid_ref[:, rows] = jnp.transpose(route)[:SUBLANE].astype(jnp.int32)


def _out_ln_router(merged, w_out, x, ln_g, ln_b, rw, rb, alpha, tm, sub):
    n, d = x.shape
    tm = min(tm, n)
    sub = min(sub, tm)
    row = lambda w: pl.BlockSpec((tm, w), lambda i: (i, 0))
    full = lambda a: pl.BlockSpec(a.shape, lambda i: (0, 0))
    w_spec = pl.BlockSpec(w_out.shape, lambda i: (0, 0), pipeline_mode=pl.Buffered(1))
    return pl.pallas_call(
        functools.partial(_out_ln_router_kernel, alpha=alpha, sub=sub),
        grid=(n // tm,),
        in_specs=[row(d), w_spec, row(d), full(ln_g), full(ln_b), full(rw), full(rb)],
        out_specs=[row(d), pl.BlockSpec((tm * (d // (2 * LANE)), LANE), lambda i: (i, 0)), row(LANE),
                   pl.BlockSpec((SUBLANE, tm), lambda i: (0, i))],
        out_shape=[jax.ShapeDtypeStruct((n, d), F32), jax.ShapeDtypeStruct((n * (d // (2 * LANE)), LANE), jnp.uint32),
                   jax.ShapeDtypeStruct((n, LANE), F32), jax.ShapeDtypeStruct((SUBLANE, n), jnp.int32)],
        compiler_params=_cparams("parallel"),
        name="out_proj_ln_router",
    )(merged, w_out, x, ln_g, ln_b, rw, rb)


def _moe_kernel(bexp_ref, nact_ref, nval_ref, sbase_ref, tok_ref, dst_ref,
                x_hbm, w1_ref, w3_ref, w2_ref, out_hbm,
                xbuf, xbf, ybuf, w1b, w3b, w2b, sem_in, sem_out, *, bm, nc):
    i = pl.program_id(0)
    nact = nact_ref[0]

    @pl.when((i < nact) & ((i == 0) | (bexp_ref[i] != bexp_ref[jnp.maximum(i - 1, 0)])))
    def _():
        w1b[...] = w1_ref[0, 0].astype(BF16)
        w3b[...] = w3_ref[0, 0].astype(BF16)
        w2b[...] = w2_ref[0, 0].astype(BF16)

    def gather(r, base):
        src = pl.multiple_of(tok_ref[base + r], nc)
        return pltpu.make_async_copy(x_hbm.at[pl.ds(src, nc), :], xbuf.at[r // SUBLANE, :, r % SUBLANE, :], sem_in)

    def scatter(r, base):
        dst = pl.multiple_of(dst_ref[base + r], nc)
        return pltpu.make_async_copy(ybuf.at[r // SUBLANE, :, r % SUBLANE, :], out_hbm.at[pl.ds(dst, nc), :], sem_out)

    def wait_rows(n_tok, sem):
        rows = pl.multiple_of(n_tok * nc, nc)

        @pl.when(n_tok > 0)
        def _():
            pltpu.make_async_copy(x_hbm.at[pl.ds(0, rows), :], out_hbm.at[pl.ds(0, rows), :], sem).wait()

    @pl.when(i == 0)
    def _():
        for r in range(bm):
            gather(r, sbase_ref[0]).start(priority=r % 2)

    @pl.when(i < nact)
    def _():
        has_next = i + 1 < nact
        prev = jnp.maximum(i - 1, 0)
        nv_prev = jnp.where(i >= 1, nval_ref[prev], 0)
        nv_head = jnp.minimum(nv_prev, SUBLANE)
        base_prev = sbase_ref[prev]
        base_next = sbase_ref[i + 1]
        for r in range(bm):
            pl.when(r < nv_prev)(functools.partial(scatter(r, base_prev).start, priority=r % 2))
        wait_rows(jnp.int32(bm), sem_in)
        half = nc * LANE
        for c in range(nc):
            hi, lo = _unpack_halves(xbuf[:, c].reshape(bm, LANE))
            xbf[:, c * LANE:(c + 1) * LANE] = hi.astype(BF16)
            xbf[:, half + c * LANE:half + (c + 1) * LANE] = lo.astype(BF16)
        for r in range(bm):
            pl.when(has_next)(functools.partial(gather(r, base_next).start, priority=r % 2))
        wait_rows(nv_head, sem_out)
        xb = xbf[...]
        h1 = _dot(xb, w1b[...])
        h3 = _dot(xb, w3b[...])
        hh = (h1 * jax.nn.sigmoid(h1) * h3).astype(BF16)
        wait_rows(nv_prev - nv_head, sem_out)
        y = _pack_halves(_dot(hh, w2b[...]))
        for c in range(nc):
            ybuf[:, c] = y[:, c * LANE:(c + 1) * LANE].reshape(bm // SUBLANE, SUBLANE, LANE)

    @pl.when(i == nact)
    def _():
        nv = nval_ref[i - 1]
        base = sbase_ref[i - 1]
        for r in range(bm):
            pl.when(r < nv)(functools.partial(scatter(r, base).start, priority=r % 2))
        wait_rows(nv, sem_out)


def _moe_experts(x1t, eid_t, w1, w3, w2, layer, bm):
    _, ne, d, de = w1.shape
    nc = d // (2 * LANE)
    n = x1t.shape[0] // nc
    a = 2 * n
    nb = a // bm + ne
    eid = eid_t[:2].reshape(a)
    order = jnp.sort(eid * a + jnp.arange(a, dtype=jnp.int32)) % a
    experts = jnp.arange(ne, dtype=jnp.int32)
    counts = jnp.sum((eid[None, :] == experts[:, None]).astype(jnp.int32), axis=1)
    padded = (counts + bm - 1) // bm * bm
    upto = experts[None, :] <= experts[:, None]
    start = jnp.sum(jnp.where(upto, counts[None, :], 0), axis=1) - counts
    pend = jnp.sum(jnp.where(upto, padded[None, :], 0), axis=1)
    pstart = pend - padded
    blk = jnp.arange(nb, dtype=jnp.int32)
    blk_start = blk * bm
    n_active = pend[ne - 1] // bm
    active = blk < n_active
    blk_exp = jnp.minimum(jnp.sum((pend[None, :] <= blk_start[:, None]).astype(jnp.int32), axis=1), ne - 1)
    mine = blk_exp[:, None] == experts[None, :]
    pick = lambda v: jnp.sum(jnp.where(mine, v[None, :], 0), axis=1)
    offset = blk_start - pick(pstart)
    blk_valid = jnp.where(active, jnp.clip(pick(counts) - offset, 0, bm), 0).astype(jnp.int32)
    blk_base = jnp.where(active, pick(start) + offset, 0).astype(jnp.int32)
    blk_exp = jnp.where(active, blk_exp, jnp.max(jnp.where(active, blk_exp, 0))).astype(jnp.int32)
    tail = jnp.zeros((bm,), jnp.int32)
    row_tok = jnp.concatenate([jnp.where(order >= n, order - n, order) * nc, tail])
    row_dst = jnp.concatenate([order * nc, tail])

    grid_spec = pltpu.PrefetchScalarGridSpec(
        num_scalar_prefetch=6,
        grid=(nb,),
        in_specs=[pl.BlockSpec(memory_space=pl.ANY),
                  pl.BlockSpec((1, 1, d, de), lambda i, be, *_: (layer, be[i], 0, 0)),
                  pl.BlockSpec((1, 1, d, de), lambda i, be, *_: (layer, be[i], 0, 0)),
                  pl.BlockSpec((1, 1, de, d), lambda i, be, *_: (layer, be[i], 0, 0))],
        out_specs=pl.BlockSpec(memory_space=pl.ANY),
        scratch_shapes=[pltpu.VMEM((bm // SUBLANE, nc, SUBLANE, LANE), jnp.uint32), pltpu.VMEM((bm, d), BF16),
                        pltpu.VMEM((bm // SUBLANE, nc, SUBLANE, LANE), jnp.uint32),
                        pltpu.VMEM((d, de), BF16), pltpu.VMEM((d, de), BF16), pltpu.VMEM((de, d), BF16),
                        pltpu.SemaphoreType.DMA(()), pltpu.SemaphoreType.DMA(())],
    )
    return pl.pallas_call(
        functools.partial(_moe_kernel, bm=bm, nc=nc),
        grid_spec=grid_spec,
        out_shape=jax.ShapeDtypeStruct((a * nc, LANE), jnp.uint32),
        compiler_params=_cparams("arbitrary"),
        name="moe_experts",
    )(blk_exp, n_active.reshape(1).astype(jnp.int32), blk_valid, blk_base, row_tok, row_dst, x1t, w1, w3, w2)


def _combine_ln_kernel(x_ref, y0_ref, y1_ref, r_ref, g_ref, b_ref, o_ref, obf_ref, *, alpha):
    rows = x_ref.shape[0]
    gate0 = r_ref[:, 2:3]
    gate1 = r_ref[:, 3:4]
    his, los = [], []
    for w0, w1 in zip(_from_token_major(y0_ref, rows), _from_token_major(y1_ref, rows)):
        hi0, lo0 = _unpack_halves(w0)
        hi1, lo1 = _unpack_halves(w1)
        his.append(gate0 * hi0 + gate1 * hi1)
        los.append(gate0 * lo0 + gate1 * lo1)
    y = jnp.concatenate(his + los, axis=-1)
    x2 = _layer_norm(alpha * x_ref[...] + y, g_ref[...], b_ref[...])
    o_ref[...] = x2
    obf_ref[...] = x2.astype(BF16)


def _combine_ln(x1, y2, route, ln_g, ln_b, alpha, tm):
    n, d = x1.shape
    tm = min(tm, n)
    steps = n // tm
    nc = d // (2 * LANE)
    return pl.pallas_call(
        functools.partial(_combine_ln_kernel, alpha=alpha),
        grid=(steps,),
        in_specs=[pl.BlockSpec((tm, d), lambda i: (i, 0)), pl.BlockSpec((tm * nc, LANE), lambda i: (i, 0)),
                  pl.BlockSpec((tm * nc, LANE), lambda i: (steps + i, 0)),
                  pl.BlockSpec((tm, LANE), lambda i: (i, 0)),
                  pl.BlockSpec((1, d), lambda i: (0, 0)), pl.BlockSpec((1, d), lambda i: (0, 0))],
        out_specs=[pl.BlockSpec((tm, d), lambda i: (i, 0)), pl.BlockSpec((tm, d), lambda i: (i, 0))],
        out_shape=[jax.ShapeDtypeStruct((n, d), F32), jax.ShapeDtypeStruct((n, d), BF16)],
        compiler_params=_cparams("parallel"),
        name="combine_ln",
    )(x1, y2, y2, route, ln_g, ln_b)


def _rot_half_cols(w):
    half = w.shape[-1] // 2
    return jnp.concatenate([-w[..., half:], w[..., :half]], axis=-1)


def _rope_cos_sin(positions, dim):
    inv = 1.0 / (ROPE_BASE ** (jnp.arange(0, dim, 2, dtype=F32) / dim))
    ang = positions.astype(F32)[..., None] * inv
    return jnp.cos(ang), jnp.sin(ang)


def kernel(x, positions, w_in, mla_q_norm, mla_w_q_b, mla_kv_norm, mla_w_kv_b, w_branch, w_out, ln1_g, ln1_b, router_group_w, router_group_b, router_expert_w, router_expert_b, expert_w1, expert_w3, expert_w2, ln2_g, ln2_b):
    batch, seq, d = x.shape
    depth = w_in.shape[0]
    n = batch * seq
    alpha = (2 * depth) ** 0.25
    n_main = (CB_MQ + 2 * MLA_LORA // LANE) * LANE

    cr, sr = _rope_cos_sin(positions, HEAD_W)
    cos_r = jnp.concatenate([cr, cr], axis=-1)
    sin_r = jnp.concatenate([-sr, sr], axis=-1)
    cm, sm = _rope_cos_sin(positions, MLA_ROPE)
    q_scale = (MLA_NOPE + MLA_ROPE) ** -0.5
    tq_tab = q_scale * jnp.concatenate([jnp.ones((batch, seq, MLA_NOPE), F32), cm, cm, sm, sm], axis=-1)
    tk_tab = jnp.concatenate([cm, cm, sm, sm], axis=-1)

    xf = x.reshape(n, d)
    xb = xf.astype(BF16)
    for l in range(depth):
        w_kr = w_in[l][:, n_main:n_main + MLA_ROPE]
        sq0, sq1 = CB_SQ * LANE, CB_SK * LANE
        w_main = jnp.concatenate([w_in[l][:, :sq0], w_in[l][:, sq0:sq1] * (LOG2E * HEAD_W ** -0.5),
                                  w_in[l][:, sq1:n_main], w_in[l][:, n_main + MLA_ROPE:]], axis=-1).astype(BF16)
        w_kr2 =jnp.concatenate([w_kr, _rot_half_cols(w_kr)], axis=-1).astype(BF16)
        wq = mla_w_q_b[l].reshape(MLA_LORA, N_HEADS, MLA_NOPE + MLA_ROPE)
        wq_pe = wq[..., MLA_NOPE:]
        wq = jnp.concatenate([wq[..., :MLA_NOPE], wq_pe, _rot_half_cols(wq_pe)], axis=-1)
        wq = wq.reshape(MLA_LORA, N_HEADS * 2 * LANE).astype(BF16)
        wkv = mla_w_kv_b[l].astype(BF16)
        rw = jnp.concatenate([router_group_w[l], router_expert_w[l],
                              jnp.zeros((d, LANE - N_GROUPS - N_EXPERTS), F32)], axis=-1)
        rw_hi = rw.astype(BF16)
        rw_cat = jnp.concatenate([rw_hi, (rw - rw_hi.astype(F32)).astype(BF16)], axis=-1)
        rb =jnp.concatenate([router_group_b[l], router_expert_b[l],
                              jnp.zeros((LANE - N_GROUPS - N_EXPERTS,), F32)]).reshape(1, LANE)

        p = _proj(xb, w_main, batch, seq, tm=2048, tn=1024)
        kr = _proj(xb, w_kr2, batch, seq, tm=2048, tn=LANE)
        ret = _retention(p, cos_r, sin_r, rows_per_step=512)
        sb = _stick_breaking(p, tq=256, hp=2)
        q, kn, v, kpe = _mla_prep(p, kr, tq_tab, tk_tab, mla_q_norm[l].reshape(1, -1), mla_kv_norm[l].reshape(1, -1),
                                  wq, wkv, tm=512)
        ml = _mla_attn(q, kn, kpe, v, tq=256, hp=4)
        merged = _merge(ret, sb, ml, p, CB_GATE, w_branch, l, tm=512, tn=1024)
        x1, x1t, route, eid_t = _out_ln_router(merged, w_out[l].astype(BF16), xf, ln1_g[l].reshape(1, d),
                                               ln1_b[l].reshape(1, d), rw_cat, rb, alpha, tm=512, sub=256)
        y2 = _moe_experts(x1t, eid_t, expert_w1, expert_w3, expert_w2, l, MOE_ROWS)
        xf, xb = _combine_ln(x1, y2, route, ln2_g[l].reshape(1, d), ln2_b[l].reshape(1, d), alpha, tm=512)
    return xf.reshape(batch, seq, d)
```

```python
import functools

import numpy as np
import jax
import jax.numpy as jnp
from jax import lax
from jax.experimental import pallas as pl
from jax.experimental.pallas import tpu as pltpu

F32 = jnp.float32
BF16 = jnp.bfloat16

LANE = 128
SUBLANE = 8
N_HEADS = 8
HEAD_W = 128
BRANCH_W = N_HEADS * HEAD_W
N_BRANCH = 3
RET_CHUNK = 128
MLA_LORA = 512
MLA_NOPE = 128
MLA_ROPE = 64
ROPE_BASE = 10000.0
N_GROUPS = 4
EXPERTS_PER_GROUP = 8
N_EXPERTS = N_GROUPS * EXPERTS_PER_GROUP
NORM_EPS = 1e-5
LOG2E = 1.4426950408889634
MOE_ROWS = 256
VMEM_LIMIT = 56 * 1024 * 1024

CB_RQ, CB_RK, CB_RV, CB_RG = 0, 8, 16, 24
CB_SQ, CB_SK, CB_SV = 32, 40, 48
CB_MQ, CB_MKV = 56, 60


def _cparams(*sem):
    return pltpu.CompilerParams(dimension_semantics=sem, vmem_limit_bytes=VMEM_LIMIT)


def _dot(a, b):
    return jnp.dot(a, b, preferred_element_type=F32)


def _dot_nt(a, b):
    return lax.dot_general(a, b, (((1,), (1,)), ((), ())), preferred_element_type=F32)


def _layer_norm(y, g, b):
    mu = jnp.mean(y, axis=-1, keepdims=True)
    d = y - mu
    var = jnp.mean(d * d, axis=-1, keepdims=True)
    return d * lax.rsqrt(var + NORM_EPS) * g + b


def _proj_kernel(x_ref, w_ref, o_ref):
    acc = _dot(x_ref[...], w_ref[...])
    for h in range(o_ref.shape[1]):
        o_ref[0, h] = acc[:, h * LANE:(h + 1) * LANE].astype(o_ref.dtype)


def _proj(x_bf, w, batch, seq, tm, tn):
    n, k = x_bf.shape
    c = w.shape[1]
    tm = min(tm, seq)
    spb = seq // tm
    return pl.pallas_call(
        _proj_kernel,
        grid=(n // tm, c // tn),
        in_specs=[pl.BlockSpec((tm, k), lambda i, j: (i, 0)),
                  pl.BlockSpec((k, tn), lambda i, j: (0, j))],
        out_specs=pl.BlockSpec((1, tn // LANE, tm, LANE), lambda i, j: (i // spb, j, i % spb, 0)),
        out_shape=jax.ShapeDtypeStruct((batch, c // LANE, seq, LANE), BF16),
        compiler_params=_cparams("parallel", "parallel"),
        name="in_proj",
    )(x_bf, w)


def _retention_tables():
    h = np.arange(N_HEADS, dtype=np.float64)
    log_gamma = np.log1p(-np.exp2(-5.0 - h))
    pos = np.arange(RET_CHUNK, dtype=np.float64)
    rel = pos[:, None] - pos[None, :]
    intra = np.where(rel >= 0, np.exp(log_gamma[:, None, None] * np.maximum(rel, 0.0)), 0.0)
    ones = np.ones((1, 1, RET_CHUNK))
    qdec = np.exp(log_gamma[:, None] * (pos + 1.0))[:, :, None] * ones
    kdec = np.exp(log_gamma[:, None] * (RET_CHUNK - 1 - pos))[:, :, None] * ones
    cdec = [float(np.float32(np.exp(lg * RET_CHUNK))) for lg in log_gamma]
    return (jnp.asarray(intra, F32), jnp.asarray(qdec, F32), jnp.asarray(kdec, F32), cdec)


def _retention_kernel(q_ref, k_ref, v_ref, g_ref, cos_ref, sin_ref, intra_ref, qdec_ref, kdec_ref,
                      o_ref, state_ref, *, n_chunk, cdec):
    @pl.when(pl.program_id(1) == 0)
    def _():
        state_ref[...] = jnp.zeros_like(state_ref)

    for h in range(N_HEADS):
        st = state_ref[h]
        for c in range(n_chunk):
            rows = slice(c * RET_CHUNK, (c + 1) * RET_CHUNK)
            cos = cos_ref[0, rows, :]
            sin = sin_ref[0, rows, :]
            q = q_ref[0, h, rows, :].astype(F32)
            k = k_ref[0, h, rows, :].astype(F32)
            q = q * cos + pltpu.roll(q, HEAD_W // 2, 1) * sin
            k = k * cos + pltpu.roll(k, HEAD_W // 2, 1) * sin
            v = v_ref[0, h, rows, :]
            scores = _dot_nt(q.astype(BF16), k.astype(BF16)) * intra_ref[h]
            o = _dot(scores.astype(BF16), v) + _dot((q * qdec_ref[h]).astype(BF16), st.astype(BF16))
            k_dec_t = jnp.transpose(k * kdec_ref[h]).astype(BF16)
            st = st * cdec[h] + _dot(k_dec_t, v)
            o = o * lax.rsqrt(jnp.mean(o * o, axis=-1, keepdims=True) + NORM_EPS)
            g = g_ref[0, h, rows, :].astype(F32)
            o_ref[0, h, rows, :] = (o * (g * jax.nn.sigmoid(g))).astype(o_ref.dtype)
        state_ref[h] = st


def _retention(p, cos_r, sin_r, rows_per_step):
    batch, _, seq, _ = p.shape
    r = min(rows_per_step, seq)
    intra, qdec, kdec, cdec = _retention_tables()

    def pspec(cb):
        return pl.BlockSpec((1, N_HEADS, r, LANE), lambda b, i: (b, cb // N_HEADS, i, 0))

    tspec = pl.BlockSpec((1, r, LANE), lambda b, i: (b, i, 0))
    cspec = pl.BlockSpec((N_HEADS, RET_CHUNK, RET_CHUNK), lambda b, i: (0, 0, 0))
    return pl.pallas_call(
        functools.partial(_retention_kernel, n_chunk=r // RET_CHUNK, cdec=cdec),
        grid=(batch, seq // r),
        in_specs=[pspec(CB_RQ), pspec(CB_RK), pspec(CB_RV), pspec(CB_RG), tspec, tspec, cspec, cspec, cspec],
        out_specs=pl.BlockSpec((1, N_HEADS, r, LANE), lambda b, i: (b, 0, i, 0)),
        out_shape=jax.ShapeDtypeStruct((batch, N_HEADS, seq, LANE), BF16),
        scratch_shapes=[pltpu.VMEM((N_HEADS, HEAD_W, HEAD_W), F32)],
        compiler_params=_cparams("parallel", "arbitrary"),
        name="retention",
    )(p, p, p, p, cos_r, sin_r, intra, qdec, kdec)


def _sb_kernel(q_ref, k_ref, v_ref, u_ref, o_ref, *, tq, nq):
    i = pl.program_id(2)
    u = u_ref[...]
    row = lax.broadcasted_iota(jnp.int32, (tq, tq), 0)
    col = lax.broadcasted_iota(jnp.int32, (tq, tq), 1)
    strict = col < row

    def attend(nblk):
        for hh in range(q_ref.shape[1]):
            attend_head(nblk, hh)

    def attend_head(nblk, hh):
        z = _dot_nt(q_ref[0, hh], k_ref[0, hh, :nblk * tq, :])
        log_keep = jnp.log(1.0 + jnp.exp2(-jnp.abs(z))) * (-LOG2E) - jnp.maximum(z, 0.0)
        c = jnp.zeros((tq, 1), F32)
        ws = [None] * nblk
        for b in reversed(range(nblk)):
            diag = b == nblk - 1
            lk = log_keep[:, b * tq:(b + 1) * tq]
            if diag:
                lk = jnp.where(strict, lk, 0.0)
            w = jnp.exp2(_dot(lk.astype(BF16), u) + (z[:, b * tq:(b + 1) * tq] + c))
            if diag:
                w = jnp.where(strict, w, 0.0)
            ws[b] = w.astype(BF16)
            c = c + jnp.sum(lk, axis=-1, keepdims=True)
        w_all = ws[0] if nblk == 1 else jnp.concatenate(ws, axis=-1)
        o_ref[0, hh] = _dot(w_all, v_ref[0, hh, :nblk * tq, :]).astype(o_ref.dtype)

    for nblk in range(1, nq + 1):
        pl.when(i == nblk - 1)(functools.partial(attend, nblk))


def _stick_breaking(p, tq, hp):
    batch, _, seq, _ = p.shape
    tq = min(tq, seq)
    idx = np.arange(tq)
    u = jnp.asarray(idx[:, None] >= idx[None, :], BF16)
    return pl.pallas_call(
        functools.partial(_sb_kernel, tq=tq, nq=seq // tq),
        grid=(batch, N_HEADS // hp, seq // tq),
        in_specs=[pl.BlockSpec((1, hp, tq, LANE), lambda b, h, i: (b, CB_SQ // hp + h, i, 0)),
                  pl.BlockSpec((1, hp, seq, LANE), lambda b, h, i: (b, CB_SK // hp + h, 0, 0)),
                  pl.BlockSpec((1, hp, seq, LANE), lambda b, h, i: (b, CB_SV // hp + h, 0, 0)),
                  pl.BlockSpec((tq, tq), lambda b, h, i: (0, 0))],
        out_specs=pl.BlockSpec((1, hp, tq, LANE), lambda b, h, i: (b, h, i, 0)),
        out_shape=jax.ShapeDtypeStruct((batch, N_HEADS, seq, LANE), BF16),
        compiler_params=_cparams("parallel", "parallel", "arbitrary"),
        name="stick_breaking",
    )(p, p, p, u)


def _mla_prep_kernel(mq_ref, mkv_ref, kr_ref, tq_ref, tk_ref, qn_ref, kvn_ref, wq_ref, wkv_ref,
                     q_out, kn_out, v_out, kpe_out):
    def rms(ref, g_ref):
        x = jnp.concatenate([ref[0, c].astype(F32) for c in range(MLA_LORA // LANE)], axis=-1)
        y = x * lax.rsqrt(jnp.mean(x * x, axis=-1, keepdims=True) + NORM_EPS)
        return (y * g_ref[...]).astype(BF16)

    qf = _dot(rms(mq_ref, qn_ref), wq_ref[...])
    tq = tq_ref[0]
    for h in range(N_HEADS):
        t = qf[:, h * 2 * LANE:(h + 1) * 2 * LANE] * tq
        u = t[:, LANE:]
        pe = u + pltpu.roll(u, LANE // 2, 1)
        q_out[0, h] = jnp.concatenate([t[:, :LANE], pe], axis=-1).astype(q_out.dtype)
    kv = _dot(rms(mkv_ref, kvn_ref), wkv_ref[...])
    for h in range(N_HEADS):
        kn_out[0, h] = kv[:, h * 2 * LANE:h * 2 * LANE + LANE].astype(kn_out.dtype)
        v_out[0, h] = kv[:, h * 2 * LANE + LANE:(h + 1) * 2 * LANE].astype(v_out.dtype)
    t = kr_ref[0, 0].astype(F32) * tk_ref[0]
    kp = t + pltpu.roll(t, LANE // 2, 1)
    lane = lax.broadcasted_iota(jnp.int32, kp.shape, 1)
    kpe_out[0] = jnp.where(lane < MLA_ROPE, kp, 0.0).astype(kpe_out.dtype)


def _mla_prep(p, kr, tq_tab, tk_tab, q_norm, kv_norm, wq, wkv, tm):
    batch, _, seq, _ = p.shape
    tm = min(tm, seq)
    nl = MLA_LORA // LANE
    head_out = lambda w: pl.BlockSpec((1, N_HEADS, tm, w), lambda b, i: (b, 0, i, 0))
    return pl.pallas_call(
        _mla_prep_kernel,
        grid=(batch, seq // tm),
        in_specs=[pl.BlockSpec((1, nl, tm, LANE), lambda b, i: (b, CB_MQ // nl, i, 0)),
                  pl.BlockSpec((1, nl, tm, LANE), lambda b, i: (b, CB_MKV // nl, i, 0)),
                  pl.BlockSpec((1, 1, tm, LANE), lambda b, i: (b, 0, i, 0)),
                  pl.BlockSpec((1, tm, 2 * LANE), lambda b, i: (b, i, 0)),
                  pl.BlockSpec((1, tm, LANE), lambda b, i: (b, i, 0)),
                  pl.BlockSpec((1, MLA_LORA), lambda b, i: (0, 0)),
                  pl.BlockSpec((1, MLA_LORA), lambda b, i: (0, 0)),
                  pl.BlockSpec(wq.shape, lambda b, i: (0, 0)),
                  pl.BlockSpec(wkv.shape, lambda b, i: (0, 0))],
        out_specs=[head_out(2 * LANE), head_out(LANE), head_out(LANE),
                   pl.BlockSpec((1, tm, LANE), lambda b, i: (b, i, 0))],
        out_shape=[jax.ShapeDtypeStruct((batch, N_HEADS, seq, 2 * LANE), BF16),
                   jax.ShapeDtypeStruct((batch, N_HEADS, seq, LANE), BF16),
                   jax.ShapeDtypeStruct((batch, N_HEADS, seq, LANE), BF16),
                   jax.ShapeDtypeStruct((batch, seq, LANE), BF16)],
        compiler_params=_cparams("parallel", "parallel"),
        name="mla_prep",
    )(p, p, kr, tq_tab, tk_tab, q_norm, kv_norm, wq, wkv)


def _mla_attn_kernel(q_ref, kn_ref, kpe_ref, v_ref, o_ref, *, tq, nq):
    i = pl.program_id(2)
    row = lax.broadcasted_iota(jnp.int32, (tq, tq), 0)
    col = lax.broadcasted_iota(jnp.int32, (tq, tq), 1)
    causal = col <= row

    def attend(nblk):
        for hh in range(q_ref.shape[1]):
            attend_head(nblk, hh)

    def attend_head(nblk, hh):
        n_keys = nblk * tq
        k = jnp.concatenate([kn_ref[0, hh, :n_keys, :], kpe_ref[0, :n_keys, :]], axis=-1)
        s = _dot_nt(q_ref[0, hh], k)
        last = jnp.where(causal, s[:, n_keys - tq:], -1e30)
        s = last if nblk == 1 else jnp.concatenate([s[:, :n_keys - tq], last], axis=-1)
        pr = jnp.exp(s - jnp.max(s, axis=-1, keepdims=True))
        l = jnp.sum(pr, axis=-1, keepdims=True)
        o_ref[0, hh] = (_dot(pr.astype(BF16), v_ref[0, hh, :n_keys, :]) / l).astype(o_ref.dtype)

    for nblk in range(1, nq + 1):
        pl.when(i == nblk - 1)(functools.partial(attend, nblk))


def _mla_attn(q, kn, kpe, v, tq, hp):
    batch, _, seq, _ = q.shape
    tq = min(tq, seq)
    return pl.pallas_call(
        functools.partial(_mla_attn_kernel, tq=tq, nq=seq // tq),
        grid=(batch, N_HEADS // hp, seq // tq),
        in_specs=[pl.BlockSpec((1, hp, tq, 2 * LANE), lambda b, h, i: (b, h, i, 0)),
                  pl.BlockSpec((1, hp, seq, LANE), lambda b, h, i: (b, h, 0, 0)),
                  pl.BlockSpec((1, seq, LANE), lambda b, h, i: (b, 0, 0)),
                  pl.BlockSpec((1, hp, seq, LANE), lambda b, h, i: (b, h, 0, 0))],
        out_specs=pl.BlockSpec((1, hp, tq, LANE), lambda b, h, i: (b, h, i, 0)),
        out_shape=jax.ShapeDtypeStruct((batch, N_HEADS, seq, LANE), BF16),
        compiler_params=_cparams("parallel", "parallel", "arbitrary"),
        name="mla_attn",
    )(q, kn, kpe, v)


def _merge_kernel(r_ref, s_ref, m_ref, g0_ref, g1_ref, g2_ref, w_ref, o_ref, wb_ref):
    @pl.when(pl.program_id(1) == 0)
    def _():
        wb_ref[...] = w_ref[0].astype(BF16)

    acc = None
    for n, (b_ref, g_ref) in enumerate(((r_ref, g0_ref), (s_ref, g1_ref), (m_ref, g2_ref))):
        a = jnp.concatenate([b_ref[0, h] for h in range(N_HEADS)], axis=-1)
        y = _dot(a, wb_ref[n])
        g = jax.nn.sigmoid(jnp.concatenate([g_ref[0, c].astype(F32) for c in range(g_ref.shape[1])], axis=-1))
        acc = g * y if acc is None else acc + g * y
    o_ref[...] = acc.astype(o_ref.dtype)


def _merge(ret, sb, ml, pg, gate_cb0, w_branch_all, layer, tm, tn):
    batch, _, seq, _ = pg.shape
    d = w_branch_all.shape[-1]
    tm = min(tm, seq)
    spb = seq // tm
    gcb = tn // LANE
    bspec = pl.BlockSpec((1, N_HEADS, tm, LANE), lambda j, i: (i // spb, 0, i % spb, 0))

    def gspec(n):
        first = (gate_cb0 + n * (d // LANE)) // gcb
        return pl.BlockSpec((1, gcb, tm, LANE), lambda j, i: (i // spb, first + j, i % spb, 0))

    return pl.pallas_call(
        _merge_kernel,
        grid=(d // tn, batch * spb),
        in_specs=[bspec, bspec, bspec, gspec(0), gspec(1), gspec(2),
                  pl.BlockSpec((1, N_BRANCH, BRANCH_W, tn), lambda j, i: (layer, 0, 0, j))],
        out_specs=pl.BlockSpec((tm, tn), lambda j, i: (i, j)),
        out_shape=jax.ShapeDtypeStruct((batch * seq, d), BF16),
        scratch_shapes=[pltpu.VMEM((N_BRANCH, BRANCH_W, tn), BF16)],
        compiler_params=_cparams("arbitrary", "arbitrary"),
        name="branch_merge",
    )(ret, sb, ml, pg, pg, pg, w_branch_all)


def _route(logits):
    lane = lax.broadcasted_iota(jnp.int32, logits.shape, 1)
    neg = -jnp.inf
    big = jnp.int32(1 << 20)
    is_g = lane < N_GROUPS
    gl = jnp.where(is_g, logits, neg)
    gm = jnp.max(gl, axis=-1, keepdims=True)
    g_sel = jnp.min(jnp.where(gl == gm, lane, big), axis=-1, keepdims=True)
    g_w = 1.0 / jnp.sum(jnp.where(is_g, jnp.exp(gl - gm), 0.0), axis=-1, keepdims=True)
    lo = N_GROUPS + EXPERTS_PER_GROUP * g_sel
    el = jnp.where((lane >= lo) & (lane < lo + EXPERTS_PER_GROUP), logits, neg)
    t1 = jnp.max(el, axis=-1, keepdims=True)
    i1 = jnp.min(jnp.where(el == t1, lane, big), axis=-1, keepdims=True)
    el2 = jnp.where(lane == i1, neg, el)
    t2 = jnp.max(el2, axis=-1, keepdims=True)
    i2 = jnp.min(jnp.where(el2 == t2, lane, big), axis=-1, keepdims=True)
    dd = jnp.exp(t2 - t1)
    w1 = g_w / (1.0 + dd)
    w2 = g_w * dd / (1.0 + dd)
    e1 = (i1 - N_GROUPS).astype(F32)
    e2 = (i2 - N_GROUPS).astype(F32)
    return jnp.where(lane == 0, e1, jnp.where(lane == 1, e2, jnp.where(lane == 2, w1, jnp.where(lane == 3, w2, 0.0))))


def _pack_halves(x):
    half = x.shape[1] // 2
    hi = lax.bitcast_convert_type(x[:, :half].astype(BF16).astype(F32), jnp.uint32)
    lo = lax.bitcast_convert_type(x[:, half:].astype(BF16).astype(F32), jnp.uint32)
    return hi | (lo >> 16)


def _unpack_halves(w):
    hi = lax.bitcast_convert_type(w & jnp.uint32(0xFFFF0000), F32)
    lo = lax.bitcast_convert_type(w << 16, F32)
    return hi, lo


def _to_token_major(ref, x):
    rows, width = x.shape
    nc = width // LANE
    for c in range(nc):
        ref[pl.ds(c, rows, stride=nc), :] = x[:, c * LANE:(c + 1) * LANE]


def _from_token_major(ref, rows):
    nc = ref.shape[0] // rows
    return [ref[pl.ds(c, rows, stride=nc), :] for c in range(nc)]


def _out_ln_router_kernel(m_ref, w_ref, x_ref, g_ref, b_ref, rw_ref, rb_ref, x1_ref, x1t_ref, r_ref, eid_ref, *,
                          alpha, sub):
    tm, d = x_ref.shape
    nc = d // (2 * LANE)
    rw_hi = rw_ref[:, :LANE]
    for s0 in range(0, tm, sub):
        rows = slice(s0, s0 + sub)
        mix = _dot(m_ref[rows, :], w_ref[...])
        x1 = _layer_norm(alpha * x_ref[rows, :] + mix, g_ref[...], b_ref[...])
        x1_ref[rows, :] = x1
        _to_token_major(x1t_ref.at[pl.ds(s0 * nc, sub * nc), :], _pack_halves(x1))
        xh = x1.astype(BF16)
        xl = (x1 - xh.astype(F32)).astype(BF16)
        both = _dot(xh, rw_ref[...])
        logits = both[:, :LANE] + both[:, LANE:] + _dot(xl, rw_hi) + rb_ref[...]
        route = _route(logits)
        r_ref[rows, :] = route
        eid_ref[:, rows] = jnp.transpose(route)[:SUBLANE].astype(jnp.int32)


def _out_ln_router(merged, w_out, x, ln_g, ln_b, rw, rb, alpha, tm, sub):
    n, d = x.shape
    tm = min(tm, n)
    sub = min(sub, tm)
    row = lambda w: pl.BlockSpec((tm, w), lambda i: (i, 0))
    full = lambda a: pl.BlockSpec(a.shape, lambda i: (0, 0))
    w_spec = pl.BlockSpec(w_out.shape, lambda i: (0, 0), pipeline_mode=pl.Buffered(1))
    return pl.pallas_call(
        functools.partial(_out_ln_router_kernel, alpha=alpha, sub=sub),
        grid=(n // tm,),
        in_specs=[row(d), w_spec, row(d), full(ln_g), full(ln_b), full(rw), full(rb)],
        out_specs=[row(d), pl.BlockSpec((tm * (d // (2 * LANE)), LANE), lambda i: (i, 0)), row(LANE),
                   pl.BlockSpec((SUBLANE, tm), lambda i: (0, i))],
        out_shape=[jax.ShapeDtypeStruct((n, d), F32), jax.ShapeDtypeStruct((n * (d // (2 * LANE)), LANE), jnp.uint32),
                   jax.ShapeDtypeStruct((n, LANE), F32), jax.ShapeDtypeStruct((SUBLANE, n), jnp.int32)],
        compiler_params=_cparams("parallel"),
        name="out_proj_ln_router",
    )(merged, w_out, x, ln_g, ln_b, rw, rb)


def _moe_kernel(bexp_ref, nact_ref, nval_ref, sbase_ref, tok_ref, dst_ref,
                x_hbm, w1_ref, w3_ref, w2_ref, out_hbm,
                xbuf, xbf, ybuf, w1b, w3b, w2b, sem_in, sem_out, *, bm, nc):
    i = pl.program_id(0)
    nact = nact_ref[0]

    @pl.when((i < nact) & ((i == 0) | (bexp_ref[i] != bexp_ref[jnp.maximum(i - 1, 0)])))
    def _():
        w1b[...] = w1_ref[0, 0].astype(BF16)
        w3b[...] = w3_ref[0, 0].astype(BF16)
        w2b[...] = w2_ref[0, 0].astype(BF16)

    def gather(r, base):
        src = pl.multiple_of(tok_ref[base + r], nc)
        return pltpu.make_async_copy(x_hbm.at[pl.ds(src, nc), :], xbuf.at[r // SUBLANE, :, r % SUBLANE, :], sem_in)

    def scatter(r, base):
        dst = pl.multiple_of(dst_ref[base + r], nc)
        return pltpu.make_async_copy(ybuf.at[r // SUBLANE, :, r % SUBLANE, :], out_hbm.at[pl.ds(dst, nc), :], sem_out)

    def wait_rows(n_tok, sem):
        rows = pl.multiple_of(n_tok * nc, nc)

        @pl.when(n_tok > 0)
        def _():
            pltpu.make_async_copy(x_hbm.at[pl.ds(0, rows), :], out_hbm.at[pl.ds(0, rows), :], sem).wait()

    @pl.when(i == 0)
    def _():
        for r in range(bm):
            gather(r, sbase_ref[0]).start(priority=r % 2)

    @pl.when(i < nact)
    def _():
        has_next = i + 1 < nact
        prev = jnp.maximum(i - 1, 0)
        nv_prev = jnp.where(i >= 1, nval_ref[prev], 0)
        nv_head = jnp.minimum(nv_prev, SUBLANE)
        base_prev = sbase_ref[prev]
        base_next = sbase_ref[i + 1]
        for r in range(bm):
            pl.when(r < nv_prev)(functools.partial(scatter(r, base_prev).start, priority=r % 2))
        wait_rows(jnp.int32(bm), sem_in)
        half = nc * LANE
        for c in range(nc):
            hi, lo = _unpack_halves(xbuf[:, c].reshape(bm, LANE))
            xbf[:, c * LANE:(c + 1) * LANE] = hi.astype(BF16)
            xbf[:, half + c * LANE:half + (c + 1) * LANE] = lo.astype(BF16)
        for r in range(bm):
            pl.when(has_next)(functools.partial(gather(r, base_next).start, priority=r % 2))
        wait_rows(nv_head, sem_out)
        xb = xbf[...]
        h1 = _dot(xb, w1b[...])
        h3 = _dot(xb, w3b[...])
        hh = (h1 * jax.nn.sigmoid(h1) * h3).astype(BF16)
        wait_rows(nv_prev - nv_head, sem_out)
        y = _pack_halves(_dot(hh, w2b[...]))
        for c in range(nc):
            ybuf[:, c] = y[:, c * LANE:(c + 1) * LANE].reshape(bm // SUBLANE, SUBLANE, LANE)

    @pl.when(i == nact)
    def _():
        nv = nval_ref[i - 1]
        base = sbase_ref[i - 1]
        for r in range(bm):
            pl.when(r < nv)(functools.partial(scatter(r, base).start, priority=r % 2))
        wait_rows(nv, sem_out)


def _moe_experts(x1t, eid_t, w1, w3, w2, layer, bm):
    _, ne, d, de = w1.shape
    nc = d // (2 * LANE)
    n = x1t.shape[0] // nc
    a = 2 * n
    nb = a // bm + ne
    eid = eid_t[:2].reshape(a)
    order = jnp.sort(eid * a + jnp.arange(a, dtype=jnp.int32)) % a
    experts = jnp.arange(ne, dtype=jnp.int32)
    counts = jnp.sum((eid[None, :] == experts[:, None]).astype(jnp.int32), axis=1)
    padded = (counts + bm - 1) // bm * bm
    upto = experts[None, :] <= experts[:, None]
    start = jnp.sum(jnp.where(upto, counts[None, :], 0), axis=1) - counts
    pend = jnp.sum(jnp.where(upto, padded[None, :], 0), axis=1)
    pstart = pend - padded
    blk = jnp.arange(nb, dtype=jnp.int32)
    blk_start = blk * bm
    n_active = pend[ne - 1] // bm
    active = blk < n_active
    blk_exp = jnp.minimum(jnp.sum((pend[None, :] <= blk_start[:, None]).astype(jnp.int32), axis=1), ne - 1)
    mine = blk_exp[:, None] == experts[None, :]
    pick = lambda v: jnp.sum(jnp.where(mine, v[None, :], 0), axis=1)
    offset = blk_start - pick(pstart)
    blk_valid = jnp.where(active, jnp.clip(pick(counts) - offset, 0, bm), 0).astype(jnp.int32)
    blk_base = jnp.where(active, pick(start) + offset, 0).astype(jnp.int32)
    blk_exp = jnp.where(active, blk_exp, jnp.max(jnp.where(active, blk_exp, 0))).astype(jnp.int32)
    tail = jnp.zeros((bm,), jnp.int32)
    row_tok = jnp.concatenate([jnp.where(order >= n, order - n, order) * nc, tail])
    row_dst = jnp.concatenate([order * nc, tail])

    grid_spec = pltpu.PrefetchScalarGridSpec(
        num_scalar_prefetch=6,
        grid=(nb,),
        in_specs=[pl.BlockSpec(memory_space=pl.ANY),
                  pl.BlockSpec((1, 1, d, de), lambda i, be, *_: (layer, be[i], 0, 0)),
                  pl.BlockSpec((1, 1, d, de), lambda i, be, *_: (layer, be[i], 0, 0)),
                  pl.BlockSpec((1, 1, de, d), lambda i, be, *_: (layer, be[i], 0, 0))],
        out_specs=pl.BlockSpec(memory_space=pl.ANY),
        scratch_shapes=[pltpu.VMEM((bm // SUBLANE, nc, SUBLANE, LANE), jnp.uint32), pltpu.VMEM((bm, d), BF16),
                        pltpu.VMEM((bm // SUBLANE, nc, SUBLANE, LANE), jnp.uint32),
                        pltpu.VMEM((d, de), BF16), pltpu.VMEM((d, de), BF16), pltpu.VMEM((de, d), BF16),
                        pltpu.SemaphoreType.DMA(()), pltpu.SemaphoreType.DMA(())],
    )
    return pl.pallas_call(
        functools.partial(_moe_kernel, bm=bm, nc=nc),
        grid_spec=grid_spec,
        out_shape=jax.ShapeDtypeStruct((a * nc, LANE), jnp.uint32),
        compiler_params=_cparams("arbitrary"),
        name="moe_experts",
    )(blk_exp, n_active.reshape(1).astype(jnp.int32), blk_valid, blk_base, row_tok, row_dst, x1t, w1, w3, w2)


def _combine_ln_kernel(x_ref, y0_ref, y1_ref, r_ref, g_ref, b_ref, o_ref, obf_ref, *, alpha):
    rows = x_ref.shape[0]
    gate0 = r_ref[:, 2:3]
    gate1 = r_ref[:, 3:4]
    his, los = [], []
    for w0, w1 in zip(_from_token_major(y0_ref, rows), _from_token_major(y1_ref, rows)):
        hi0, lo0 = _unpack_halves(w0)
        hi1, lo1 = _unpack_halves(w1)
        his.append(gate0 * hi0 + gate1 * hi1)
        los.append(gate0 * lo0 + gate1 * lo1)
    y = jnp.concatenate(his + los, axis=-1)
    x2 = _layer_norm(alpha * x_ref[...] + y, g_ref[...], b_ref[...])
    o_ref[...] = x2
    obf_ref[...] = x2.astype(BF16)


def _combine_ln(x1, y2, route, ln_g, ln_b, alpha, tm):
    n, d = x1.shape
    tm = min(tm, n)
    steps = n // tm
    nc = d // (2 * LANE)
    return pl.pallas_call(
        functools.partial(_combine_ln_kernel, alpha=alpha),
        grid=(steps,),
        in_specs=[pl.BlockSpec((tm, d), lambda i: (i, 0)), pl.BlockSpec((tm * nc, LANE), lambda i: (i, 0)),
                  pl.BlockSpec((tm * nc, LANE), lambda i: (steps + i, 0)),
                  pl.BlockSpec((tm, LANE), lambda i: (i, 0)),
                  pl.BlockSpec((1, d), lambda i: (0, 0)), pl.BlockSpec((1, d), lambda i: (0, 0))],
        out_specs=[pl.BlockSpec((tm, d), lambda i: (i, 0)), pl.BlockSpec((tm, d), lambda i: (i, 0))],
        out_shape=[jax.ShapeDtypeStruct((n, d), F32), jax.ShapeDtypeStruct((n, d), BF16)],
        compiler_params=_cparams("parallel"),
        name="combine_ln",
    )(x1, y2, y2, route, ln_g, ln_b)


def _rot_half_cols(w):
    half = w.shape[-1] // 2
    return jnp.concatenate([-w[..., half:], w[..., :half]], axis=-1)


def _rope_cos_sin(positions, dim):
    inv = 1.0 / (ROPE_BASE ** (jnp.arange(0, dim, 2, dtype=F32) / dim))
    ang = positions.astype(F32)[..., None] * inv
    return jnp.cos(ang), jnp.sin(ang)


def kernel(x, positions, w_in, mla_q_norm, mla_w_q_b, mla_kv_norm, mla_w_kv_b, w_branch, w_out, ln1_g, ln1_b, router_group_w, router_group_b, router_expert_w, router_expert_b, expert_w1, expert_w3, expert_w2, ln2_g, ln2_b):
    batch, seq, d = x.shape
    depth = w_in.shape[0]
    n = batch * seq
    alpha = (2 * depth) ** 0.25
    n_main = (CB_MQ + 2 * MLA_LORA // LANE) * LANE

    cr, sr = _rope_cos_sin(positions, HEAD_W)
    cos_r = jnp.concatenate([cr, cr], axis=-1)
    sin_r = jnp.concatenate([-sr, sr], axis=-1)
    cm, sm = _rope_cos_sin(positions, MLA_ROPE)
    q_scale = (MLA_NOPE + MLA_ROPE) ** -0.5
    tq_tab = q_scale * jnp.concatenate([jnp.ones((batch, seq, MLA_NOPE), F32), cm, cm, sm, sm], axis=-1)
    tk_tab = jnp.concatenate([cm, cm, sm, sm], axis=-1)

    col = jnp.arange(n_main) // LANE
    col_scale = jnp.where((col >= CB_SQ) & (col < CB_SK), LOG2E * HEAD_W ** -0.5,
                          jnp.where((col >= CB_RK) & (col < CB_RV), HEAD_W ** -0.5, 1.0)).astype(F32)[None, :]

    xf = x.reshape(n, d)
    xb = xf.astype(BF16)
    for l in range(depth):
        w_kr = w_in[l][:, n_main:n_main + MLA_ROPE]
        w_main = (w_in[l][:, :n_main] * col_scale).astype(BF16)
        w_gate = w_in[l][:, n_main + MLA_ROPE:].astype(BF16)
        w_kr2 =jnp.concatenate([w_kr, _rot_half_cols(w_kr)], axis=-1).astype(BF16)
        wq = mla_w_q_b[l].reshape(MLA_LORA, N_HEADS, MLA_NOPE + MLA_ROPE)
        wq_pe = wq[..., MLA_NOPE:]
        wq = jnp.concatenate([wq[..., :MLA_NOPE], wq_pe, _rot_half_cols(wq_pe)], axis=-1)
        wq = wq.reshape(MLA_LORA, N_HEADS * 2 * LANE).astype(BF16)
        wkv = mla_w_kv_b[l].astype(BF16)
        rw = jnp.concatenate([router_group_w[l], router_expert_w[l],
                              jnp.zeros((d, LANE - N_GROUPS - N_EXPERTS), F32)], axis=-1)
        rw_hi = rw.astype(BF16)
        rw_cat = jnp.concatenate([rw_hi, (rw - rw_hi.astype(F32)).astype(BF16)], axis=-1)
        rb =jnp.concatenate([router_group_b[l], router_expert_b[l],
                              jnp.zeros((LANE - N_GROUPS - N_EXPERTS,), F32)]).reshape(1, LANE)

        p = _proj(xb, w_main, batch, seq, tm=2048, tn=1024)
        pg = _proj(xb, w_gate, batch, seq, tm=2048, tn=1024)
        kr = _proj(xb, w_kr2, batch, seq, tm=2048, tn=LANE)
        ret = _retention(p, cos_r, sin_r, rows_per_step=512)
        sb = _stick_breaking(p, tq=256, hp=2)
        q, kn, v, kpe = _mla_prep(p, kr, tq_tab, tk_tab, mla_q_norm[l].reshape(1, -1), mla_kv_norm[l].reshape(1, -1),
                                  wq, wkv, tm=512)
        ml = _mla_attn(q, kn, kpe, v, tq=256, hp=4)
        merged = _merge(ret, sb, ml, pg, 0, w_branch, l, tm=512, tn=1024)
        x1, x1t, route, eid_t = _out_ln_router(merged, w_out[l].astype(BF16), xf, ln1_g[l].reshape(1, d),
                                               ln1_b[l].reshape(1, d), rw_cat, rb, alpha, tm=512, sub=256)
        y2 = _moe_experts(x1t, eid_t, expert_w1, expert_w3, expert_w2, l, MOE_ROWS)
        xf, xb = _combine_ln(x1, y2, route, ln2_g[l].reshape(1, d), ln2_b[l].reshape(1, d), alpha, tm=512)
    return xf.reshape(batch, seq, d)
```

```python
import functools

import numpy as np
import jax
import jax.numpy as jnp
from jax import lax
from jax.experimental import pallas as pl
from jax.experimental.pallas import tpu as pltpu

F32 = jnp.float32
BF16 = jnp.bfloat16

LANE = 128
SUBLANE = 8
N_HEADS = 8
HEAD_W = 128
BRANCH_W = N_HEADS * HEAD_W
N_BRANCH = 3
RET_CHUNK = 128
MLA_LORA = 512
MLA_NOPE = 128
MLA_ROPE = 64
ROPE_BASE = 10000.0
N_GROUPS = 4
EXPERTS_PER_GROUP = 8
N_EXPERTS = N_GROUPS * EXPERTS_PER_GROUP
NORM_EPS = 1e-5
LOG2E = 1.4426950408889634
MOE_ROWS = 256
VMEM_LIMIT = 56 * 1024 * 1024

CB_RQ, CB_RK, CB_RV, CB_RG = 0, 8, 16, 24
CB_SQ, CB_SK, CB_SV = 32, 40, 48
CB_MQ, CB_MKV = 56, 60


def _cparams(*sem):
    return pltpu.CompilerParams(dimension_semantics=sem, vmem_limit_bytes=VMEM_LIMIT)


def _dot(a, b):
    return jnp.dot(a, b, preferred_element_type=F32)


def _dot_nt(a, b):
    return lax.dot_general(a, b, (((1,), (1,)), ((), ())), preferred_element_type=F32)


def _layer_norm(y, g, b):
    mu = jnp.mean(y, axis=-1, keepdims=True)
    d = y - mu
    var = jnp.mean(d * d, axis=-1, keepdims=True)
    return d * lax.rsqrt(var + NORM_EPS) * g + b


def _proj_kernel(x_ref, w_ref, o_ref):
    acc = _dot(x_ref[...], w_ref[...])
    for h in range(o_ref.shape[1]):
        o_ref[0, h] = acc[:, h * LANE:(h + 1) * LANE].astype(o_ref.dtype)


def _proj(x_bf, w, batch, seq, tm, tn):
    n, k = x_bf.shape
    c = w.shape[1]
    tm = min(tm, seq)
    spb = seq // tm
    return pl.pallas_call(
        _proj_kernel,
        grid=(n // tm, c // tn),
        in_specs=[pl.BlockSpec((tm, k), lambda i, j: (i, 0)),
                  pl.BlockSpec((k, tn), lambda i, j: (0, j))],
        out_specs=pl.BlockSpec((1, tn // LANE, tm, LANE), lambda i, j: (i // spb, j, i % spb, 0)),
        out_shape=jax.ShapeDtypeStruct((batch, c // LANE, seq, LANE), BF16),
        compiler_params=_cparams("parallel", "parallel"),
        name="in_proj",
    )(x_bf, w)


def _retention_tables():
    h = np.arange(N_HEADS, dtype=np.float64)
    log_gamma = np.log1p(-np.exp2(-5.0 - h))
    pos = np.arange(RET_CHUNK, dtype=np.float64)
    rel = pos[:, None] - pos[None, :]
    intra = np.where(rel >= 0, np.exp(log_gamma[:, None, None] * np.maximum(rel, 0.0)), 0.0)
    ones = np.ones((1, 1, RET_CHUNK))
    qdec = np.exp(log_gamma[:, None] * (pos + 1.0))[:, :, None] * ones
    kdec = np.exp(log_gamma[:, None] * (RET_CHUNK - 1 - pos))[:, :, None] * ones
    cdec = [float(np.float32(np.exp(lg * RET_CHUNK))) for lg in log_gamma]
    return (jnp.asarray(intra, F32), jnp.asarray(qdec, F32), jnp.asarray(kdec, F32), cdec)


def _retention_kernel(q_ref, k_ref, v_ref, g_ref, cos_ref, sin_ref, intra_ref, qdec_ref, kdec_ref,
                      o_ref, state_ref, *, n_chunk, cdec):
    @pl.when(pl.program_id(1) == 0)
    def _():
        state_ref[...] = jnp.zeros_like(state_ref)

    for h in range(N_HEADS):
        st = state_ref[h]
        for c in range(n_chunk):
            rows = slice(c * RET_CHUNK, (c + 1) * RET_CHUNK)
            cos = cos_ref[0, rows, :]
            sin = sin_ref[0, rows, :]
            q = q_ref[0, h, rows, :].astype(F32)
            k = k_ref[0, h, rows, :].astype(F32)
            q = q * cos + pltpu.roll(q, HEAD_W // 2, 1) * sin
            k = k * cos + pltpu.roll(k, HEAD_W // 2, 1) * sin
            v = v_ref[0, h, rows, :]
            scores = _dot_nt(q.astype(BF16), k.astype(BF16)) * intra_ref[h]
            o = _dot(scores.astype(BF16), v) + _dot((q * qdec_ref[h]).astype(BF16), st.astype(BF16))
            k_dec_t = jnp.transpose(k * kdec_ref[h]).astype(BF16)
            st = st * cdec[h] + _dot(k_dec_t, v)
            o = o * lax.rsqrt(jnp.mean(o * o, axis=-1, keepdims=True) + NORM_EPS)
            g = g_ref[0, h, rows, :].astype(F32)
            o_ref[0, h, rows, :] = (o * (g * jax.nn.sigmoid(g))).astype(o_ref.dtype)
        state_ref[h] = st


def _retention(p, cos_r, sin_r, rows_per_step):
    batch, _, seq, _ = p.shape
    r = min(rows_per_step, seq)
    intra, qdec, kdec, cdec = _retention_tables()

    def pspec(cb):
        return pl.BlockSpec((1, N_HEADS, r, LANE), lambda b, i: (b, cb // N_HEADS, i, 0))

    tspec = pl.BlockSpec((1, r, LANE), lambda b, i: (b, i, 0))
    cspec = pl.BlockSpec((N_HEADS, RET_CHUNK, RET_CHUNK), lambda b, i: (0, 0, 0))
    return pl.pallas_call(
        functools.partial(_retention_kernel, n_chunk=r // RET_CHUNK, cdec=cdec),
        grid=(batch, seq // r),
        in_specs=[pspec(CB_RQ), pspec(CB_RK), pspec(CB_RV), pspec(CB_RG), tspec, tspec, cspec, cspec, cspec],
        out_specs=pl.BlockSpec((1, N_HEADS, r, LANE), lambda b, i: (b, 0, i, 0)),
        out_shape=jax.ShapeDtypeStruct((batch, N_HEADS, seq, LANE), BF16),
        scratch_shapes=[pltpu.VMEM((N_HEADS, HEAD_W, HEAD_W), F32)],
        compiler_params=_cparams("parallel", "arbitrary"),
        name="retention",
    )(p, p, p, p, cos_r, sin_r, intra, qdec, kdec)


def _sb_kernel(q_ref, k_ref, v_ref, u_ref, o_ref, *, tq, nq):
    i = pl.program_id(2)
    u = u_ref[...]
    row = lax.broadcasted_iota(jnp.int32, (tq, tq), 0)
    col = lax.broadcasted_iota(jnp.int32, (tq, tq), 1)
    strict = col < row

    def attend(nblk):
        for hh in range(q_ref.shape[1]):
            attend_head(nblk, hh)

    def attend_head(nblk, hh):
        z = _dot_nt(q_ref[0, hh], k_ref[0, hh, :nblk * tq, :])
        log_keep = jnp.log(1.0 + jnp.exp2(-jnp.abs(z))) * (-LOG2E) - jnp.maximum(z, 0.0)
        c = jnp.zeros((tq, 1), F32)
        ws = [None] * nblk
        for b in reversed(range(nblk)):
            diag = b == nblk - 1
            lk = log_keep[:, b * tq:(b + 1) * tq]
            if diag:
                lk = jnp.where(strict, lk, 0.0)
            w = jnp.exp2(_dot(lk.astype(BF16), u) + (z[:, b * tq:(b + 1) * tq] + c))
            if diag:
                w = jnp.where(strict, w, 0.0)
            ws[b] = w.astype(BF16)
            c = c + jnp.sum(lk, axis=-1, keepdims=True)
        w_all = ws[0] if nblk == 1 else jnp.concatenate(ws, axis=-1)
        o_ref[0, hh] = _dot(w_all, v_ref[0, hh, :nblk * tq, :]).astype(o_ref.dtype)

    for nblk in range(1, nq + 1):
        pl.when(i == nblk - 1)(functools.partial(attend, nblk))


def _stick_breaking(p, tq, hp):
    batch, _, seq, _ = p.shape
    tq = min(tq, seq)
    idx = np.arange(tq)
    u = jnp.asarray(idx[:, None] >= idx[None, :], BF16)
    return pl.pallas_call(
        functools.partial(_sb_kernel, tq=tq, nq=seq // tq),
        grid=(batch, N_HEADS // hp, seq // tq),
        in_specs=[pl.BlockSpec((1, hp, tq, LANE), lambda b, h, i: (b, CB_SQ // hp + h, i, 0)),
                  pl.BlockSpec((1, hp, seq, LANE), lambda b, h, i: (b, CB_SK // hp + h, 0, 0)),
                  pl.BlockSpec((1, hp, seq, LANE), lambda b, h, i: (b, CB_SV // hp + h, 0, 0)),
                  pl.BlockSpec((tq, tq), lambda b, h, i: (0, 0))],
        out_specs=pl.BlockSpec((1, hp, tq, LANE), lambda b, h, i: (b, h, i, 0)),
        out_shape=jax.ShapeDtypeStruct((batch, N_HEADS, seq, LANE), BF16),
        compiler_params=_cparams("parallel", "parallel", "arbitrary"),
        name="stick_breaking",
    )(p, p, p, u)


def _mla_prep_kernel(mq_ref, mkv_ref, kr_ref, tq_ref, tk_ref, qn_ref, kvn_ref, wq_ref, wkv_ref,
                     q_out, kn_out, v_out, kpe_out):
    def rms(ref, g_ref):
        x = jnp.concatenate([ref[0, c].astype(F32) for c in range(MLA_LORA // LANE)], axis=-1)
        y = x * lax.rsqrt(jnp.mean(x * x, axis=-1, keepdims=True) + NORM_EPS)
        return (y * g_ref[...]).astype(BF16)

    qf = _dot(rms(mq_ref, qn_ref), wq_ref[...])
    tq = tq_ref[0]
    for h in range(N_HEADS):
        t = qf[:, h * 2 * LANE:(h + 1) * 2 * LANE] * tq
        u = t[:, LANE:]
        pe = u + pltpu.roll(u, LANE // 2, 1)
        q_out[0, h] = jnp.concatenate([t[:, :LANE], pe], axis=-1).astype(q_out.dtype)
    kv = _dot(rms(mkv_ref, kvn_ref), wkv_ref[...])
    for h in range(N_HEADS):
        kn_out[0, h] = kv[:, h * 2 * LANE:h * 2 * LANE + LANE].astype(kn_out.dtype)
        v_out[0, h] = kv[:, h * 2 * LANE + LANE:(h + 1) * 2 * LANE].astype(v_out.dtype)
    kr = kr_ref[0, 0].astype(F32)
    half = MLA_ROPE // 2
    kr_rot = jnp.concatenate([-kr[:, half:MLA_ROPE], kr[:, :half], kr[:, MLA_ROPE:]], axis=-1)
    kpe_out[0] = (kr * tk_ref[0, :, :LANE] + kr_rot * tk_ref[0, :, LANE:]).astype(kpe_out.dtype)


def _mla_prep(p, kr, kr_cb, tq_tab, tk_tab, q_norm, kv_norm, wq, wkv, tm):
    batch, _, seq, _ = p.shape
    tm = min(tm, seq)
    nl = MLA_LORA // LANE
    head_out = lambda w: pl.BlockSpec((1, N_HEADS, tm, w), lambda b, i: (b, 0, i, 0))
    return pl.pallas_call(
        _mla_prep_kernel,
        grid=(batch, seq // tm),
        in_specs=[pl.BlockSpec((1, nl, tm, LANE), lambda b, i: (b, CB_MQ // nl, i, 0)),
                  pl.BlockSpec((1, nl, tm, LANE), lambda b, i: (b, CB_MKV // nl, i, 0)),
                  pl.BlockSpec((1, 1, tm, LANE), lambda b, i: (b, kr_cb, i, 0)),
                  pl.BlockSpec((1, tm, 2 * LANE), lambda b, i: (b, i, 0)),
                  pl.BlockSpec((1, tm, 2 * LANE), lambda b, i: (b, i, 0)),
                  pl.BlockSpec((1, MLA_LORA), lambda b, i: (0, 0)),
                  pl.BlockSpec((1, MLA_LORA), lambda b, i: (0, 0)),
                  pl.BlockSpec(wq.shape, lambda b, i: (0, 0)),
                  pl.BlockSpec(wkv.shape, lambda b, i: (0, 0))],
        out_specs=[head_out(2 * LANE), head_out(LANE), head_out(LANE),
                   pl.BlockSpec((1, tm, LANE), lambda b, i: (b, i, 0))],
        out_shape=[jax.ShapeDtypeStruct((batch, N_HEADS, seq, 2 * LANE), BF16),
                   jax.ShapeDtypeStruct((batch, N_HEADS, seq, LANE), BF16),
                   jax.ShapeDtypeStruct((batch, N_HEADS, seq, LANE), BF16),
                   jax.ShapeDtypeStruct((batch, seq, LANE), BF16)],
        compiler_params=_cparams("parallel", "parallel"),
        name="mla_prep",
    )(p, p, kr, tq_tab, tk_tab, q_norm, kv_norm, wq, wkv)


def _mla_attn_kernel(q_ref, kn_ref, kpe_ref, v_ref, o_ref, *, tq, nq):
    i = pl.program_id(2)
    row = lax.broadcasted_iota(jnp.int32, (tq, tq), 0)
    col = lax.broadcasted_iota(jnp.int32, (tq, tq), 1)
    causal = col <= row

    def attend(nblk):
        for hh in range(q_ref.shape[1]):
            attend_head(nblk, hh)

    def attend_head(nblk, hh):
        n_keys = nblk * tq
        k = jnp.concatenate([kn_ref[0, hh, :n_keys, :], kpe_ref[0, :n_keys, :]], axis=-1)
        s = _dot_nt(q_ref[0, hh], k)
        last = jnp.where(causal, s[:, n_keys - tq:], -1e30)
        s = last if nblk == 1 else jnp.concatenate([s[:, :n_keys - tq], last], axis=-1)
        pr = jnp.exp(s - jnp.max(s, axis=-1, keepdims=True))
        l = jnp.sum(pr, axis=-1, keepdims=True)
        o_ref[0, hh] = (_dot(pr.astype(BF16), v_ref[0, hh, :n_keys, :]) / l).astype(o_ref.dtype)

    for nblk in range(1, nq + 1):
        pl.when(i == nblk - 1)(functools.partial(attend, nblk))


def _mla_attn(q, kn, kpe, v, tq, hp):
    batch, _, seq, _ = q.shape
    tq = min(tq, seq)
    return pl.pallas_call(
        functools.partial(_mla_attn_kernel, tq=tq, nq=seq // tq),
        grid=(batch, N_HEADS // hp, seq // tq),
        in_specs=[pl.BlockSpec((1, hp, tq, 2 * LANE), lambda b, h, i: (b, h, i, 0)),
                  pl.BlockSpec((1, hp, seq, LANE), lambda b, h, i: (b, h, 0, 0)),
                  pl.BlockSpec((1, seq, LANE), lambda b, h, i: (b, 0, 0)),
                  pl.BlockSpec((1, hp, seq, LANE), lambda b, h, i: (b, h, 0, 0))],
        out_specs=pl.BlockSpec((1, hp, tq, LANE), lambda b, h, i: (b, h, i, 0)),
        out_shape=jax.ShapeDtypeStruct((batch, N_HEADS, seq, LANE), BF16),
        compiler_params=_cparams("parallel", "parallel", "arbitrary"),
        name="mla_attn",
    )(q, kn, kpe, v)


def _merge_kernel(r_ref, s_ref, m_ref, g0_ref, g0t_ref, g1_ref, g1t_ref, g2_ref, g2t_ref, w_ref, o_ref, wb_ref, *,
                  lane0):
    @pl.when(pl.program_id(1) == 0)
    def _():
        wb_ref[...] = w_ref[0].astype(BF16)

    tn = o_ref.shape[1]
    acc = None
    for n, (b_ref, g_ref, gt_ref) in enumerate(((r_ref, g0_ref, g0t_ref), (s_ref, g1_ref, g1t_ref),
                                                (m_ref, g2_ref, g2t_ref))):
        a = jnp.concatenate([b_ref[0, h] for h in range(N_HEADS)], axis=-1)
        y = _dot(a, wb_ref[n])
        window = jnp.concatenate([g_ref[0, c].astype(F32) for c in range(g_ref.shape[1])]
                                 + [gt_ref[0, 0].astype(F32)], axis=-1)
        g = jax.nn.sigmoid(window[:, lane0:lane0 + tn])
        acc = g * y if acc is None else acc + g * y
    o_ref[...] = acc.astype(o_ref.dtype)


def _merge(ret, sb, ml, pg, gate_col0, w_branch_all, layer, tm, tn):
    batch, _, seq, _ = pg.shape
    d = w_branch_all.shape[-1]
    tm = min(tm, seq)
    spb = seq // tm
    gcb = tn // LANE
    cb0, lane0 = gate_col0 // LANE, gate_col0 % LANE
    bspec = pl.BlockSpec((1, N_HEADS, tm, LANE), lambda j, i: (i // spb, 0, i % spb, 0))

    def gspecs(n):
        first = cb0 + n * (d // LANE)
        assert first % gcb == 0
        return [pl.BlockSpec((1, gcb, tm, LANE), lambda j, i: (i // spb, first // gcb + j, i % spb, 0)),
                pl.BlockSpec((1, 1, tm, LANE), lambda j, i: (i // spb, first + gcb * (j + 1), i % spb, 0))]

    return pl.pallas_call(
        functools.partial(_merge_kernel, lane0=lane0),
        grid=(d // tn, batch * spb),
        in_specs=[bspec, bspec, bspec] + gspecs(0) + gspecs(1) + gspecs(2)
                 + [pl.BlockSpec((1, N_BRANCH, BRANCH_W, tn), lambda j, i: (layer, 0, 0, j))],
        out_specs=pl.BlockSpec((tm, tn), lambda j, i: (i, j)),
        out_shape=jax.ShapeDtypeStruct((batch * seq, d), BF16),
        scratch_shapes=[pltpu.VMEM((N_BRANCH, BRANCH_W, tn), BF16)],
        compiler_params=_cparams("arbitrary", "arbitrary"),
        name="branch_merge",
    )(ret, sb, ml, pg, pg, pg, pg, pg, pg, w_branch_all)


def _route(logits):
    lane = lax.broadcasted_iota(jnp.int32, logits.shape, 1)
    neg = -jnp.inf
    big = jnp.int32(1 << 20)
    is_g = lane < N_GROUPS
    gl = jnp.where(is_g, logits, neg)
    gm = jnp.max(gl, axis=-1, keepdims=True)
    g_sel = jnp.min(jnp.where(gl == gm, lane, big), axis=-1, keepdims=True)
    g_w = 1.0 / jnp.sum(jnp.where(is_g, jnp.exp(gl - gm), 0.0), axis=-1, keepdims=True)
    lo = N_GROUPS + EXPERTS_PER_GROUP * g_sel
    el = jnp.where((lane >= lo) & (lane < lo + EXPERTS_PER_GROUP), logits, neg)
    t1 = jnp.max(el, axis=-1, keepdims=True)
    i1 = jnp.min(jnp.where(el == t1, lane, big), axis=-1, keepdims=True)
    el2 = jnp.where(lane == i1, neg, el)
    t2 = jnp.max(el2, axis=-1, keepdims=True)
    i2 = jnp.min(jnp.where(el2 == t2, lane, big), axis=-1, keepdims=True)
    dd = jnp.exp(t2 - t1)
    w1 = g_w / (1.0 + dd)
    w2 = g_w * dd / (1.0 + dd)
    e1 = (i1 - N_GROUPS).astype(F32)
    e2 = (i2 - N_GROUPS).astype(F32)
    return jnp.where(lane == 0, e1, jnp.where(lane == 1, e2, jnp.where(lane == 2, w1, jnp.where(lane == 3, w2, 0.0))))


def _pack_halves(x):
    half = x.shape[1] // 2
    hi = lax.bitcast_convert_type(x[:, :half].astype(BF16).astype(F32), jnp.uint32)
    lo = lax.bitcast_convert_type(x[:, half:].astype(BF16).astype(F32), jnp.uint32)
    return hi | (lo >> 16)


def _unpack_halves(w):
    hi = lax.bitcast_convert_type(w & jnp.uint32(0xFFFF0000), F32)
    lo = lax.bitcast_convert_type(w << 16, F32)
    return hi, lo


def _to_token_major(ref, x):
    rows, width = x.shape
    nc = width // LANE
    for c in range(nc):
        ref[pl.ds(c, rows, stride=nc), :] = x[:, c * LANE:(c + 1) * LANE]


def _from_token_major(ref, rows):
    nc = ref.shape[0] // rows
    return [ref[pl.ds(c, rows, stride=nc), :] for c in range(nc)]


def _out_ln_router_kernel(m_ref, w_ref, x_ref, g_ref, b_ref, rw_ref, rb_ref, x1_ref, x1t_ref, r_ref, eid_ref, *,
                          alpha, sub):
    tm, d = x_ref.shape
    nc = d // (2 * LANE)
    rw_hi = rw_ref[:, :LANE]
    for s0 in range(0, tm, sub):
        rows = slice(s0, s0 + sub)
        mix = _dot(m_ref[rows, :], w_ref[...])
        x1 = _layer_norm(alpha * x_ref[rows, :] + mix, g_ref[...], b_ref[...])
        x1_ref[rows, :] = x1
        _to_token_major(x1t_ref.at[pl.ds(s0 * nc, sub * nc), :], _pack_halves(x1))
        xh = x1.astype(BF16)
        xl = (x1 - xh.astype(F32)).astype(BF16)
        both = _dot(xh, rw_ref[...])
        logits = both[:, :LANE] + both[:, LANE:] + _dot(xl, rw_hi) + rb_ref[...]
        route = _route(logits)
        r_ref[rows, :] = route
        eid_ref[:, rows] = jnp.transpose(route)[:SUBLANE].astype(jnp.int32)


def _out_ln_router(merged, w_out, x, ln_g, ln_b, rw, rb, alpha, tm, sub):
    n, d = x.shape
    tm = min(tm, n)
    sub = min(sub, tm)
    row = lambda w: pl.BlockSpec((tm, w), lambda i: (i, 0))
    full = lambda a: pl.BlockSpec(a.shape, lambda i: (0, 0))
    w_spec = pl.BlockSpec(w_out.shape, lambda i: (0, 0), pipeline_mode=pl.Buffered(1))
    return pl.pallas_call(
        functools.partial(_out_ln_router_kernel, alpha=alpha, sub=sub),
        grid=(n // tm,),
        in_specs=[row(d), w_spec, row(d), full(ln_g), full(ln_b), full(rw), full(rb)],
        out_specs=[row(d), pl.BlockSpec((tm * (d // (2 * LANE)), LANE), lambda i: (i, 0)), row(LANE),
                   pl.BlockSpec((SUBLANE, tm), lambda i: (0, i))],
        out_shape=[jax.ShapeDtypeStruct((n, d), F32), jax.ShapeDtypeStruct((n * (d // (2 * LANE)), LANE), jnp.uint32),
                   jax.ShapeDtypeStruct((n, LANE), F32), jax.ShapeDtypeStruct((SUBLANE, n), jnp.int32)],
        compiler_params=_cparams("parallel"),
        name="out_proj_ln_router",
    )(merged, w_out, x, ln_g, ln_b, rw, rb)


def _moe_kernel(bexp_ref, nact_ref, nval_ref, sbase_ref, tok_ref, dst_ref,
                x_hbm, w1_ref, w3_ref, w2_ref, out_hbm,
                xbuf, xbf, ybuf, w1b, w3b, w2b, sem_in, sem_out, *, bm, nc):
    i = pl.program_id(0)
    nact = nact_ref[0]

    @pl.when((i < nact) & ((i == 0) | (bexp_ref[i] != bexp_ref[jnp.maximum(i - 1, 0)])))
    def _():
        w1b[...] = w1_ref[0, 0].astype(BF16)
        w3b[...] = w3_ref[0, 0].astype(BF16)
        w2b[...] = w2_ref[0, 0].astype(BF16)

    def gather(r, base):
        src = pl.multiple_of(tok_ref[base + r], nc)
        return pltpu.make_async_copy(x_hbm.at[pl.ds(src, nc), :], xbuf.at[r // SUBLANE, :, r % SUBLANE, :], sem_in)

    def scatter(r, base):
        dst = pl.multiple_of(dst_ref[base + r], nc)
        return pltpu.make_async_copy(ybuf.at[r // SUBLANE, :, r % SUBLANE, :], out_hbm.at[pl.ds(dst, nc), :], sem_out)

    def wait_rows(n_tok, sem):
        rows = pl.multiple_of(n_tok * nc, nc)

        @pl.when(n_tok > 0)
        def _():
            pltpu.make_async_copy(x_hbm.at[pl.ds(0, rows), :], out_hbm.at[pl.ds(0, rows), :], sem).wait()

    @pl.when(i == 0)
    def _():
        for r in range(bm):
            gather(r, sbase_ref[0]).start(priority=r % 2)

    @pl.when(i < nact)
    def _():
        has_next = i + 1 < nact
        prev = jnp.maximum(i - 1, 0)
        nv_prev = jnp.where(i >= 1, nval_ref[prev], 0)
        nv_head = jnp.minimum(nv_prev, SUBLANE)
        base_prev = sbase_ref[prev]
        base_next = sbase_ref[i + 1]
        for r in range(bm):
            pl.when(r < nv_prev)(functools.partial(scatter(r, base_prev).start, priority=r % 2))
        wait_rows(jnp.int32(bm), sem_in)
        half = nc * LANE
        for c in range(nc):
            hi, lo = _unpack_halves(xbuf[:, c].reshape(bm, LANE))
            xbf[:, c * LANE:(c + 1) * LANE] = hi.astype(BF16)
            xbf[:, half + c * LANE:half + (c + 1) * LANE] = lo.astype(BF16)
        for r in range(bm):
            pl.when(has_next)(functools.partial(gather(r, base_next).start, priority=r % 2))
        wait_rows(nv_head, sem_out)
        xb = xbf[...]
        h1 = _dot(xb, w1b[...])
        h3 = _dot(xb, w3b[...])
        hh = (h1 * jax.nn.sigmoid(h1) * h3).astype(BF16)
        wait_rows(nv_prev - nv_head, sem_out)
        y = _pack_halves(_dot(hh, w2b[...]))
        for c in range(nc):
            ybuf[:, c] = y[:, c * LANE:(c + 1) * LANE].reshape(bm // SUBLANE, SUBLANE, LANE)

    @pl.when(i == nact)
    def _():
        nv = nval_ref[i - 1]
        base = sbase_ref[i - 1]
        for r in range(bm):
            pl.when(r < nv)(functools.partial(scatter(r, base).start, priority=r % 2))
        wait_rows(nv, sem_out)


def _moe_experts(x1t, eid_t, w1, w3, w2, layer, bm):
    _, ne, d, de = w1.shape
    nc = d // (2 * LANE)
    n = x1t.shape[0] // nc
    a = 2 * n
    nb = a // bm + ne
    eid = eid_t[:2].reshape(a)
    order = jnp.sort(eid * a + jnp.arange(a, dtype=jnp.int32)) % a
    experts = jnp.arange(ne, dtype=jnp.int32)
    counts = jnp.sum((eid[None, :] == experts[:, None]).astype(jnp.int32), axis=1)
    padded = (counts + bm - 1) // bm * bm
    upto = experts[None, :] <= experts[:, None]
    start = jnp.sum(jnp.where(upto, counts[None, :], 0), axis=1) - counts
    pend = jnp.sum(jnp.where(upto, padded[None, :], 0), axis=1)
    pstart = pend - padded
    blk = jnp.arange(nb, dtype=jnp.int32)
    blk_start = blk * bm
    n_active = pend[ne - 1] // bm
    active = blk < n_active
    blk_exp = jnp.minimum(jnp.sum((pend[None, :] <= blk_start[:, None]).astype(jnp.int32), axis=1), ne - 1)
    mine = blk_exp[:, None] == experts[None, :]
    pick = lambda v: jnp.sum(jnp.where(mine, v[None, :], 0), axis=1)
    offset = blk_start - pick(pstart)
    blk_valid = jnp.where(active, jnp.clip(pick(counts) - offset, 0, bm), 0).astype(jnp.int32)
    blk_base = jnp.where(active, pick(start) + offset, 0).astype(jnp.int32)
    blk_exp = jnp.where(active, blk_exp, jnp.max(jnp.where(active, blk_exp, 0))).astype(jnp.int32)
    tail = jnp.zeros((bm,), jnp.int32)
    row_tok = jnp.concatenate([jnp.where(order >= n, order - n, order) * nc, tail])
    row_dst = jnp.concatenate([order * nc, tail])

    grid_spec = pltpu.PrefetchScalarGridSpec(
        num_scalar_prefetch=6,
        grid=(nb,),
        in_specs=[pl.BlockSpec(memory_space=pl.ANY),
                  pl.BlockSpec((1, 1, d, de), lambda i, be, *_: (layer, be[i], 0, 0)),
                  pl.BlockSpec((1, 1, d, de), lambda i, be, *_: (layer, be[i], 0, 0)),
                  pl.BlockSpec((1, 1, de, d), lambda i, be, *_: (layer, be[i], 0, 0))],
        out_specs=pl.BlockSpec(memory_space=pl.ANY),
        scratch_shapes=[pltpu.VMEM((bm // SUBLANE, nc, SUBLANE, LANE), jnp.uint32), pltpu.VMEM((bm, d), BF16),
                        pltpu.VMEM((bm // SUBLANE, nc, SUBLANE, LANE), jnp.uint32),
                        pltpu.VMEM((d, de), BF16), pltpu.VMEM((d, de), BF16), pltpu.VMEM((de, d), BF16),
                        pltpu.SemaphoreType.DMA(()), pltpu.SemaphoreType.DMA(())],
    )
    return pl.pallas_call(
        functools.partial(_moe_kernel, bm=bm, nc=nc),
        grid_spec=grid_spec,
        out_shape=jax.ShapeDtypeStruct((a * nc, LANE), jnp.uint32),
        compiler_params=_cparams("arbitrary"),
        name="moe_experts",
    )(blk_exp, n_active.reshape(1).astype(jnp.int32), blk_valid, blk_base, row_tok, row_dst, x1t, w1, w3, w2)


def _combine_ln_kernel(x_ref, y0_ref, y1_ref, r_ref, g_ref, b_ref, o_ref, obf_ref, *, alpha):
    rows = x_ref.shape[0]
    gate0 = r_ref[:, 2:3]
    gate1 = r_ref[:, 3:4]
    his, los = [], []
    for w0, w1 in zip(_from_token_major(y0_ref, rows), _from_token_major(y1_ref, rows)):
        hi0, lo0 = _unpack_halves(w0)
        hi1, lo1 = _unpack_halves(w1)
        his.append(gate0 * hi0 + gate1 * hi1)
        los.append(gate0 * lo0 + gate1 * lo1)
    y = jnp.concatenate(his + los, axis=-1)
    x2 = _layer_norm(alpha * x_ref[...] + y, g_ref[...], b_ref[...])
    o_ref[...] = x2
    obf_ref[...] = x2.astype(BF16)


def _combine_ln(x1, y2, route, ln_g, ln_b, alpha, tm):
    n, d = x1.shape
    tm = min(tm, n)
    steps = n // tm
    nc = d // (2 * LANE)
    return pl.pallas_call(
        functools.partial(_combine_ln_kernel, alpha=alpha),
        grid=(steps,),
        in_specs=[pl.BlockSpec((tm, d), lambda i: (i, 0)), pl.BlockSpec((tm * nc, LANE), lambda i: (i, 0)),
                  pl.BlockSpec((tm * nc, LANE), lambda i: (steps + i, 0)),
                  pl.BlockSpec((tm, LANE), lambda i: (i, 0)),
                  pl.BlockSpec((1, d), lambda i: (0, 0)), pl.BlockSpec((1, d), lambda i: (0, 0))],
        out_specs=[pl.BlockSpec((tm, d), lambda i: (i, 0)), pl.BlockSpec((tm, d), lambda i: (i, 0))],
        out_shape=[jax.ShapeDtypeStruct((n, d), F32), jax.ShapeDtypeStruct((n, d), BF16)],
        compiler_params=_cparams("parallel"),
        name="combine_ln",
    )(x1, y2, y2, route, ln_g, ln_b)


def _rot_half_cols(w):
    half = w.shape[-1] // 2
    return jnp.concatenate([-w[..., half:], w[..., :half]], axis=-1)


def _rope_cos_sin(positions, dim):
    inv = 1.0 / (ROPE_BASE ** (jnp.arange(0, dim, 2, dtype=F32) / dim))
    ang = positions.astype(F32)[..., None] * inv
    return jnp.cos(ang), jnp.sin(ang)


def kernel(x, positions, w_in, mla_q_norm, mla_w_q_b, mla_kv_norm, mla_w_kv_b, w_branch, w_out, ln1_g, ln1_b, router_group_w, router_group_b, router_expert_w, router_expert_b, expert_w1, expert_w3, expert_w2, ln2_g, ln2_b):
    batch, seq, d = x.shape
    depth = w_in.shape[0]
    n = batch * seq
    alpha = (2 * depth) ** 0.25
    n_main = (CB_MQ + 2 * MLA_LORA // LANE) * LANE

    cr, sr = _rope_cos_sin(positions, HEAD_W)
    cos_r = jnp.concatenate([cr, cr], axis=-1)
    sin_r = jnp.concatenate([-sr, sr], axis=-1)
    cm, sm = _rope_cos_sin(positions, MLA_ROPE)
    q_scale = (MLA_NOPE + MLA_ROPE) ** -0.5
    tq_tab = q_scale * jnp.concatenate([jnp.ones((batch, seq, MLA_NOPE), F32), cm, cm, sm, sm], axis=-1)
    pad = jnp.zeros((batch, seq, LANE - MLA_ROPE), F32)
    tk_tab = jnp.concatenate([cm, cm, pad, sm, sm, pad], axis=-1)

    n_tail = w_in.shape[-1] - n_main
    tail_pad = -n_tail % LANE
    tail_blocks = (n_tail + tail_pad) // LANE
    tail_tn = LANE * max(k for k in range(1, 9) if tail_blocks % k == 0)
    col = jnp.arange(n_main) // LANE
    col_scale = jnp.where((col >= CB_SQ) & (col < CB_SK), LOG2E * HEAD_W ** -0.5,
                          jnp.where((col >= CB_RK) & (col < CB_RV), HEAD_W ** -0.5, 1.0)).astype(F32)[None, :]

    xf = x.reshape(n, d)
    xb = xf.astype(BF16)
    for l in range(depth):
        w_main = (w_in[l][:, :n_main] * col_scale).astype(BF16)
        w_tail = jnp.pad(w_in[l][:, n_main:], ((0, 0), (0, tail_pad))).astype(BF16)
        wq =mla_w_q_b[l].reshape(MLA_LORA, N_HEADS, MLA_NOPE + MLA_ROPE)
        wq_pe = wq[..., MLA_NOPE:]
        wq = jnp.concatenate([wq[..., :MLA_NOPE], wq_pe, _rot_half_cols(wq_pe)], axis=-1)
        wq = wq.reshape(MLA_LORA, N_HEADS * 2 * LANE).astype(BF16)
        wkv = mla_w_kv_b[l].astype(BF16)
        rw = jnp.concatenate([router_group_w[l], router_expert_w[l],
                              jnp.zeros((d, LANE - N_GROUPS - N_EXPERTS), F32)], axis=-1)
        rw_hi = rw.astype(BF16)
        rw_cat = jnp.concatenate([rw_hi, (rw - rw_hi.astype(F32)).astype(BF16)], axis=-1)
        rb =jnp.concatenate([router_group_b[l], router_expert_b[l],
                              jnp.zeros((LANE - N_GROUPS - N_EXPERTS,), F32)]).reshape(1, LANE)

        p = _proj(xb, w_main, batch, seq, tm=2048, tn=1024)
        pg = _proj(xb, w_tail, batch, seq, tm=2048, tn=tail_tn)
        ret = _retention(p, cos_r, sin_r, rows_per_step=512)
        sb = _stick_breaking(p, tq=256, hp=2)
        q, kn, v, kpe = _mla_prep(p, pg, 0, tq_tab, tk_tab, mla_q_norm[l].reshape(1, -1),
                                  mla_kv_norm[l].reshape(1, -1), wq, wkv, tm=512)
        ml = _mla_attn(q, kn, kpe, v, tq=256, hp=4)
        merged = _merge(ret, sb, ml, pg, MLA_ROPE, w_branch, l, tm=512, tn=1024)
        x1, x1t, route, eid_t = _out_ln_router(merged, w_out[l].astype(BF16), xf, ln1_g[l].reshape(1, d),
                                               ln1_b[l].reshape(1, d), rw_cat, rb, alpha, tm=512, sub=256)
        y2 = _moe_experts(x1t, eid_t, expert_w1, expert_w3, expert_w2, l, MOE_ROWS)
        xf, xb = _combine_ln(x1, y2, route, ln2_g[l].reshape(1, d), ln2_b[l].reshape(1, d), alpha, tm=512)
    return xf.reshape(batch, seq, d)
```

```python
import functools

import numpy as np
import jax
import jax.numpy as jnp
from jax import lax
from jax.experimental import pallas as pl
from jax.experimental.pallas import tpu as pltpu

F32 = jnp.float32
BF16 = jnp.bfloat16

LANE = 128
SUBLANE = 8
N_HEADS = 8
HEAD_W = 128
BRANCH_W = N_HEADS * HEAD_W
N_BRANCH = 3
RET_CHUNK = 128
MLA_LORA = 512
MLA_NOPE = 128
MLA_ROPE = 64
ROPE_BASE = 10000.0
N_GROUPS = 4
EXPERTS_PER_GROUP = 8
N_EXPERTS = N_GROUPS * EXPERTS_PER_GROUP
NORM_EPS = 1e-5
LOG2E = 1.4426950408889634
MOE_ROWS = 256
VMEM_LIMIT = 56 * 1024 * 1024

CB_RQ, CB_RK, CB_RV, CB_RG = 0, 8, 16, 24
CB_SQ, CB_SK, CB_SV = 32, 40, 48
CB_MQ, CB_MKV = 56, 60


def _cparams(*sem):
    return pltpu.CompilerParams(dimension_semantics=sem, vmem_limit_bytes=VMEM_LIMIT)


def _dot(a, b):
    return jnp.dot(a, b, preferred_element_type=F32)


def _dot_nt(a, b):
    return lax.dot_general(a, b, (((1,), (1,)), ((), ())), preferred_element_type=F32)


def _layer_norm(y, g, b):
    mu = jnp.mean(y, axis=-1, keepdims=True)
    d = y - mu
    var = jnp.mean(d * d, axis=-1, keepdims=True)
    return d * lax.rsqrt(var + NORM_EPS) * g + b


def _proj_kernel(x_ref, w_ref, o_ref):
    acc = _dot(x_ref[...], w_ref[...])
    for h in range(o_ref.shape[1]):
        o_ref[0, h] = acc[:, h * LANE:(h + 1) * LANE].astype(o_ref.dtype)


def _proj(x_bf, w, batch, seq, tm, tn):
    n, k = x_bf.shape
    c = w.shape[1]
    tm = min(tm, seq)
    spb = seq // tm
    return pl.pallas_call(
        _proj_kernel,
        grid=(n // tm, c // tn),
        in_specs=[pl.BlockSpec((tm, k), lambda i, j: (i, 0)),
                  pl.BlockSpec((k, tn), lambda i, j: (0, j))],
        out_specs=pl.BlockSpec((1, tn // LANE, tm, LANE), lambda i, j: (i // spb, j, i % spb, 0)),
        out_shape=jax.ShapeDtypeStruct((batch, c // LANE, seq, LANE), BF16),
        compiler_params=_cparams("parallel", "parallel"),
        name="in_proj",
    )(x_bf, w)


def _retention_tables():
    h = np.arange(N_HEADS, dtype=np.float64)
    log_gamma = np.log1p(-np.exp2(-5.0 - h))
    pos = np.arange(RET_CHUNK, dtype=np.float64)
    rel = pos[:, None] - pos[None, :]
    intra = np.where(rel >= 0, np.exp(log_gamma[:, None, None] * np.maximum(rel, 0.0)), 0.0)
    ones = np.ones((1, 1, RET_CHUNK))
    qdec = np.exp(log_gamma[:, None] * (pos + 1.0))[:, :, None] * ones
    kdec = np.exp(log_gamma[:, None] * (RET_CHUNK - 1 - pos))[:, :, None] * ones
    cdec = [float(np.float32(np.exp(lg * RET_CHUNK))) for lg in log_gamma]
    return (jnp.asarray(intra, F32), jnp.asarray(qdec, F32), jnp.asarray(kdec, F32), cdec)


def _retention_kernel(q_ref, k_ref, v_ref, g_ref, cos_ref, sin_ref, intra_ref, qdec_ref, kdec_ref,
                      o_ref, state_ref, *, n_chunk, cdec):
    @pl.when(pl.program_id(1) == 0)
    def _():
        state_ref[...] = jnp.zeros_like(state_ref)

    for h in range(N_HEADS):
        st = state_ref[h]
        for c in range(n_chunk):
            rows = slice(c * RET_CHUNK, (c + 1) * RET_CHUNK)
            cos = cos_ref[0, rows, :]
            sin = sin_ref[0, rows, :]
            q = q_ref[0, h, rows, :].astype(F32)
            k = k_ref[0, h, rows, :].astype(F32)
            q = q * cos + pltpu.roll(q, HEAD_W // 2, 1) * sin
            k = k * cos + pltpu.roll(k, HEAD_W // 2, 1) * sin
            v = v_ref[0, h, rows, :]
            scores = _dot_nt(q.astype(BF16), k.astype(BF16)) * intra_ref[h]
            o = _dot(scores.astype(BF16), v) + _dot((q * qdec_ref[h]).astype(BF16), st.astype(BF16))
            k_dec_t = jnp.transpose(k * kdec_ref[h]).astype(BF16)
            st = st * cdec[h] + _dot(k_dec_t, v)
            o = o * lax.rsqrt(jnp.mean(o * o, axis=-1, keepdims=True) + NORM_EPS)
            g = g_ref[0, h, rows, :].astype(F32)
            o_ref[0, h, rows, :] = (o * (g * jax.nn.sigmoid(g))).astype(o_ref.dtype)
        state_ref[h] = st


def _retention(p, cos_r, sin_r, rows_per_step):
    batch, _, seq, _ = p.shape
    r = min(rows_per_step, seq)
    intra, qdec, kdec, cdec = _retention_tables()

    def pspec(cb):
        return pl.BlockSpec((1, N_HEADS, r, LANE), lambda b, i: (b, cb // N_HEADS, i, 0))

    tspec = pl.BlockSpec((1, r, LANE), lambda b, i: (b, i, 0))
    cspec = pl.BlockSpec((N_HEADS, RET_CHUNK, RET_CHUNK), lambda b, i: (0, 0, 0))
    return pl.pallas_call(
        functools.partial(_retention_kernel, n_chunk=r // RET_CHUNK, cdec=cdec),
        grid=(batch, seq // r),
        in_specs=[pspec(CB_RQ), pspec(CB_RK), pspec(CB_RV), pspec(CB_RG), tspec, tspec, cspec, cspec, cspec],
        out_specs=pl.BlockSpec((1, N_HEADS, r, LANE), lambda b, i: (b, 0, i, 0)),
        out_shape=jax.ShapeDtypeStruct((batch, N_HEADS, seq, LANE), BF16),
        scratch_shapes=[pltpu.VMEM((N_HEADS, HEAD_W, HEAD_W), F32)],
        compiler_params=_cparams("parallel", "arbitrary"),
        name="retention",
    )(p, p, p, p, cos_r, sin_r, intra, qdec, kdec)


def _sb_kernel(q_ref, k_ref, v_ref, u_ref, o_ref, *, tq, nq):
    i = pl.program_id(2)
    u = u_ref[...]
    row = lax.broadcasted_iota(jnp.int32, (tq, tq), 0)
    col = lax.broadcasted_iota(jnp.int32, (tq, tq), 1)
    strict = col < row

    def attend(nblk):
        for hh in range(q_ref.shape[1]):
            attend_head(nblk, hh)

    def attend_head(nblk, hh):
        z = _dot_nt(q_ref[0, hh], k_ref[0, hh, :nblk * tq, :])
        log_keep = jnp.log(1.0 + jnp.exp2(-jnp.abs(z))) * (-LOG2E) - jnp.maximum(z, 0.0)
        c = jnp.zeros((tq, 1), F32)
        ws = [None] * nblk
        for b in reversed(range(nblk)):
            diag = b == nblk - 1
            lk = log_keep[:, b * tq:(b + 1) * tq]
            if diag:
                lk = jnp.where(strict, lk, 0.0)
            w = jnp.exp2(_dot(lk.astype(BF16), u) + (z[:, b * tq:(b + 1) * tq] + c))
            if diag:
                w = jnp.where(strict, w, 0.0)
            ws[b] = w.astype(BF16)
            c = c + jnp.sum(lk, axis=-1, keepdims=True)
        w_all = ws[0] if nblk == 1 else jnp.concatenate(ws, axis=-1)
        o_ref[0, hh] = _dot(w_all, v_ref[0, hh, :nblk * tq, :]).astype(o_ref.dtype)

    for nblk in range(1, nq + 1):
        pl.when(i == nblk - 1)(functools.partial(attend, nblk))


def _stick_breaking(p, tq, hp):
    batch, _, seq, _ = p.shape
    tq = min(tq, seq)
    idx = np.arange(tq)
    u = jnp.asarray(idx[:, None] >= idx[None, :], BF16)
    return pl.pallas_call(
        functools.partial(_sb_kernel, tq=tq, nq=seq // tq),
        grid=(batch, N_HEADS // hp, seq // tq),
        in_specs=[pl.BlockSpec((1, hp, tq, LANE), lambda b, h, i: (b, CB_SQ // hp + h, i, 0)),
                  pl.BlockSpec((1, hp, seq, LANE), lambda b, h, i: (b, CB_SK // hp + h, 0, 0)),
                  pl.BlockSpec((1, hp, seq, LANE), lambda b, h, i: (b, CB_SV // hp + h, 0, 0)),
                  pl.BlockSpec((tq, tq), lambda b, h, i: (0, 0))],
        out_specs=pl.BlockSpec((1, hp, tq, LANE), lambda b, h, i: (b, h, i, 0)),
        out_shape=jax.ShapeDtypeStruct((batch, N_HEADS, seq, LANE), BF16),
        compiler_params=_cparams("parallel", "parallel", "arbitrary"),
        name="stick_breaking",
    )(p, p, p, u)


def _mla_prep_kernel(mq_ref, mkv_ref, kr_ref, tq_ref, tk_ref, qn_ref, kvn_ref, wq_ref, wkv_ref,
                     q_out, kn_out, v_out, kpe_out):
    def rms(ref, g_ref):
        x = jnp.concatenate([ref[0, c].astype(F32) for c in range(MLA_LORA // LANE)], axis=-1)
        y = x * lax.rsqrt(jnp.mean(x * x, axis=-1, keepdims=True) + NORM_EPS)
        return (y * g_ref[...]).astype(BF16)

    qf = _dot(rms(mq_ref, qn_ref), wq_ref[...])
    tq = tq_ref[0]
    for h in range(N_HEADS):
        t = qf[:, h * 2 * LANE:(h + 1) * 2 * LANE] * tq
        u = t[:, LANE:]
        pe = u + pltpu.roll(u, LANE // 2, 1)
        q_out[0, h] = jnp.concatenate([t[:, :LANE], pe], axis=-1).astype(q_out.dtype)
    kv = _dot(rms(mkv_ref, kvn_ref), wkv_ref[...])
    for h in range(N_HEADS):
        kn_out[0, h] = kv[:, h * 2 * LANE:h * 2 * LANE + LANE].astype(kn_out.dtype)
        v_out[0, h] = kv[:, h * 2 * LANE + LANE:(h + 1) * 2 * LANE].astype(v_out.dtype)
    kr = kr_ref[0, 0].astype(F32)
    half = MLA_ROPE // 2
    kr_rot = jnp.concatenate([-kr[:, half:MLA_ROPE], kr[:, :half], kr[:, MLA_ROPE:]], axis=-1)
    kpe_out[0] = (kr * tk_ref[0, :, :LANE] + kr_rot * tk_ref[0, :, LANE:]).astype(kpe_out.dtype)


def _mla_prep(p, kr, kr_cb, tq_tab, tk_tab, q_norm, kv_norm, wq, wkv, tm):
    batch, _, seq, _ = p.shape
    tm = min(tm, seq)
    nl = MLA_LORA // LANE
    head_out = lambda w: pl.BlockSpec((1, N_HEADS, tm, w), lambda b, i: (b, 0, i, 0))
    return pl.pallas_call(
        _mla_prep_kernel,
        grid=(batch, seq // tm),
        in_specs=[pl.BlockSpec((1, nl, tm, LANE), lambda b, i: (b, CB_MQ // nl, i, 0)),
                  pl.BlockSpec((1, nl, tm, LANE), lambda b, i: (b, CB_MKV // nl, i, 0)),
                  pl.BlockSpec((1, 1, tm, LANE), lambda b, i: (b, kr_cb, i, 0)),
                  pl.BlockSpec((1, tm, 2 * LANE), lambda b, i: (b, i, 0)),
                  pl.BlockSpec((1, tm, 2 * LANE), lambda b, i: (b, i, 0)),
                  pl.BlockSpec((1, MLA_LORA), lambda b, i: (0, 0)),
                  pl.BlockSpec((1, MLA_LORA), lambda b, i: (0, 0)),
                  pl.BlockSpec(wq.shape, lambda b, i: (0, 0)),
                  pl.BlockSpec(wkv.shape, lambda b, i: (0, 0))],
        out_specs=[head_out(2 * LANE), head_out(LANE), head_out(LANE),
                   pl.BlockSpec((1, tm, LANE), lambda b, i: (b, i, 0))],
        out_shape=[jax.ShapeDtypeStruct((batch, N_HEADS, seq, 2 * LANE), BF16),
                   jax.ShapeDtypeStruct((batch, N_HEADS, seq, LANE), BF16),
                   jax.ShapeDtypeStruct((batch, N_HEADS, seq, LANE), BF16),
                   jax.ShapeDtypeStruct((batch, seq, LANE), BF16)],
        compiler_params=_cparams("parallel", "parallel"),
        name="mla_prep",
    )(p, p, kr, tq_tab, tk_tab, q_norm, kv_norm, wq, wkv)


def _mla_attn_kernel(q_ref, kn_ref, kpe_ref, v_ref, o_ref, *, tq, nq):
    i = pl.program_id(2)
    row = lax.broadcasted_iota(jnp.int32, (tq, tq), 0)
    col = lax.broadcasted_iota(jnp.int32, (tq, tq), 1)
    causal = col <= row

    def attend(nblk):
        for hh in range(q_ref.shape[1]):
            attend_head(nblk, hh)

    def attend_head(nblk, hh):
        n_keys = nblk * tq
        k = jnp.concatenate([kn_ref[0, hh, :n_keys, :], kpe_ref[0, :n_keys, :]], axis=-1)
        s = _dot_nt(q_ref[0, hh], k)
        last = jnp.where(causal, s[:, n_keys - tq:], -1e30)
        s = last if nblk == 1 else jnp.concatenate([s[:, :n_keys - tq], last], axis=-1)
        pr = jnp.exp(s - jnp.max(s, axis=-1, keepdims=True))
        l = jnp.sum(pr, axis=-1, keepdims=True)
        o_ref[0, hh] = (_dot(pr.astype(BF16), v_ref[0, hh, :n_keys, :]) / l).astype(o_ref.dtype)

    for nblk in range(1, nq + 1):
        pl.when(i == nblk - 1)(functools.partial(attend, nblk))


def _mla_attn(q, kn, kpe, v, tq, hp):
    batch, _, seq, _ = q.shape
    tq = min(tq, seq)
    return pl.pallas_call(
        functools.partial(_mla_attn_kernel, tq=tq, nq=seq // tq),
        grid=(batch, N_HEADS // hp, seq // tq),
        in_specs=[pl.BlockSpec((1, hp, tq, 2 * LANE), lambda b, h, i: (b, h, i, 0)),
                  pl.BlockSpec((1, hp, seq, LANE), lambda b, h, i: (b, h, 0, 0)),
                  pl.BlockSpec((1, seq, LANE), lambda b, h, i: (b, 0, 0)),
                  pl.BlockSpec((1, hp, seq, LANE), lambda b, h, i: (b, h, 0, 0))],
        out_specs=pl.BlockSpec((1, hp, tq, LANE), lambda b, h, i: (b, h, i, 0)),
        out_shape=jax.ShapeDtypeStruct((batch, N_HEADS, seq, LANE), BF16),
        compiler_params=_cparams("parallel", "parallel", "arbitrary"),
        name="mla_attn",
    )(q, kn, kpe, v)


def _merge_kernel(r_ref, s_ref, m_ref, g0_ref, g0t_ref, g1_ref, g1t_ref, g2_ref, g2t_ref, w_ref, o_ref, wb_ref, *,
                  lane0):
    @pl.when(pl.program_id(1) == 0)
    def _():
        wb_ref[...] = w_ref[0].astype(BF16)

    tn = o_ref.shape[1]
    acc = None
    for n, (b_ref, g_ref, gt_ref) in enumerate(((r_ref, g0_ref, g0t_ref), (s_ref, g1_ref, g1t_ref),
                                                (m_ref, g2_ref, g2t_ref))):
        a = jnp.concatenate([b_ref[0, h] for h in range(N_HEADS)], axis=-1)
        y = _dot(a, wb_ref[n])
        window = jnp.concatenate([g_ref[0, c].astype(F32) for c in range(g_ref.shape[1])]
                                 + [gt_ref[0, 0].astype(F32)], axis=-1)
        g = jax.nn.sigmoid(window[:, lane0:lane0 + tn])
        acc = g * y if acc is None else acc + g * y
    o_ref[...] = acc.astype(o_ref.dtype)


def _merge(ret, sb, ml, pg, gate_col0, w_branch_all, layer, tm, tn):
    batch, _, seq, _ = pg.shape
    d = w_branch_all.shape[-1]
    tm = min(tm, seq)
    spb = seq // tm
    gcb = tn // LANE
    cb0, lane0 = gate_col0 // LANE, gate_col0 % LANE
    bspec = pl.BlockSpec((1, N_HEADS, tm, LANE), lambda j, i: (i // spb, 0, i % spb, 0))

    def gspecs(n):
        first = cb0 + n * (d // LANE)
        assert first % gcb == 0
        return [pl.BlockSpec((1, gcb, tm, LANE), lambda j, i: (i // spb, first // gcb + j, i % spb, 0)),
                pl.BlockSpec((1, 1, tm, LANE), lambda j, i: (i // spb, first + gcb * (j + 1), i % spb, 0))]

    return pl.pallas_call(
        functools.partial(_merge_kernel, lane0=lane0),
        grid=(d // tn, batch * spb),
        in_specs=[bspec, bspec, bspec] + gspecs(0) + gspecs(1) + gspecs(2)
                 + [pl.BlockSpec((1, N_BRANCH, BRANCH_W, tn), lambda j, i: (layer, 0, 0, j))],
        out_specs=pl.BlockSpec((tm, tn), lambda j, i: (i, j)),
        out_shape=jax.ShapeDtypeStruct((batch * seq, d), BF16),
        scratch_shapes=[pltpu.VMEM((N_BRANCH, BRANCH_W, tn), BF16)],
        compiler_params=_cparams("arbitrary", "arbitrary"),
        name="branch_merge",
    )(ret, sb, ml, pg, pg, pg, pg, pg, pg, w_branch_all)


def _route(logits):
    lane = lax.broadcasted_iota(jnp.int32, logits.shape, 1)
    neg = -jnp.inf
    big = jnp.int32(1 << 20)
    is_g = lane < N_GROUPS
    gl = jnp.where(is_g, logits, neg)
    gm = jnp.max(gl, axis=-1, keepdims=True)
    g_sel = jnp.min(jnp.where(gl == gm, lane, big), axis=-1, keepdims=True)
    g_w = 1.0 / jnp.sum(jnp.where(is_g, jnp.exp(gl - gm), 0.0), axis=-1, keepdims=True)
    lo = N_GROUPS + EXPERTS_PER_GROUP * g_sel
    el = jnp.where((lane >= lo) & (lane < lo + EXPERTS_PER_GROUP), logits, neg)
    t1 = jnp.max(el, axis=-1, keepdims=True)
    i1 = jnp.min(jnp.where(el == t1, lane, big), axis=-1, keepdims=True)
    el2 = jnp.where(lane == i1, neg, el)
    t2 = jnp.max(el2, axis=-1, keepdims=True)
    i2 = jnp.min(jnp.where(el2 == t2, lane, big), axis=-1, keepdims=True)
    dd = jnp.exp(t2 - t1)
    w1 = g_w / (1.0 + dd)
    w2 = g_w * dd / (1.0 + dd)
    e1 = (i1 - N_GROUPS).astype(F32)
    e2 = (i2 - N_GROUPS).astype(F32)
    return jnp.where(lane == 0, e1, jnp.where(lane == 1, e2, jnp.where(lane == 2, w1, jnp.where(lane == 3, w2, 0.0))))


def _pack_halves(x):
    half = x.shape[1] // 2
    hi = lax.bitcast_convert_type(x[:, :half].astype(BF16).astype(F32), jnp.uint32)
    lo = lax.bitcast_convert_type(x[:, half:].astype(BF16).astype(F32), jnp.uint32)
    return hi | (lo >> 16)


def _unpack_halves(w):
    hi = lax.bitcast_convert_type(w & jnp.uint32(0xFFFF0000), F32)
    lo = lax.bitcast_convert_type(w << 16, F32)
    return hi, lo


def _to_token_major(ref, x):
    rows, width = x.shape
    nc = width // LANE
    for c in range(nc):
        ref[pl.ds(c, rows, stride=nc), :] = x[:, c * LANE:(c + 1) * LANE]


def _from_token_major(ref, rows):
    nc = ref.shape[0] // rows
    return [ref[pl.ds(c, rows, stride=nc), :] for c in range(nc)]


def _out_ln_router_kernel(m_ref, w_ref, x_ref, g_ref, b_ref, rw_ref, rb_ref, x1_ref, x1t_ref, r_ref, eid_ref, *,
                          alpha, sub):
    tm, d = x_ref.shape
    nc = d // (2 * LANE)
    rw_hi = rw_ref[:, :LANE]
    for s0 in range(0, tm, sub):
        rows = slice(s0, s0 + sub)
        mix = _dot(m_ref[rows, :], w_ref[...])
        x1 = _layer_norm(alpha * x_ref[rows, :] + mix, g_ref[...], b_ref[...])
        x1_ref[rows, :] = x1
        _to_token_major(x1t_ref.at[pl.ds(s0 * nc, sub * nc), :], _pack_halves(x1))
        xh = x1.astype(BF16)
        xl = (x1 - xh.astype(F32)).astype(BF16)
        both = _dot(xh, rw_ref[...])
        logits = both[:, :LANE] + both[:, LANE:] + _dot(xl, rw_hi) + rb_ref[...]
        route = _route(logits)
        r_ref[rows, :] = route
        eid_ref[:, rows] = jnp.transpose(route)[:SUBLANE].astype(jnp.int32)


def _out_ln_router(merged, w_out, x, ln_g, ln_b, rw, rb, alpha, tm, sub):
    n, d = x.shape
    tm = min(tm, n)
    sub = min(sub, tm)
    row = lambda w: pl.BlockSpec((tm, w), lambda i: (i, 0))
    full = lambda a: pl.BlockSpec(a.shape, lambda i: (0, 0))
    w_spec = pl.BlockSpec(w_out.shape, lambda i: (0, 0), pipeline_mode=pl.Buffered(1))
    return pl.pallas_call(
        functools.partial(_out_ln_router_kernel, alpha=alpha, sub=sub),
        grid=(n // tm,),
        in_specs=[row(d), w_spec, row(d), full(ln_g), full(ln_b), full(rw), full(rb)],
        out_specs=[row(d), pl.BlockSpec((tm * (d // (2 * LANE)), LANE), lambda i: (i, 0)), row(LANE),
                   pl.BlockSpec((SUBLANE, tm), lambda i: (0, i))],
        out_shape=[jax.ShapeDtypeStruct((n, d), F32), jax.ShapeDtypeStruct((n * (d // (2 * LANE)), LANE), jnp.uint32),
                   jax.ShapeDtypeStruct((n, LANE), F32), jax.ShapeDtypeStruct((SUBLANE, n), jnp.int32)],
        compiler_params=_cparams("parallel"),
        name="out_proj_ln_router",
    )(merged, w_out, x, ln_g, ln_b, rw, rb)


def _moe_kernel(bexp_ref, nact_ref, nval_ref, sbase_ref, tok_ref, dst_ref,
                x_hbm, w1_ref, w3_ref, w2_ref, out_hbm,
                xbuf, xbf, ybuf, w1b, w3b, w2b, sem_in, sem_out, *, bm, nc):
    i = pl.program_id(0)
    nact = nact_ref[0]

    @pl.when((i < nact) & ((i == 0) | (bexp_ref[i] != bexp_ref[jnp.maximum(i - 1, 0)])))
    def _():
        w1b[...] = w1_ref[0, 0].astype(BF16)
        w3b[...] = w3_ref[0, 0].astype(BF16)
        w2b[...] = w2_ref[0, 0].astype(BF16)

    def gather(r, base):
        src = pl.multiple_of(tok_ref[base + r], nc)
        return pltpu.make_async_copy(x_hbm.at[pl.ds(src, nc), :], xbuf.at[r // SUBLANE, :, r % SUBLANE, :], sem_in)

    def scatter(r, base):
        dst = pl.multiple_of(dst_ref[base + r], nc)
        return pltpu.make_async_copy(ybuf.at[r // SUBLANE, :, r % SUBLANE, :], out_hbm.at[pl.ds(dst, nc), :], sem_out)

    def wait_rows(n_tok, sem):
        rows = pl.multiple_of(n_tok * nc, nc)

        @pl.when(n_tok > 0)
        def _():
            pltpu.make_async_copy(x_hbm.at[pl.ds(0, rows), :], out_hbm.at[pl.ds(0, rows), :], sem).wait()

    @pl.when(i == 0)
    def _():
        for r in range(bm):
            gather(r, sbase_ref[0]).start(priority=r % 2)

    @pl.when(i < nact)
    def _():
        has_next = i + 1 < nact
        prev = jnp.maximum(i - 1, 0)
        nv_prev = jnp.where(i >= 1, nval_ref[prev], 0)
        nv_head = jnp.minimum(nv_prev, SUBLANE)
        base_prev = sbase_ref[prev]
        base_next = sbase_ref[i + 1]
        for r in range(bm):
            pl.when(r < nv_prev)(functools.partial(scatter(r, base_prev).start, priority=r % 2))
        wait_rows(jnp.int32(bm), sem_in)
        half = nc * LANE
        for c in range(nc):
            hi, lo = _unpack_halves(xbuf[:, c].reshape(bm, LANE))
            xbf[:, c * LANE:(c + 1) * LANE] = hi.astype(BF16)
            xbf[:, half + c * LANE:half + (c + 1) * LANE] = lo.astype(BF16)
        for r in range(bm):
            pl.when(has_next)(functools.partial(gather(r, base_next).start, priority=r % 2))
        wait_rows(nv_head, sem_out)
        xb = xbf[...]
        h1 = _dot(xb, w1b[...])
        h3 = _dot(xb, w3b[...])
        hh = (h1 * jax.nn.sigmoid(h1) * h3).astype(BF16)
        wait_rows(nv_prev - nv_head, sem_out)
        y = _pack_halves(_dot(hh, w2b[...]))
        for c in range(nc):
            ybuf[:, c] = y[:, c * LANE:(c + 1) * LANE].reshape(bm // SUBLANE, SUBLANE, LANE)

    @pl.when(i == nact)
    def _():
        nv = nval_ref[i - 1]
        base = sbase_ref[i - 1]
        for r in range(bm):
            pl.when(r < nv)(functools.partial(scatter(r, base).start, priority=r % 2))
        wait_rows(nv, sem_out)


def _moe_experts(x1t, eid_t, w1, w3, w2, layer, bm):
    _, ne, d, de = w1.shape
    nc = d // (2 * LANE)
    n = x1t.shape[0] // nc
    a = 2 * n
    nb = a // bm + ne
    eid = eid_t[:2].reshape(a)
    order = jnp.sort(eid * a + jnp.arange(a, dtype=jnp.int32)) % a
    experts = jnp.arange(ne, dtype=jnp.int32)
    counts = jnp.sum((eid[None, :] == experts[:, None]).astype(jnp.int32), axis=1)
    padded = (counts + bm - 1) // bm * bm
    upto = experts[None, :] <= experts[:, None]
    start = jnp.sum(jnp.where(upto, counts[None, :], 0), axis=1) - counts
    pend = jnp.sum(jnp.where(upto, padded[None, :], 0), axis=1)
    pstart = pend - padded
    blk = jnp.arange(nb, dtype=jnp.int32)
    blk_start = blk * bm
    n_active = pend[ne - 1] // bm
    active = blk < n_active
    blk_exp = jnp.minimum(jnp.sum((pend[None, :] <= blk_start[:, None]).astype(jnp.int32), axis=1), ne - 1)
    mine = blk_exp[:, None] == experts[None, :]
    pick = lambda v: jnp.sum(jnp.where(mine, v[None, :], 0), axis=1)
    offset = blk_start - pick(pstart)
    blk_valid = jnp.where(active, jnp.clip(pick(counts) - offset, 0, bm), 0).astype(jnp.int32)
    blk_base = jnp.where(active, pick(start) + offset, 0).astype(jnp.int32)
    blk_exp = jnp.where(active, blk_exp, jnp.max(jnp.where(active, blk_exp, 0))).astype(jnp.int32)
    tail = jnp.zeros((bm,), jnp.int32)
    row_tok = jnp.concatenate([jnp.where(order >= n, order - n, order) * nc, tail])
    row_dst = jnp.concatenate([order * nc, tail])

    grid_spec = pltpu.PrefetchScalarGridSpec(
        num_scalar_prefetch=6,
        grid=(nb,),
        in_specs=[pl.BlockSpec(memory_space=pl.ANY),
                  pl.BlockSpec((1, 1, d, de), lambda i, be, *_: (layer, be[i], 0, 0)),
                  pl.BlockSpec((1, 1, d, de), lambda i, be, *_: (layer, be[i], 0, 0)),
                  pl.BlockSpec((1, 1, de, d), lambda i, be, *_: (layer, be[i], 0, 0))],
        out_specs=pl.BlockSpec(memory_space=pl.ANY),
        scratch_shapes=[pltpu.VMEM((bm // SUBLANE, nc, SUBLANE, LANE), jnp.uint32), pltpu.VMEM((bm, d), BF16),
                        pltpu.VMEM((bm // SUBLANE, nc, SUBLANE, LANE), jnp.uint32),
                        pltpu.VMEM((d, de), BF16), pltpu.VMEM((d, de), BF16), pltpu.VMEM((de, d), BF16),
                        pltpu.SemaphoreType.DMA(()), pltpu.SemaphoreType.DMA(())],
    )
    return pl.pallas_call(
        functools.partial(_moe_kernel, bm=bm, nc=nc),
        grid_spec=grid_spec,
        out_shape=jax.ShapeDtypeStruct((a * nc, LANE), jnp.uint32),
        compiler_params=_cparams("arbitrary"),
        name="moe_experts",
    )(blk_exp, n_active.reshape(1).astype(jnp.int32), blk_valid, blk_base, row_tok, row_dst, x1t, w1, w3, w2)


def _combine_ln_kernel(x_ref, y0_ref, y1_ref, r_ref, g_ref, b_ref, o_ref, obf_ref, *, alpha):
    rows = x_ref.shape[0]
    gate0 = r_ref[:, 2:3]
    gate1 = r_ref[:, 3:4]
    his, los = [], []
    for w0, w1 in zip(_from_token_major(y0_ref, rows), _from_token_major(y1_ref, rows)):
        hi0, lo0 = _unpack_halves(w0)
        hi1, lo1 = _unpack_halves(w1)
        his.append(gate0 * hi0 + gate1 * hi1)
        los.append(gate0 * lo0 + gate1 * lo1)
    y = jnp.concatenate(his + los, axis=-1)
    x2 = _layer_norm(alpha * x_ref[...] + y, g_ref[...], b_ref[...])
    o_ref[...] = x2
    obf_ref[...] = x2.astype(BF16)


def _combine_ln(x1, y2, route, ln_g, ln_b, alpha, tm):
    n, d = x1.shape
    tm = min(tm, n)
    steps = n // tm
    nc = d // (2 * LANE)
    return pl.pallas_call(
        functools.partial(_combine_ln_kernel, alpha=alpha),
        grid=(steps,),
        in_specs=[pl.BlockSpec((tm, d), lambda i: (i, 0)), pl.BlockSpec((tm * nc, LANE), lambda i: (i, 0)),
                  pl.BlockSpec((tm * nc, LANE), lambda i: (steps + i, 0)),
                  pl.BlockSpec((tm, LANE), lambda i: (i, 0)),
                  pl.BlockSpec((1, d), lambda i: (0, 0)), pl.BlockSpec((1, d), lambda i: (0, 0))],
        out_specs=[pl.BlockSpec((tm, d), lambda i: (i, 0)), pl.BlockSpec((tm, d), lambda i: (i, 0))],
        out_shape=[jax.ShapeDtypeStruct((n, d), F32), jax.ShapeDtypeStruct((n, d), BF16)],
        compiler_params=_cparams("parallel"),
        name="combine_ln",
    )(x1, y2, y2, route, ln_g, ln_b)


def _rot_half_cols(w):
    half = w.shape[-1] // 2
    return jnp.concatenate([-w[..., half:], w[..., :half]], axis=-1)


def _rope_cos_sin(positions, dim):
    inv = 1.0 / (ROPE_BASE ** (jnp.arange(0, dim, 2, dtype=F32) / dim))
    ang = positions.astype(F32)[..., None] * inv
    return jnp.cos(ang), jnp.sin(ang)


def kernel(x, positions, w_in, mla_q_norm, mla_w_q_b, mla_kv_norm, mla_w_kv_b, w_branch, w_out, ln1_g, ln1_b, router_group_w, router_group_b, router_expert_w, router_expert_b, expert_w1, expert_w3, expert_w2, ln2_g, ln2_b):
    batch, seq, d = x.shape
    depth = w_in.shape[0]
    n = batch * seq
    alpha = (2 * depth) ** 0.25
    n_main = (CB_MQ + 2 * MLA_LORA // LANE) * LANE

    cr, sr = _rope_cos_sin(positions, HEAD_W)
    cos_r = jnp.concatenate([cr, cr], axis=-1)
    sin_r = jnp.concatenate([-sr, sr], axis=-1)
    cm, sm = _rope_cos_sin(positions, MLA_ROPE)
    q_scale = (MLA_NOPE + MLA_ROPE) ** -0.5
    tq_tab = q_scale * jnp.concatenate([jnp.ones((batch, seq, MLA_NOPE), F32), cm, cm, sm, sm], axis=-1)
    pad = jnp.zeros((batch, seq, LANE - MLA_ROPE), F32)
    tk_tab = jnp.concatenate([cm, cm, pad, sm, sm, pad], axis=-1)

    n_tail = w_in.shape[-1] - n_main
    tail_pad = -n_tail % LANE
    tail_blocks = (n_tail + tail_pad) // LANE
    tail_tn = LANE * max(k for k in range(1, 9) if tail_blocks % k == 0)
    col = jnp.arange(n_main) // LANE
    col_scale = jnp.where((col >= CB_SQ) & (col < CB_SK), LOG2E * HEAD_W ** -0.5,
                          jnp.where((col >= CB_RK) & (col < CB_RV), HEAD_W ** -0.5, 1.0)).astype(F32)[None, :]

    xf = x.reshape(n, d)
    xb = xf.astype(BF16)
    for l in range(depth):
        w_main = (w_in[l][:, :n_main] * col_scale).astype(BF16)
        w_tail = jnp.pad(w_in[l][:, n_main:], ((0, 0), (0, tail_pad))).astype(BF16)
        wq =mla_w_q_b[l].reshape(MLA_LORA, N_HEADS, MLA_NOPE + MLA_ROPE)
        wq_pe = wq[..., MLA_NOPE:]
        wq = jnp.concatenate([wq[..., :MLA_NOPE], wq_pe, _rot_half_cols(wq_pe)], axis=-1)
        wq = wq.reshape(MLA_LORA, N_HEADS * 2 * LANE).astype(BF16)
        wkv = mla_w_kv_b[l].astype(BF16)
        rw = jnp.concatenate([router_group_w[l], router_expert_w[l],
                              jnp.zeros((d, LANE - N_GROUPS - N_EXPERTS), F32)], axis=-1)
        rw_hi = rw.astype(BF16)
        rw_cat = jnp.concatenate([rw_hi, (rw - rw_hi.astype(F32)).astype(BF16)], axis=-1)
        rb =jnp.concatenate([router_group_b[l], router_expert_b[l],
                              jnp.zeros((LANE - N_GROUPS - N_EXPERTS,), F32)]).reshape(1, LANE)

        p = _proj(xb, w_main, batch, seq, tm=2048, tn=1024)
        pg = _proj(xb, w_tail, batch, seq, tm=2048, tn=tail_tn)
        ret = _retention(p, cos_r, sin_r, rows_per_step=512)
        sb = _stick_breaking(p, tq=256, hp=4)
        q, kn, v, kpe = _mla_prep(p, pg, 0, tq_tab, tk_tab, mla_q_norm[l].reshape(1, -1),
                                  mla_kv_norm[l].reshape(1, -1), wq, wkv, tm=512)
        ml = _mla_attn(q, kn, kpe, v, tq=256, hp=8)
        merged = _merge(ret, sb, ml, pg, MLA_ROPE, w_branch, l, tm=512, tn=1024)
        x1, x1t, route, eid_t = _out_ln_router(merged, w_out[l].astype(BF16), xf, ln1_g[l].reshape(1, d),
                                               ln1_b[l].reshape(1, d), rw_cat, rb, alpha, tm=512, sub=256)
        y2 = _moe_experts(x1t, eid_t, expert_w1, expert_w3, expert_w2, l, MOE_ROWS)
        xf, xb = _combine_ln(x1, y2, route, ln2_g[l].reshape(1, d), ln2_b[l].reshape(1, d), alpha, tm=512)
    return xf.reshape(batch, seq, d)
```

```python
import functools

import numpy as np
import jax
import jax.numpy as jnp
from jax import lax
from jax.experimental import pallas as pl
from jax.experimental.pallas import tpu as pltpu

F32 = jnp.float32
BF16 = jnp.bfloat16

LANE = 128
SUBLANE = 8
N_HEADS = 8
HEAD_W = 128
BRANCH_W = N_HEADS * HEAD_W
N_BRANCH = 3
RET_CHUNK = 128
MLA_LORA = 512
MLA_NOPE = 128
MLA_ROPE = 64
ROPE_BASE = 10000.0
N_GROUPS = 4
EXPERTS_PER_GROUP = 8
N_EXPERTS = N_GROUPS * EXPERTS_PER_GROUP
NORM_EPS = 1e-5
LOG2E = 1.4426950408889634
MOE_ROWS = 256
VMEM_LIMIT = 56 * 1024 * 1024

CB_RQ, CB_RK, CB_RV, CB_RG = 0, 8, 16, 24
CB_SQ, CB_SK, CB_SV = 32, 40, 48
CB_MQ, CB_MKV = 56, 60


def _cparams(*sem):
    return pltpu.CompilerParams(dimension_semantics=sem, vmem_limit_bytes=VMEM_LIMIT)


def _dot(a, b):
    return jnp.dot(a, b, preferred_element_type=F32)


def _dot_nt(a, b):
    return lax.dot_general(a, b, (((1,), (1,)), ((), ())), preferred_element_type=F32)


def _layer_norm(y, g, b):
    mu = jnp.mean(y, axis=-1, keepdims=True)
    d = y - mu
    var = jnp.mean(d * d, axis=-1, keepdims=True)
    return d * lax.rsqrt(var + NORM_EPS) * g + b


def _proj_kernel(x_ref, w_ref, o_ref):
    acc = _dot(x_ref[...], w_ref[...])
    for h in range(o_ref.shape[1]):
        o_ref[0, h] = acc[:, h * LANE:(h + 1) * LANE].astype(o_ref.dtype)


def _proj(x_bf, w, batch, seq, tm, tn):
    n, k = x_bf.shape
    c = w.shape[1]
    tm = min(tm, seq)
    spb = seq // tm
    return pl.pallas_call(
        _proj_kernel,
        grid=(n // tm, c // tn),
        in_specs=[pl.BlockSpec((tm, k), lambda i, j: (i, 0)),
                  pl.BlockSpec((k, tn), lambda i, j: (0, j))],
        out_specs=pl.BlockSpec((1, tn // LANE, tm, LANE), lambda i, j: (i // spb, j, i % spb, 0)),
        out_shape=jax.ShapeDtypeStruct((batch, c // LANE, seq, LANE), BF16),
        compiler_params=_cparams("parallel", "parallel"),
        name="in_proj",
    )(x_bf, w)


def _retention_tables():
    h = np.arange(N_HEADS, dtype=np.float64)
    log_gamma = np.log1p(-np.exp2(-5.0 - h))
    pos = np.arange(RET_CHUNK, dtype=np.float64)
    rel = pos[:, None] - pos[None, :]
    intra = np.where(rel >= 0, np.exp(log_gamma[:, None, None] * np.maximum(rel, 0.0)), 0.0)
    ones = np.ones((1, 1, RET_CHUNK))
    qdec = np.exp(log_gamma[:, None] * (pos + 1.0))[:, :, None] * ones
    kdec = np.exp(log_gamma[:, None] * (RET_CHUNK - 1 - pos))[:, :, None] * ones
    cdec = [float(np.float32(np.exp(lg * RET_CHUNK))) for lg in log_gamma]
    return (jnp.asarray(intra, F32), jnp.asarray(qdec, F32), jnp.asarray(kdec, F32), cdec)


def _retention_kernel(q_ref, k_ref, v_ref, g_ref, cos_ref, sin_ref, intra_ref, qdec_ref, kdec_ref,
                      o_ref, state_ref, *, n_chunk, cdec):
    @pl.when(pl.program_id(1) == 0)
    def _():
        state_ref[...] = jnp.zeros_like(state_ref)

    for h in range(N_HEADS):
        st = state_ref[h]
        for c in range(n_chunk):
            rows = slice(c * RET_CHUNK, (c + 1) * RET_CHUNK)
            cos = cos_ref[0, rows, :]
            sin = sin_ref[0, rows, :]
            q = q_ref[0, h, rows, :].astype(F32)
            k = k_ref[0, h, rows, :].astype(F32)
            q = q * cos + pltpu.roll(q, HEAD_W // 2, 1) * sin
            k = k * cos + pltpu.roll(k, HEAD_W // 2, 1) * sin
            v = v_ref[0, h, rows, :]
            scores = _dot_nt(q.astype(BF16), k.astype(BF16)) * intra_ref[h]
            o = _dot(scores.astype(BF16), v) + _dot((q * qdec_ref[h]).astype(BF16), st.astype(BF16))
            k_dec_t = jnp.transpose(k * kdec_ref[h]).astype(BF16)
            st = st * cdec[h] + _dot(k_dec_t, v)
            o = o * lax.rsqrt(jnp.mean(o * o, axis=-1, keepdims=True) + NORM_EPS)
            g = g_ref[0, h, rows, :].astype(F32)
            o_ref[0, h, rows, :] = (o * (g * jax.nn.sigmoid(g))).astype(o_ref.dtype)
        state_ref[h] = st


def _retention(p, cos_r, sin_r, rows_per_step):
    batch, _, seq, _ = p.shape
    r = min(rows_per_step, seq)
    intra, qdec, kdec, cdec = _retention_tables()

    def pspec(cb):
        return pl.BlockSpec((1, N_HEADS, r, LANE), lambda b, i: (b, cb // N_HEADS, i, 0))

    tspec = pl.BlockSpec((1, r, LANE), lambda b, i: (b, i, 0))
    cspec = pl.BlockSpec((N_HEADS, RET_CHUNK, RET_CHUNK), lambda b, i: (0, 0, 0))
    return pl.pallas_call(
        functools.partial(_retention_kernel, n_chunk=r // RET_CHUNK, cdec=cdec),
        grid=(batch, seq // r),
        in_specs=[pspec(CB_RQ), pspec(CB_RK), pspec(CB_RV), pspec(CB_RG), tspec, tspec, cspec, cspec, cspec],
        out_specs=pl.BlockSpec((1, N_HEADS, r, LANE), lambda b, i: (b, 0, i, 0)),
        out_shape=jax.ShapeDtypeStruct((batch, N_HEADS, seq, LANE), BF16),
        scratch_shapes=[pltpu.VMEM((N_HEADS, HEAD_W, HEAD_W), F32)],
        compiler_params=_cparams("parallel", "arbitrary"),
        name="retention",
    )(p, p, p, p, cos_r, sin_r, intra, qdec, kdec)


def _sb_kernel(q_ref, k_ref, v_ref, u_ref, o_ref, *, tq, nq):
    i = pl.program_id(2)
    u = u_ref[...]
    row = lax.broadcasted_iota(jnp.int32, (tq, tq), 0)
    col = lax.broadcasted_iota(jnp.int32, (tq, tq), 1)
    strict = col < row

    def attend(nblk):
        for hh in range(q_ref.shape[1]):
            attend_head(nblk, hh)

    def attend_head(nblk, hh):
        z = _dot_nt(q_ref[0, hh], k_ref[0, hh, :nblk * tq, :])
        log_keep = jnp.log(1.0 + jnp.exp2(-jnp.abs(z))) * (-LOG2E) - jnp.maximum(z, 0.0)
        c = jnp.zeros((tq, 1), F32)
        ws = [None] * nblk
        for b in reversed(range(nblk)):
            diag = b == nblk - 1
            lk = log_keep[:, b * tq:(b + 1) * tq]
            if diag:
                lk = jnp.where(strict, lk, 0.0)
            w = jnp.exp2(_dot(lk.astype(BF16), u) + (z[:, b * tq:(b + 1) * tq] + c))
            if diag:
                w = jnp.where(strict, w, 0.0)
            ws[b] = w.astype(BF16)
            c = c + jnp.sum(lk, axis=-1, keepdims=True)
        w_all = ws[0] if nblk == 1 else jnp.concatenate(ws, axis=-1)
        o_ref[0, hh] = _dot(w_all, v_ref[0, hh, :nblk * tq, :]).astype(o_ref.dtype)

    for nblk in range(1, nq + 1):
        pl.when(i == nblk - 1)(functools.partial(attend, nblk))


def _stick_breaking(p, tq, hp):
    batch, _, seq, _ = p.shape
    tq = min(tq, seq)
    idx = np.arange(tq)
    u = jnp.asarray(idx[:, None] >= idx[None, :], BF16)
    return pl.pallas_call(
        functools.partial(_sb_kernel, tq=tq, nq=seq // tq),
        grid=(batch, N_HEADS // hp, seq // tq),
        in_specs=[pl.BlockSpec((1, hp, tq, LANE), lambda b, h, i: (b, CB_SQ // hp + h, i, 0)),
                  pl.BlockSpec((1, hp, seq, LANE), lambda b, h, i: (b, CB_SK // hp + h, 0, 0)),
                  pl.BlockSpec((1, hp, seq, LANE), lambda b, h, i: (b, CB_SV // hp + h, 0, 0)),
                  pl.BlockSpec((tq, tq), lambda b, h, i: (0, 0))],
        out_specs=pl.BlockSpec((1, hp, tq, LANE), lambda b, h, i: (b, h, i, 0)),
        out_shape=jax.ShapeDtypeStruct((batch, N_HEADS, seq, LANE), BF16),
        compiler_params=_cparams("parallel", "parallel", "arbitrary"),
        name="stick_breaking",
    )(p, p, p, u)


def _mla_prep_kernel(mq_ref, mkv_ref, kr_ref, tq_ref, tk_ref, qn_ref, kvn_ref, wq_ref, wkv_ref,
                     q_out, kn_out, v_out, kpe_out):
    def rms(ref, g_ref):
        x = jnp.concatenate([ref[0, c].astype(F32) for c in range(MLA_LORA // LANE)], axis=-1)
        y = x * lax.rsqrt(jnp.mean(x * x, axis=-1, keepdims=True) + NORM_EPS)
        return (y * g_ref[...]).astype(BF16)

    qf = _dot(rms(mq_ref, qn_ref), wq_ref[...])
    tq = tq_ref[0]
    for h in range(N_HEADS):
        t = qf[:, h * 2 * LANE:(h + 1) * 2 * LANE] * tq
        u = t[:, LANE:]
        pe = u + pltpu.roll(u, LANE // 2, 1)
        q_out[0, h] = jnp.concatenate([t[:, :LANE], pe], axis=-1).astype(q_out.dtype)
    kv = _dot(rms(mkv_ref, kvn_ref), wkv_ref[...])
    for h in range(N_HEADS):
        kn_out[0, h] = kv[:, h * 2 * LANE:h * 2 * LANE + LANE].astype(kn_out.dtype)
        v_out[0, h] = kv[:, h * 2 * LANE + LANE:(h + 1) * 2 * LANE].astype(v_out.dtype)
    kr = kr_ref[0, 0].astype(F32)
    half = MLA_ROPE // 2
    kr_rot = jnp.concatenate([-kr[:, half:MLA_ROPE], kr[:, :half], kr[:, MLA_ROPE:]], axis=-1)
    kpe_out[0] = (kr * tk_ref[0, :, :LANE] + kr_rot * tk_ref[0, :, LANE:]).astype(kpe_out.dtype)


def _mla_prep(p, kr, kr_cb, tq_tab, tk_tab, q_norm, kv_norm, wq, wkv, tm):
    batch, _, seq, _ = p.shape
    tm = min(tm, seq)
    nl = MLA_LORA // LANE
    head_out = lambda w: pl.BlockSpec((1, N_HEADS, tm, w), lambda b, i: (b, 0, i, 0))
    return pl.pallas_call(
        _mla_prep_kernel,
        grid=(batch, seq // tm),
        in_specs=[pl.BlockSpec((1, nl, tm, LANE), lambda b, i: (b, CB_MQ // nl, i, 0)),
                  pl.BlockSpec((1, nl, tm, LANE), lambda b, i: (b, CB_MKV // nl, i, 0)),
                  pl.BlockSpec((1, 1, tm, LANE), lambda b, i: (b, kr_cb, i, 0)),
                  pl.BlockSpec((1, tm, 2 * LANE), lambda b, i: (b, i, 0)),
                  pl.BlockSpec((1, tm, 2 * LANE), lambda b, i: (b, i, 0)),
                  pl.BlockSpec((1, MLA_LORA), lambda b, i: (0, 0)),
                  pl.BlockSpec((1, MLA_LORA), lambda b, i: (0, 0)),
                  pl.BlockSpec(wq.shape, lambda b, i: (0, 0)),
                  pl.BlockSpec(wkv.shape, lambda b, i: (0, 0))],
        out_specs=[head_out(2 * LANE), head_out(LANE), head_out(LANE),
                   pl.BlockSpec((1, tm, LANE), lambda b, i: (b, i, 0))],
        out_shape=[jax.ShapeDtypeStruct((batch, N_HEADS, seq, 2 * LANE), BF16),
                   jax.ShapeDtypeStruct((batch, N_HEADS, seq, LANE), BF16),
                   jax.ShapeDtypeStruct((batch, N_HEADS, seq, LANE), BF16),
                   jax.ShapeDtypeStruct((batch, seq, LANE), BF16)],
        compiler_params=_cparams("parallel", "parallel"),
        name="mla_prep",
    )(p, p, kr, tq_tab, tk_tab, q_norm, kv_norm, wq, wkv)


def _mla_attn_kernel(q_ref, kn_ref, kpe_ref, v_ref, o_ref, *, tq, nq):
    i = pl.program_id(2)
    row = lax.broadcasted_iota(jnp.int32, (tq, tq), 0)
    col = lax.broadcasted_iota(jnp.int32, (tq, tq), 1)
    causal = col <= row

    def attend(nblk):
        for hh in range(q_ref.shape[1]):
            attend_head(nblk, hh)

    def attend_head(nblk, hh):
        n_keys = nblk * tq
        k = jnp.concatenate([kn_ref[0, hh, :n_keys, :], kpe_ref[0, :n_keys, :]], axis=-1)
        s = _dot_nt(q_ref[0, hh], k)
        last = jnp.where(causal, s[:, n_keys - tq:], -1e30)
        s = last if nblk == 1 else jnp.concatenate([s[:, :n_keys - tq], last], axis=-1)
        pr = jnp.exp(s - jnp.max(s, axis=-1, keepdims=True))
        l = jnp.sum(pr, axis=-1, keepdims=True)
        o_ref[0, hh] = (_dot(pr.astype(BF16), v_ref[0, hh, :n_keys, :]) / l).astype(o_ref.dtype)

    for nblk in range(1, nq + 1):
        pl.when(i == nblk - 1)(functools.partial(attend, nblk))


def _mla_attn(q, kn, kpe, v, tq, hp):
    batch, _, seq, _ = q.shape
    tq = min(tq, seq)
    return pl.pallas_call(
        functools.partial(_mla_attn_kernel, tq=tq, nq=seq // tq),
        grid=(batch, N_HEADS // hp, seq // tq),
        in_specs=[pl.BlockSpec((1, hp, tq, 2 * LANE), lambda b, h, i: (b, h, i, 0)),
                  pl.BlockSpec((1, hp, seq, LANE), lambda b, h, i: (b, h, 0, 0)),
                  pl.BlockSpec((1, seq, LANE), lambda b, h, i: (b, 0, 0)),
                  pl.BlockSpec((1, hp, seq, LANE), lambda b, h, i: (b, h, 0, 0))],
        out_specs=pl.BlockSpec((1, hp, tq, LANE), lambda b, h, i: (b, h, i, 0)),
        out_shape=jax.ShapeDtypeStruct((batch, N_HEADS, seq, LANE), BF16),
        compiler_params=_cparams("parallel", "parallel", "arbitrary"),
        name="mla_attn",
    )(q, kn, kpe, v)


def _merge_kernel(r_ref, s_ref, m_ref, g0_ref, g0t_ref, g1_ref, g1t_ref, g2_ref, g2t_ref, w_ref, o_ref, wb_ref, *,
                  lane0):
    @pl.when(pl.program_id(1) == 0)
    def _():
        wb_ref[...] = w_ref[0].astype(BF16)

    tn = o_ref.shape[1]
    acc = None
    for n, (b_ref, g_ref, gt_ref) in enumerate(((r_ref, g0_ref, g0t_ref), (s_ref, g1_ref, g1t_ref),
                                                (m_ref, g2_ref, g2t_ref))):
        a = jnp.concatenate([b_ref[0, h] for h in range(N_HEADS)], axis=-1)
        y = _dot(a, wb_ref[n])
        window = jnp.concatenate([g_ref[0, c].astype(F32) for c in range(g_ref.shape[1])]
                                 + [gt_ref[0, 0].astype(F32)], axis=-1)
        g = jax.nn.sigmoid(window[:, lane0:lane0 + tn])
        acc = g * y if acc is None else acc + g * y
    o_ref[...] = acc.astype(o_ref.dtype)


def _merge(ret, sb, ml, pg, gate_col0, w_branch_all, layer, tm, tn):
    batch, _, seq, _ = pg.shape
    d = w_branch_all.shape[-1]
    tm = min(tm, seq)
    spb = seq // tm
    gcb = tn // LANE
    cb0, lane0 = gate_col0 // LANE, gate_col0 % LANE
    bspec = pl.BlockSpec((1, N_HEADS, tm, LANE), lambda j, i: (i // spb, 0, i % spb, 0))

    def gspecs(n):
        first = cb0 + n * (d // LANE)
        assert first % gcb == 0
        return [pl.BlockSpec((1, gcb, tm, LANE), lambda j, i: (i // spb, first // gcb + j, i % spb, 0)),
                pl.BlockSpec((1, 1, tm, LANE), lambda j, i: (i // spb, first + gcb * (j + 1), i % spb, 0))]

    return pl.pallas_call(
        functools.partial(_merge_kernel, lane0=lane0),
        grid=(d // tn, batch * spb),
        in_specs=[bspec, bspec, bspec] + gspecs(0) + gspecs(1) + gspecs(2)
                 + [pl.BlockSpec((1, N_BRANCH, BRANCH_W, tn), lambda j, i: (layer, 0, 0, j))],
        out_specs=pl.BlockSpec((tm, tn), lambda j, i: (i, j)),
        out_shape=jax.ShapeDtypeStruct((batch * seq, d), BF16),
        scratch_shapes=[pltpu.VMEM((N_BRANCH, BRANCH_W, tn), BF16)],
        compiler_params=_cparams("arbitrary", "arbitrary"),
        name="branch_merge",
    )(ret, sb, ml, pg, pg, pg, pg, pg, pg, w_branch_all)


def _route(logits):
    lane = lax.broadcasted_iota(jnp.int32, logits.shape, 1)
    neg = -jnp.inf
    big = jnp.int32(1 << 20)
    is_g = lane < N_GROUPS
    gl = jnp.where(is_g, logits, neg)
    gm = jnp.max(gl, axis=-1, keepdims=True)
    g_sel = jnp.min(jnp.where(gl == gm, lane, big), axis=-1, keepdims=True)
    g_w = 1.0 / jnp.sum(jnp.where(is_g, jnp.exp(gl - gm), 0.0), axis=-1, keepdims=True)
    lo = N_GROUPS + EXPERTS_PER_GROUP * g_sel
    el = jnp.where((lane >= lo) & (lane < lo + EXPERTS_PER_GROUP), logits, neg)
    t1 = jnp.max(el, axis=-1, keepdims=True)
    i1 = jnp.min(jnp.where(el == t1, lane, big), axis=-1, keepdims=True)
    el2 = jnp.where(lane == i1, neg, el)
    t2 = jnp.max(el2, axis=-1, keepdims=True)
    i2 = jnp.min(jnp.where(el2 == t2, lane, big), axis=-1, keepdims=True)
    dd = jnp.exp(t2 - t1)
    w1 = g_w / (1.0 + dd)
    w2 = g_w * dd / (1.0 + dd)
    e1 = (i1 - N_GROUPS).astype(F32)
    e2 = (i2 - N_GROUPS).astype(F32)
    return jnp.where(lane == 0, e1, jnp.where(lane == 1, e2, jnp.where(lane == 2, w1, jnp.where(lane == 3, w2, 0.0))))


def _pack_halves(x):
    half = x.shape[1] // 2
    hi = lax.bitcast_convert_type(x[:, :half].astype(BF16).astype(F32), jnp.uint32)
    lo = lax.bitcast_convert_type(x[:, half:].astype(BF16).astype(F32), jnp.uint32)
    return hi | (lo >> 16)


def _unpack_halves(w):
    hi = lax.bitcast_convert_type(w & jnp.uint32(0xFFFF0000), F32)
    lo = lax.bitcast_convert_type(w << 16, F32)
    return hi, lo


def _to_token_major(ref, x):
    rows, width = x.shape
    nc = width // LANE
    for c in range(nc):
        ref[pl.ds(c, rows, stride=nc), :] = x[:, c * LANE:(c + 1) * LANE]


def _from_token_major(ref, rows):
    nc = ref.shape[0] // rows
    return [ref[pl.ds(c, rows, stride=nc), :] for c in range(nc)]


def _out_ln_router_kernel(m_ref, w_ref, x_ref, g_ref, b_ref, rw_ref, rb_ref, x1_ref, x1t_ref, r_ref, eid_ref, *,
                          alpha, sub):
    tm, d = x_ref.shape
    nc = d // (2 * LANE)
    rw_hi = rw_ref[:, :LANE]
    for s0 in range(0, tm, sub):
        rows = slice(s0, s0 + sub)
        mix = _dot(m_ref[rows, :], w_ref[...])
        x1 = _layer_norm(alpha * x_ref[rows, :] + mix, g_ref[...], b_ref[...])
        x1_ref[rows, :] = x1
        _to_token_major(x1t_ref.at[pl.ds(s0 * nc, sub * nc), :], _pack_halves(x1))
        xh = x1.astype(BF16)
        xl = (x1 - xh.astype(F32)).astype(BF16)
        both = _dot(xh, rw_ref[...])
        logits = both[:, :LANE] + both[:, LANE:] + _dot(xl, rw_hi) + rb_ref[...]
        route = _route(logits)
        r_ref[rows, :] = route
        eid_ref[:, rows] = jnp.transpose(route)[:SUBLANE].astype(jnp.int32)


def _out_ln_router(merged, w_out, x, ln_g, ln_b, rw, rb, alpha, tm, sub):
    n, d = x.shape
    tm = min(tm, n)
    sub = min(sub, tm)
    row = lambda w: pl.BlockSpec((tm, w), lambda i: (i, 0))
    full = lambda a: pl.BlockSpec(a.shape, lambda i: (0, 0))
    w_spec = pl.BlockSpec(w_out.shape, lambda i: (0, 0), pipeline_mode=pl.Buffered(1))
    return pl.pallas_call(
        functools.partial(_out_ln_router_kernel, alpha=alpha, sub=sub),
        grid=(n // tm,),
        in_specs=[row(d), w_spec, row(d), full(ln_g), full(ln_b), full(rw), full(rb)],
        out_specs=[row(d), pl.BlockSpec((tm * (d // (2 * LANE)), LANE), lambda i: (i, 0)), row(LANE),
                   pl.BlockSpec((SUBLANE, tm), lambda i: (0, i))],
        out_shape=[jax.ShapeDtypeStruct((n, d), F32), jax.ShapeDtypeStruct((n * (d // (2 * LANE)), LANE), jnp.uint32),
                   jax.ShapeDtypeStruct((n, LANE), F32), jax.ShapeDtypeStruct((SUBLANE, n), jnp.int32)],
        compiler_params=_cparams("parallel"),
        name="out_proj_ln_router",
    )(merged, w_out, x, ln_g, ln_b, rw, rb)


def _moe_kernel(bexp_ref, nact_ref, nval_ref, sbase_ref, tok_ref, dst_ref,
                x_hbm, w1_ref, w3_ref, w2_ref, out_hbm,
                xbuf, xbf, ybuf, w1b, w3b, w2b, sem_in, sem_out, *, bm, nc):
    i = pl.program_id(0)
    nact = nact_ref[0]

    @pl.when((i < nact) & ((i == 0) | (bexp_ref[i] != bexp_ref[jnp.maximum(i - 1, 0)])))
    def _():
        w1b[...] = w1_ref[0, 0].astype(BF16)
        w3b[...] = w3_ref[0, 0].astype(BF16)
        w2b[...] = w2_ref[0, 0].astype(BF16)

    def gather(r, base):
        src = pl.multiple_of(tok_ref[base + r], nc)
        return pltpu.make_async_copy(x_hbm.at[pl.ds(src, nc), :], xbuf.at[r // SUBLANE, :, r % SUBLANE, :], sem_in)

    def scatter(r, base):
        dst = pl.multiple_of(dst_ref[base + r], nc)
        return pltpu.make_async_copy(ybuf.at[r // SUBLANE, :, r % SUBLANE, :], out_hbm.at[pl.ds(dst, nc), :], sem_out)

    def wait_rows(n_tok, sem):
        rows = pl.multiple_of(n_tok * nc, nc)

        @pl.when(n_tok > 0)
        def _():
            pltpu.make_async_copy(x_hbm.at[pl.ds(0, rows), :], out_hbm.at[pl.ds(0, rows), :], sem).wait()

    @pl.when(i == 0)
    def _():
        for r in range(bm):
            gather(r, sbase_ref[0]).start(priority=r % 2)

    @pl.when(i < nact)
    def _():
        has_next = i + 1 < nact
        prev = jnp.maximum(i - 1, 0)
        nv_prev = jnp.where(i >= 1, nval_ref[prev], 0)
        nv_head = jnp.minimum(nv_prev, SUBLANE)
        base_prev = sbase_ref[prev]
        base_next = sbase_ref[i + 1]
        for r in range(bm):
            pl.when(r < nv_prev)(functools.partial(scatter(r, base_prev).start, priority=r % 2))
        wait_rows(jnp.int32(bm), sem_in)
        half = nc * LANE
        for c in range(nc):
            hi, lo = _unpack_halves(xbuf[:, c].reshape(bm, LANE))
            xbf[:, c * LANE:(c + 1) * LANE] = hi.astype(BF16)
            xbf[:, half + c * LANE:half + (c + 1) * LANE] = lo.astype(BF16)
        for r in range(bm):
            pl.when(has_next)(functools.partial(gather(r, base_next).start, priority=r % 2))
        wait_rows(nv_head, sem_out)
        xb = xbf[...]
        h1 = _dot(xb, w1b[...])
        h3 = _dot(xb, w3b[...])
        hh = (h1 * jax.nn.sigmoid(h1) * h3).astype(BF16)
        wait_rows(nv_prev - nv_head, sem_out)
        y = _pack_halves(_dot(hh, w2b[...]))
        for c in range(nc):
            ybuf[:, c] = y[:, c * LANE:(c + 1) * LANE].reshape(bm // SUBLANE, SUBLANE, LANE)

    @pl.when(i == nact)
    def _():
        nv = nval_ref[i - 1]
        base = sbase_ref[i - 1]
        for r in range(bm):
            pl.when(r < nv)(functools.partial(scatter(r, base).start, priority=r % 2))
        wait_rows(nv, sem_out)


def _moe_experts(x1t, eid_t, w1, w3, w2, layer, bm):
    _, ne, d, de = w1.shape
    nc = d // (2 * LANE)
    n = x1t.shape[0] // nc
    a = 2 * n
    nb = a // bm + ne
    eid = eid_t[:2].reshape(a)
    order = jnp.sort(eid * a + jnp.arange(a, dtype=jnp.int32)) % a
    experts = jnp.arange(ne, dtype=jnp.int32)
    counts = jnp.sum((eid[None, :] == experts[:, None]).astype(jnp.int32), axis=1)
    padded = (counts + bm - 1) // bm * bm
    upto = experts[None, :] <= experts[:, None]
    start = jnp.sum(jnp.where(upto, counts[None, :], 0), axis=1) - counts
    pend = jnp.sum(jnp.where(upto, padded[None, :], 0), axis=1)
    pstart = pend - padded
    blk = jnp.arange(nb, dtype=jnp.int32)
    blk_start = blk * bm
    n_active = pend[ne - 1] // bm
    active = blk < n_active
    blk_exp = jnp.minimum(jnp.sum((pend[None, :] <= blk_start[:, None]).astype(jnp.int32), axis=1), ne - 1)
    mine = blk_exp[:, None] == experts[None, :]
    pick = lambda v: jnp.sum(jnp.where(mine, v[None, :], 0), axis=1)
    offset = blk_start - pick(pstart)
    blk_valid = jnp.where(active, jnp.clip(pick(counts) - offset, 0, bm), 0).astype(jnp.int32)
    blk_base = jnp.where(active, pick(start) + offset, 0).astype(jnp.int32)
    blk_exp = jnp.where(active, blk_exp, jnp.max(jnp.where(active, blk_exp, 0))).astype(jnp.int32)
    tail = jnp.zeros((bm,), jnp.int32)
    row_tok = jnp.concatenate([jnp.where(order >= n, order - n, order) * nc, tail])
    row_dst = jnp.concatenate([order * nc, tail])

    grid_spec = pltpu.PrefetchScalarGridSpec(
        num_scalar_prefetch=6,
        grid=(nb,),
        in_specs=[pl.BlockSpec(memory_space=pl.ANY),
                  pl.BlockSpec((1, 1, d, de), lambda i, be, *_: (layer, be[i], 0, 0)),
                  pl.BlockSpec((1, 1, d, de), lambda i, be, *_: (layer, be[i], 0, 0)),
                  pl.BlockSpec((1, 1, de, d), lambda i, be, *_: (layer, be[i], 0, 0))],
        out_specs=pl.BlockSpec(memory_space=pl.ANY),
        scratch_shapes=[pltpu.VMEM((bm // SUBLANE, nc, SUBLANE, LANE), jnp.uint32), pltpu.VMEM((bm, d), BF16),
                        pltpu.VMEM((bm // SUBLANE, nc, SUBLANE, LANE), jnp.uint32),
                        pltpu.VMEM((d, de), BF16), pltpu.VMEM((d, de), BF16), pltpu.VMEM((de, d), BF16),
                        pltpu.SemaphoreType.DMA(()), pltpu.SemaphoreType.DMA(())],
    )
    return pl.pallas_call(
        functools.partial(_moe_kernel, bm=bm, nc=nc),
        grid_spec=grid_spec,
        out_shape=jax.ShapeDtypeStruct((a * nc, LANE), jnp.uint32),
        compiler_params=_cparams("arbitrary"),
        name="moe_experts",
    )(blk_exp, n_active.reshape(1).astype(jnp.int32), blk_valid, blk_base, row_tok, row_dst, x1t, w1, w3, w2)


def _combine_ln_kernel(x_ref, y0_ref, y1_ref, r_ref, g_ref, b_ref, o_ref, obf_ref, *, alpha):
    rows = x_ref.shape[0]
    gate0 = r_ref[:, 2:3]
    gate1 = r_ref[:, 3:4]
    his, los = [], []
    for w0, w1 in zip(_from_token_major(y0_ref, rows), _from_token_major(y1_ref, rows)):
        hi0, lo0 = _unpack_halves(w0)
        hi1, lo1 = _unpack_halves(w1)
        his.append(gate0 * hi0 + gate1 * hi1)
        los.append(gate0 * lo0 + gate1 * lo1)
    y = jnp.concatenate(his + los, axis=-1)
    x2 = _layer_norm(alpha * x_ref[...] + y, g_ref[...], b_ref[...])
    o_ref[...] = x2
    obf_ref[...] = x2.astype(BF16)


def _combine_ln(x1, y2, route, ln_g, ln_b, alpha, tm):
    n, d = x1.shape
    tm = min(tm, n)
    steps = n // tm
    nc = d // (2 * LANE)
    return pl.pallas_call(
        functools.partial(_combine_ln_kernel, alpha=alpha),
        grid=(steps,),
        in_specs=[pl.BlockSpec((tm, d), lambda i: (i, 0)), pl.BlockSpec((tm * nc, LANE), lambda i: (i, 0)),
                  pl.BlockSpec((tm * nc, LANE), lambda i: (steps + i, 0)),
                  pl.BlockSpec((tm, LANE), lambda i: (i, 0)),
                  pl.BlockSpec((1, d), lambda i: (0, 0)), pl.BlockSpec((1, d), lambda i: (0, 0))],
        out_specs=[pl.BlockSpec((tm, d), lambda i: (i, 0)), pl.BlockSpec((tm, d), lambda i: (i, 0))],
        out_shape=[jax.ShapeDtypeStruct((n, d), F32), jax.ShapeDtypeStruct((n, d), BF16)],
        compiler_params=_cparams("parallel"),
        name="combine_ln",
    )(x1, y2, y2, route, ln_g, ln_b)


def _rot_half_cols(w):
    half = w.shape[-1] // 2
    return jnp.concatenate([-w[..., half:], w[..., :half]], axis=-1)


def _rope_cos_sin(positions, dim):
    inv = 1.0 / (ROPE_BASE ** (jnp.arange(0, dim, 2, dtype=F32) / dim))
    ang = positions.astype(F32)[..., None] * inv
    return jnp.cos(ang), jnp.sin(ang)


def kernel(x, positions, w_in, mla_q_norm, mla_w_q_b, mla_kv_norm, mla_w_kv_b, w_branch, w_out, ln1_g, ln1_b, router_group_w, router_group_b, router_expert_w, router_expert_b, expert_w1, expert_w3, expert_w2, ln2_g, ln2_b):
    batch, seq, d = x.shape
    depth = w_in.shape[0]
    n = batch * seq
    alpha = (2 * depth) ** 0.25
    n_main = (CB_MQ + 2 * MLA_LORA // LANE) * LANE

    cr, sr = _rope_cos_sin(positions, HEAD_W)
    cos_r = jnp.concatenate([cr, cr], axis=-1)
    sin_r = jnp.concatenate([-sr, sr], axis=-1)
    cm, sm = _rope_cos_sin(positions, MLA_ROPE)
    q_scale = (MLA_NOPE + MLA_ROPE) ** -0.5
    tq_tab = q_scale * jnp.concatenate([jnp.ones((batch, seq, MLA_NOPE), F32), cm, cm, sm, sm], axis=-1)
    pad = jnp.zeros((batch, seq, LANE - MLA_ROPE), F32)
    tk_tab = jnp.concatenate([cm, cm, pad, sm, sm, pad], axis=-1)

    n_tail = w_in.shape[-1] - n_main
    tail_pad = -n_tail % (2 * LANE)
    tail_blocks = (n_tail + tail_pad) // LANE
    tail_tn = LANE * max(k for k in range(2, 11, 2) if tail_blocks % k == 0)
    col = jnp.arange(n_main) // LANE
    col_scale = jnp.where((col >= CB_SQ) & (col < CB_SK), LOG2E * HEAD_W ** -0.5,
                          jnp.where((col >= CB_RK) & (col < CB_RV), HEAD_W ** -0.5, 1.0)).astype(F32)[None, :]

    xf = x.reshape(n, d)
    xb = xf.astype(BF16)
    for l in range(depth):
        w_main = (w_in[l][:, :n_main] * col_scale).astype(BF16)
        w_tail = jnp.pad(w_in[l][:, n_main:], ((0, 0), (0, tail_pad))).astype(BF16)
        wq =mla_w_q_b[l].reshape(MLA_LORA, N_HEADS, MLA_NOPE + MLA_ROPE)
        wq_pe = wq[..., MLA_NOPE:]
        wq = jnp.concatenate([wq[..., :MLA_NOPE], wq_pe, _rot_half_cols(wq_pe)], axis=-1)
        wq = wq.reshape(MLA_LORA, N_HEADS * 2 * LANE).astype(BF16)
        wkv = mla_w_kv_b[l].astype(BF16)
        rw = jnp.concatenate([router_group_w[l], router_expert_w[l],
                              jnp.zeros((d, LANE - N_GROUPS - N_EXPERTS), F32)], axis=-1)
        rw_hi = rw.astype(BF16)
        rw_cat = jnp.concatenate([rw_hi, (rw - rw_hi.astype(F32)).astype(BF16)], axis=-1)
        rb =jnp.concatenate([router_group_b[l], router_expert_b[l],
                              jnp.zeros((LANE - N_GROUPS - N_EXPERTS,), F32)]).reshape(1, LANE)

        p = _proj(xb, w_main, batch, seq, tm=2048, tn=1024)
        pg = _proj(xb, w_tail, batch, seq, tm=2048, tn=tail_tn)
        ret = _retention(p, cos_r, sin_r, rows_per_step=512)
        sb = _stick_breaking(p, tq=256, hp=4)
        q, kn, v, kpe = _mla_prep(p, pg, 0, tq_tab, tk_tab, mla_q_norm[l].reshape(1, -1),
                                  mla_kv_norm[l].reshape(1, -1), wq, wkv, tm=512)
        ml = _mla_attn(q, kn, kpe, v, tq=256, hp=8)
        merged = _merge(ret, sb, ml, pg, MLA_ROPE, w_branch, l, tm=512, tn=1024)
        x1, x1t, route, eid_t = _out_ln_router(merged, w_out[l].astype(BF16), xf, ln1_g[l].reshape(1, d),
                                               ln1_b[l].reshape(1, d), rw_cat, rb, alpha, tm=512, sub=256)
        y2 = _moe_experts(x1t, eid_t, expert_w1, expert_w3, expert_w2, l, MOE_ROWS)
        xf, xb = _combine_ln(x1, y2, route, ln2_g[l].reshape(1, d), ln2_b[l].reshape(1, d), alpha, tm=512)
    return xf.reshape(batch, seq, d)
```
